```python
import math
import jax, jax.numpy as jnp
from jax import lax
import numpy as np

D_MODEL = 1024
BATCH = 8
SEQ = 4096
DEPTH = 4

GRID_W = 64
CTX_LEN = 256
N_MIXERS = 4
EPS = 1e-6
Q_BLOCK = 128
N_MOD = 6

MLA_HEADS = 16
MLA_Q_RANK = 384
MLA_KV_RANK = 256
MLA_NOPE_DIM = 64
MLA_ROPE_DIM = 32
MLA_V_DIM = 64
MLA_QK_DIM = MLA_NOPE_DIM + MLA_ROPE_DIM
ROPE_THETA = 10000.0

S5_WIDTH = D_MODEL // 2
S5_GROUP_CH = 16
S5_GROUPS = S5_WIDTH // S5_GROUP_CH
S5_STATE = 64
S5_DT_MIN = 1e-3
S5_DT_MAX = 1e-1

HG_HEAD_DIM = 128
HG_HEADS = D_MODEL // HG_HEAD_DIM
HG_CHUNK = 64

NA_HEADS = 16
NA_HEAD_DIM = D_MODEL // NA_HEADS
NA_KH = 8
NA_KW = 16

FFN_DIM = ((8 * D_MODEL // 3 + 255) // 256) * 256

kernel_name = "hybrid_interleaved_diffusion_trunk"

F32 = jnp.float32


def rms_norm(x, g):
    xf = x.astype(F32)
    y = xf * lax.rsqrt(jnp.mean(xf * xf, axis=-1, keepdims=True) + EPS)
    return (y * g.astype(F32)).astype(x.dtype)


def modulate(x, shift, scale):
    return x * (1 + scale) + shift


def _flip(t):
    return jnp.flip(t, axis=1)


def _ident(t):
    return t


def blocked_attention(q, k, v, scale):
    bsz, lq, h, dk = q.shape
    qb = jnp.moveaxis(q.reshape(bsz, lq // Q_BLOCK, Q_BLOCK, h, dk), 1, 0)

    def attend(qblk):
        s = jnp.einsum("bqhd,bkhd->bhqk", qblk, k, preferred_element_type=F32) * scale
        p = jax.nn.softmax(s, axis=-1).astype(v.dtype)
        return jnp.einsum("bhqk,bkhd->bqhd", p, v)

    o = lax.map(attend, qb)
    return jnp.moveaxis(o, 0, 1).reshape(bsz, lq, h, v.shape[-1])


def rope_rotate(v, pos):
    half = v.shape[-1] // 2
    inv = 1.0 / (ROPE_THETA ** (jnp.arange(half, dtype=F32) / half))
    ang = pos.astype(F32)[:, None] * inv[None, :]
    cos = jnp.cos(ang)[None, :, None, :]
    sin = jnp.sin(ang)[None, :, None, :]
    vf = v.astype(F32)
    v1, v2 = vf[..., :half], vf[..., half:]
    return jnp.concatenate([v1 * cos - v2 * sin, v1 * sin + v2 * cos], axis=-1).astype(v.dtype)


def axial_rope_2d(x):
    t = jnp.arange(x.shape[1])
    a = MLA_ROPE_DIM // 2
    return jnp.concatenate([rope_rotate(x[..., :a], t // GRID_W),
                            rope_rotate(x[..., a:], t % GRID_W)], axis=-1)


def mla_project(u, w_in, q_norm_g, w_q_up, kv_norm_g, w_kv_up):
    bsz, n, _ = u.shape
    lat = u @ w_in
    q_lat = lat[..., :MLA_Q_RANK]
    kv_lat = lat[..., MLA_Q_RANK:MLA_Q_RANK + MLA_KV_RANK]
    k_rope = lat[..., MLA_Q_RANK + MLA_KV_RANK:][:, :, None, :]
    q = (rms_norm(q_lat, q_norm_g) @ w_q_up).reshape(bsz, n, MLA_HEADS, MLA_QK_DIM)
    kv = (rms_norm(kv_lat, kv_norm_g) @ w_kv_up).reshape(bsz, n, MLA_HEADS, MLA_NOPE_DIM + MLA_V_DIM)
    return q[..., :MLA_NOPE_DIM], q[..., MLA_NOPE_DIM:], kv[..., :MLA_NOPE_DIM], k_rope, kv[..., MLA_NOPE_DIM:]


def mla_mixer(u, uc, w_in, q_norm_g, w_q_up, kv_norm_g, w_kv_up, w_out, with_ctx_out):
    bsz, n, _ = u.shape
    qn, qr, kn, kr, v = mla_project(u, w_in, q_norm_g, w_q_up, kv_norm_g, w_kv_up)
    qr, kr = axial_rope_2d(qr), axial_rope_2d(kr)
    q = jnp.concatenate([qn, qr], axis=-1)
    k = jnp.concatenate([kn, jnp.broadcast_to(kr, qr.shape)], axis=-1)
    qnc, qrc, knc, krc, vc = mla_project(uc, w_in, q_norm_g, w_q_up, kv_norm_g, w_kv_up)
    kc = jnp.concatenate([knc, jnp.broadcast_to(krc, qrc.shape)], axis=-1)
    scale = MLA_QK_DIM ** -0.5
    o = blocked_attention(q, jnp.concatenate([kc, k], axis=1), jnp.concatenate([vc, v], axis=1), scale)
    y = o.reshape(bsz, n, MLA_HEADS * MLA_V_DIM) @ w_out
    yc = None
    if with_ctx_out:
        oc = blocked_attention(jnp.concatenate([qnc, qrc], axis=-1), kc, vc, scale)
        yc = oc.reshape(bsz, uc.shape[1], MLA_HEADS * MLA_V_DIM) @ w_out
    return y, yc


def s5_discretize(lam_re, lam_im, log_dt, b_re, b_im):
    lam = lax.complex(lam_re.astype(F32), lam_im.astype(F32))
    dt = jnp.exp(log_dt.astype(F32))[:, None]
    lam_bar = jnp.exp(lam * dt)
    b_mat = lax.complex(b_re.astype(F32), b_im.astype(F32))
    b_bar = ((lam_bar - 1.0) / lam)[..., None] * b_mat
    return lam_bar, b_bar


def diagonal_scan(lam_bar, bu, h0):
    bu = bu.at[:, 0].add(lam_bar * h0)
    a = jnp.broadcast_to(lam_bar, (1,) + bu.shape[1:])

    def combine(left, right):
        a_l, b_l = left
        a_r, b_r = right
        return a_r * a_l, a_r * b_l + b_r

    return lax.associative_scan(combine, (a, bu), axis=1)[1]


def s5_mixer(u, uc, w_in, lam_re, lam_im, log_dt, b_re, b_im, c_re, c_im, d_skip, w_glu, with_ctx_out):
    z = (u @ w_in).astype(F32)
    zc = (uc @ w_in).astype(F32)

    def grp(t):
        return t.reshape(t.shape[0], t.shape[1], S5_GROUPS, S5_GROUP_CH)

    dsk = d_skip.astype(F32)
    y = z * dsk
    yc = zc * dsk
    for d in range(2):
        fl = _flip if d == 1 else _ident
        lam_bar, b_bar = s5_discretize(lam_re[d], lam_im[d], log_dt[d], b_re[d], b_im[d])
        c_mat = lax.complex(c_re[d].astype(F32), c_im[d].astype(F32))
        h0 = jnp.zeros((zc.shape[0], S5_GROUPS, S5_STATE), jnp.complex64)
        hc = diagonal_scan(lam_bar, jnp.einsum("gnc,blgc->blgn", b_bar, grp(fl(zc))), h0)
        h = diagonal_scan(lam_bar, jnp.einsum("gnc,blgc->blgn", b_bar, grp(fl(z))), hc[:, -1])
        y = y + fl(jnp.real(jnp.einsum("gcn,blgn->blgc", c_mat, h))).reshape(z.shape)
        if with_ctx_out:
            yc = yc + fl(jnp.real(jnp.einsum("gcn,blgn->blgc", c_mat, hc))).reshape(zc.shape)

    def glu(t):
        a, g = jnp.split(jax.nn.gelu(t).astype(u.dtype) @ w_glu, 2, axis=-1)
        return a * jax.nn.sigmoid(g)

    return glu(y), (glu(yc) if with_ctx_out else None)


def gla_chunkwise(q, k, v, log_f, s0):
    bsz, n, h, dk = q.shape
    dv = v.shape[-1]
    nc = n // HG_CHUNK

    def blk(t):
        return t.reshape(bsz, nc, HG_CHUNK, h, t.shape[-1])

    q, k, v, log_f = blk(q), blk(k), blk(v), blk(log_f)
    b = jnp.cumsum(log_f, axis=2)
    b_last = b[:, :, -1:]
    q_in = q * jnp.exp(b)
    k_in = k * jnp.exp(-b)
    k_out = k * jnp.exp(b_last - b)
    att = jnp.einsum("bcthd,bcshd->bchts", q_in, k_in)
    incl = jnp.tril(jnp.ones((HG_CHUNK, HG_CHUNK), dtype=bool))
    att = jnp.where(incl, att, 0.0)
    o = jnp.einsum("bchts,bcshv->bcthv", att, v)
    ds = jnp.einsum("bcshd,bcshv->cbhdv", k_out, v)
    decay = jnp.moveaxis(jnp.exp(b_last[:, :, 0]), 1, 0)

    def step(s, inp):
        dec, d_s = inp
        return dec[..., None] * s + d_s, s

    s_final, s_start = lax.scan(step, s0, (decay, ds))
    o = o + jnp.einsum("bcthd,cbhdv->bcthv", q_in, s_start)
    return o.reshape(bsz, n, h, dv), s_final


def hgrn2_mixer(u, uc, w_in, lower_bound, norm_g, w_out, with_ctx_out):
    lb = lower_bound.astype(F32).reshape(HG_HEADS, HG_HEAD_DIM)

    def prep(t):
        z = (t @ w_in).astype(F32).reshape(t.shape[0], t.shape[1], 5, HG_HEADS, HG_HEAD_DIM)
        q = jax.nn.silu(z[:, :, 0]) * HG_HEAD_DIM ** -0.5
        forget = lb + (1.0 - lb) * jax.nn.sigmoid(z[:, :, 1:3])
        return q, 1.0 - forget, jnp.log(forget), z[:, :, 3], z[:, :, 4]

    q, k, log_f, v, g = prep(u)
    qc, kc, log_fc, vc, gc = prep(uc)
    o = jnp.zeros_like(v)
    oc = jnp.zeros_like(vc)
    for d in range(2):
        fl = _flip if d == 1 else _ident
        s0 = jnp.zeros((uc.shape[0], HG_HEADS, HG_HEAD_DIM, HG_HEAD_DIM), F32)
        oc_d, s_ctx = gla_chunkwise(fl(qc), fl(kc[:, :, d]), fl(vc), fl(log_fc[:, :, d]), s0)
        o_d, _ = gla_chunkwise(fl(q), fl(k[:, :, d]), fl(v), fl(log_f[:, :, d]), s_ctx)
        o = o + fl(o_d)
        if with_ctx_out:
            oc = oc + fl(oc_d)

    def readout(o_h, g_h):
        on = o_h * lax.rsqrt(jnp.mean(o_h * o_h, axis=-1, keepdims=True) + EPS)
        shp = (o_h.shape[0], o_h.shape[1], D_MODEL)
        on = on.reshape(shp) * norm_g.astype(F32) * jax.nn.silu(g_h.reshape(shp))
        return on.astype(u.dtype) @ w_out

    return readout(o, g), (readout(oc, gc) if with_ctx_out else None)


def natten_mixer(u, uc, w_qkv, rpb, w_out, with_ctx_out):
    bsz, n, _ = u.shape
    rows = n // GRID_W
    kh = min(NA_KH, rows)
    n_loc = kh * NA_KW

    def heads(t):
        return t.reshape(t.shape[0], t.shape[1], NA_HEADS, NA_HEAD_DIM)

    q, k, v = (heads(t) for t in jnp.split(u @ w_qkv, 3, axis=-1))
    qc, kc, vc = (heads(t) for t in jnp.split(uc @ w_qkv, 3, axis=-1))
    scale = NA_HEAD_DIM ** -0.5

    def grid(t):
        return t.reshape(bsz, rows, GRID_W, NA_HEADS, NA_HEAD_DIM)

    qg, kg, vg = grid(q), grid(k), grid(v)
    col = jnp.arange(GRID_W)
    col_idx = jnp.clip(col - NA_KW // 2, 0, GRID_W - NA_KW)[:, None] + jnp.arange(NA_KW)[None, :]
    col_off = col_idx - col[:, None] + (NA_KW - 1)

    def row_block(r):
        r0 = jnp.clip(r - kh // 2, 0, rows - kh)
        k_win = lax.dynamic_slice_in_dim(kg, r0, kh, axis=1)[:, :, col_idx]
        v_win = lax.dynamic_slice_in_dim(vg, r0, kh, axis=1)[:, :, col_idx]
        q_row = lax.dynamic_index_in_dim(qg, r, axis=1, keepdims=False)
        row_off = r0 + jnp.arange(kh) - r + (NA_KH - 1)
        bias = rpb[:, row_off[:, None, None], col_off[None, :, :]]
        s_loc = jnp.einsum("bwhd,brwchd->bhwrc", q_row, k_win, preferred_element_type=F32) * scale
        s_loc = s_loc + jnp.transpose(bias, (0, 2, 1, 3))[None].astype(F32)
        s_ctx = jnp.einsum("bwhd,bjhd->bhwj", q_row, kc, preferred_element_type=F32) * scale
        s = jnp.concatenate([s_loc.reshape(bsz, NA_HEADS, GRID_W, n_loc), s_ctx], axis=-1)
        p = jax.nn.softmax(s, axis=-1).astype(v.dtype)
        p_loc = p[..., :n_loc].reshape(bsz, NA_HEADS, GRID_W, kh, NA_KW)
        return (jnp.einsum("bhwrc,brwchd->bwhd", p_loc, v_win)
                + jnp.einsum("bhwj,bjhd->bwhd", p[..., n_loc:], vc))

    o = lax.map(row_block, jnp.arange(rows))
    y = jnp.moveaxis(o, 0, 1).reshape(bsz, n, D_MODEL) @ w_out
    yc = None
    if with_ctx_out:
        yc = blocked_attention(qc, kc, vc, scale).reshape(bsz, uc.shape[1], D_MODEL) @ w_out
    return y, yc


def conv_ffn(u, w_up, conv_w, conv_b, w_down):
    h = u @ w_up
    hp = jnp.pad(h, ((0, 0), (1, 1), (0, 0)))
    h = hp[:, :-2] * conv_w[0] + hp[:, 1:-1] * conv_w[1] + hp[:, 2:] * conv_w[2] + conv_b
    a, g = jnp.split(h, 2, axis=-1)
    return (a * jax.nn.silu(g)) @ w_down


def setup_inputs(seed: int = 0) -> dict:
    key = jax.random.key(seed)
    ks = jax.random.split(key, 36)

    def nrm(i, shape, s):
        return jax.random.normal(ks[i], shape, F32) * s

    def gain(i, shape):
        return 1.0 + nrm(i, shape, 0.02)

    na, nb, nc, nd = (len(range(kind, DEPTH, N_MIXERS)) for kind in range(N_MIXERS))
    n_idx = jnp.arange(S5_STATE, dtype=F32)
    d = D_MODEL
    return {
        "x": nrm(0, (BATCH, SEQ, d), 1.0),
        "c": nrm(1, (BATCH, d), 1.0),
        "ctx": nrm(2, (BATCH, CTX_LEN, d), 1.0),
        "c_ctx": nrm(3, (d,), 1.0),
        "ada_w": nrm(4, (DEPTH, d, N_MOD * d), 0.5 * d ** -0.5),
        "ada_b": nrm(5, (DEPTH, N_MOD * d), 0.02),
        "norm1_g": gain(6, (DEPTH, d)),
        "norm2_g": gain(7, (DEPTH, d)),
        "mla_w_in": nrm(8, (na, d, MLA_Q_RANK + MLA_KV_RANK + MLA_ROPE_DIM), d ** -0.5),
        "mla_q_norm_g": gain(9, (na, MLA_Q_RANK)),
        "mla_w_q_up": nrm(10, (na, MLA_Q_RANK, MLA_HEADS * MLA_QK_DIM), MLA_Q_RANK ** -0.5),
        "mla_kv_norm_g": gain(11, (na, MLA_KV_RANK)),
        "mla_w_kv_up": nrm(12, (na, MLA_KV_RANK, MLA_HEADS * (MLA_NOPE_DIM + MLA_V_DIM)), MLA_KV_RANK ** -0.5),
        "mla_w_out": nrm(13, (na, MLA_HEADS * MLA_V_DIM, d), (MLA_HEADS * MLA_V_DIM) ** -0.5),
        "s5_w_in": nrm(14, (nb, d, S5_WIDTH), d ** -0.5),
        "s5_lambda_re": -0.5 + nrm(15, (nb, 2, S5_GROUPS, S5_STATE), 0.01),
        "s5_lambda_im": math.pi * n_idx + nrm(16, (nb, 2, S5_GROUPS, S5_STATE), 0.01),
        "s5_log_dt": jax.random.uniform(ks[17], (nb, 2, S5_GROUPS), F32,
                                        minval=math.log(S5_DT_MIN), maxval=math.log(S5_DT_MAX)),
        "s5_b_re": nrm(18, (nb, 2, S5_GROUPS, S5_STATE, S5_GROUP_CH), (2 * S5_GROUP_CH) ** -0.5),
        "s5_b_im": nrm(19, (nb, 2, S5_GROUPS, S5_STATE, S5_GROUP_CH), (2 * S5_GROUP_CH) ** -0.5),
        "s5_c_re": nrm(20, (nb, 2, S5_GROUPS, S5_GROUP_CH, S5_STATE), S5_STATE ** -0.5),
        "s5_c_im": nrm(21, (nb, 2, S5_GROUPS, S5_GROUP_CH, S5_STATE), S5_STATE ** -0.5),
        "s5_d": nrm(22, (nb, S5_WIDTH), 1.0),
        "s5_w_glu": nrm(23, (nb, S5_WIDTH, 2 * d), S5_WIDTH ** -0.5),
        "hg_w_in": nrm(24, (nc, d, 5 * d), d ** -0.5),
        "hg_lower_bound": nrm(25, (DEPTH, HG_HEADS * HG_HEAD_DIM), 0.1),
        "hg_norm_g": gain(26, (nc, d)),
        "hg_w_out": nrm(27, (nc, d, d), d ** -0.5),
        "na_w_qkv": nrm(28, (nd, d, 3 * d), d ** -0.5),
        "na_rpb": nrm(29, (nd, NA_HEADS, 2 * NA_KH - 1, 2 * NA_KW - 1), 0.1),
        "na_w_out": nrm(30, (nd, d, d), d ** -0.5),
        "ffn_w_up": nrm(31, (DEPTH, d, 2 * FFN_DIM), d ** -0.5),
        "ffn_conv_w": nrm(32, (DEPTH, 3, 2 * FFN_DIM), 3 ** -0.5),
        "ffn_conv_b": nrm(33, (DEPTH, 2 * FFN_DIM), 0.02),
        "ffn_w_down": nrm(34, (DEPTH, FFN_DIM, d), FFN_DIM ** -0.5),
        "final_g": gain(35, (d,)),
    }


def reference(x, c, ctx, c_ctx, ada_w, ada_b, norm1_g, norm2_g,
              mla_w_in, mla_q_norm_g, mla_w_q_up, mla_kv_norm_g, mla_w_kv_up, mla_w_out,
              s5_w_in, s5_lambda_re, s5_lambda_im, s5_log_dt, s5_b_re, s5_b_im, s5_c_re, s5_c_im,
              s5_d, s5_w_glu,
              hg_w_in, hg_lower_bound, hg_norm_g, hg_w_out,
              na_w_qkv, na_rpb, na_w_out,
              ffn_w_up, ffn_conv_w, ffn_conv_b, ffn_w_down,
              final_g):
    lb_cum = jnp.cumsum(jax.nn.softmax(hg_lower_bound.astype(F32), axis=0), axis=0)
    lower_bounds = lb_cum - lb_cum[0]
    cond = jax.nn.silu(c)
    cond_ctx = jax.nn.silu(c_ctx)
    h_ctx = ctx
    for i in range(DEPTH):
        kind, j = i % N_MIXERS, i // N_MIXERS
        with_ctx_out = i < DEPTH - 1
        mod = cond @ ada_w[i] + ada_b[i]
        mod_c = cond_ctx @ ada_w[i] + ada_b[i]
        sh1, sc1, g1, sh2, sc2, g2 = jnp.split(mod[:, None, :], N_MOD, axis=-1)
        csh1, csc1, cg1, csh2, csc2, cg2 = jnp.split(mod_c, N_MOD, axis=-1)
        u = modulate(rms_norm(x, norm1_g[i]), sh1, sc1)
        uc = modulate(rms_norm(h_ctx, norm1_g[i]), csh1, csc1)
        if kind == 0:
            y, yc = mla_mixer(u, uc, mla_w_in[j], mla_q_norm_g[j], mla_w_q_up[j], mla_kv_norm_g[j],
                              mla_w_kv_up[j], mla_w_out[j], with_ctx_out)
        elif kind == 1:
            y, yc = s5_mixer(u, uc, s5_w_in[j], s5_lambda_re[j], s5_lambda_im[j], s5_log_dt[j],
                             s5_b_re[j], s5_b_im[j], s5_c_re[j], s5_c_im[j], s5_d[j], s5_w_glu[j],
                             with_ctx_out)
        elif kind == 2:
            y, yc = hgrn2_mixer(u, uc, hg_w_in[j], lower_bounds[i], hg_norm_g[j], hg_w_out[j], with_ctx_out)
        else:
            y, yc = natten_mixer(u, uc, na_w_qkv[j], na_rpb[j], na_w_out[j], with_ctx_out)
        x = x + g1 * y
        x = x + g2 * conv_ffn(modulate(rms_norm(x, norm2_g[i]), sh2, sc2),
                              ffn_w_up[i], ffn_conv_w[i], ffn_conv_b[i], ffn_w_down[i])
        if with_ctx_out:
            h_ctx = h_ctx + cg1 * yc
            h_ctx = h_ctx + cg2 * conv_ffn(modulate(rms_norm(h_ctx, norm2_g[i]), csh2, csc2),
                                           ffn_w_up[i], ffn_conv_w[i], ffn_conv_b[i], ffn_w_down[i])
    return rms_norm(x, final_g)
```

```python
import functools
import math

import jax
import jax.numpy as jnp
import numpy as np
from jax import lax
from jax.experimental import pallas as pl
from jax.experimental.pallas import tpu as pltpu

F32 = jnp.float32
BF16 = jnp.bfloat16
HI = lax.Precision.HIGHEST

EPS = 1e-6
GRID_W = 64
ROPE_THETA = 10000.0
LOG2E = math.log2(math.e)

LANES = 128
VMEM_LIMIT_BYTES = 56 * 1024 * 1024
TM = 256

MLA_HEADS = 16
MLA_Q_RANK = 384
MLA_KV_RANK = 256
MLA_NOPE = 64
MLA_ROPE = 32
MLA_V = 64
MLA_QK = MLA_NOPE + MLA_ROPE

S5_GROUP_CH = 16
S5_STATE = 64
S5_T = 16

HG_HEAD_DIM = 128
HG_CHUNK = 64

NA_HEADS = 16
NA_HEAD_DIM = 64
NA_KH = 8
NA_KW = 16
NEG_BIG = -1e30


def _cparams(*sem):
    return pltpu.CompilerParams(dimension_semantics=sem, vmem_limit_bytes=VMEM_LIMIT_BYTES)


def _seg(ncb):
    return lambda i: jnp.where(i >= ncb, 1, 0)


def _norm_mod(x, g, sh, sc):
    ms = jnp.mean(x * x, axis=-1, keepdims=True)
    return (x * lax.rsqrt(ms + EPS) * g) * (1.0 + sc) + sh


def _sigmoid(x):
    return 1.0 / (1.0 + jnp.exp(-x))


def _dot(a, b):
    return jnp.dot(a, b, preferred_element_type=F32)


def _dot_nt(a, b):
    return lax.dot_general(a, b, (((1,), (1,)), ((), ())), preferred_element_type=F32)


def _dot_tn(a, b):
    return lax.dot_general(a, b, (((0,), (0,)), ((), ())), preferred_element_type=F32)


def _ada_kernel(cond_ref, w_ref, b_ref, o_ref):
    cond = cond_ref[...]
    a = (cond * _sigmoid(cond)).astype(BF16)
    o_ref[0] = _dot(a, w_ref[0].astype(BF16)) + b_ref[0]


def _ada_mod(cond_rows, ada_w, ada_b):
    depth, d, n = ada_w.shape
    rows = cond_rows.shape[0]
    tn = 1536
    return pl.pallas_call(
        _ada_kernel,
        grid=(depth, n // tn),
        in_specs=[
            pl.BlockSpec((rows, d), lambda i, j: (0, 0)),
            pl.BlockSpec((1, d, tn), lambda i, j: (i, 0, j)),
            pl.BlockSpec((1, 1, tn), lambda i, j: (i, 0, j)),
        ],
        out_specs=pl.BlockSpec((1, rows, tn), lambda i, j: (i, 0, j)),
        out_shape=jax.ShapeDtypeStruct((depth, rows, n), F32),
        compiler_params=_cparams("parallel", "parallel"),
        name="ada_mod",
    )(cond_rows, ada_w, ada_b.reshape(depth, 1, n))


def _nmm_kernel(x_ref, g_ref, sh_ref, sc_ref, w_ref, cs_ref, o_ref, *, tn):
    u = _norm_mod(x_ref[0], g_ref[...], sh_ref[0, 0], sc_ref[0, 0]).astype(BF16)
    n = w_ref.shape[1]
    for j in range(n // tn):
        sl = slice(j * tn, (j + 1) * tn)
        o_ref[0, :, sl] = (_dot(u, w_ref[:, sl]) * cs_ref[:, sl]).astype(o_ref.dtype)


def _norm_mod_matmul(hs, g, sh, sc, w, col_scale, *, ncb, out_dtype, tn):
    bsz, lt, d = hs.shape
    n = w.shape[1]
    seg = _seg(ncb)
    return pl.pallas_call(
        functools.partial(_nmm_kernel, tn=tn),
        grid=(bsz, lt // TM),
        in_specs=[
            pl.BlockSpec((1, TM, d), lambda b, i: (b, i, 0)),
            pl.BlockSpec((1, d), lambda b, i: (0, 0)),
            pl.BlockSpec((1, 1, 1, d), lambda b, i: (b, seg(i), 0, 0)),
            pl.BlockSpec((1, 1, 1, d), lambda b, i: (b, seg(i), 0, 0)),
            pl.BlockSpec((d, n), lambda b, i: (0, 0)),
            pl.BlockSpec((1, n), lambda b, i: (0, 0)),
        ],
        out_specs=pl.BlockSpec((1, TM, n), lambda b, i: (b, i, 0)),
        out_shape=jax.ShapeDtypeStruct((bsz, lt, n), out_dtype),
        compiler_params=_cparams("parallel", "parallel"),
        name="norm_mod_matmul",
    )(hs, g.reshape(1, d), sh, sc, w, col_scale)


def _mgr_kernel(a_ref, w_ref, x_ref, gate_ref, o_ref):
    o_ref[0] = x_ref[0] + gate_ref[0, 0] * _dot(a_ref[0], w_ref[...])


def _matmul_gate_residual(a, w, hs, gate, *, ncb):
    bsz, lt, d = hs.shape
    k = a.shape[2]
    seg = _seg(ncb)
    return pl.pallas_call(
        _mgr_kernel,
        grid=(bsz, lt // TM),
        in_specs=[
            pl.BlockSpec((1, TM, k), lambda b, i: (b, i, 0)),
            pl.BlockSpec((k, d), lambda b, i: (0, 0)),
            pl.BlockSpec((1, TM, d), lambda b, i: (b, i, 0)),
            pl.BlockSpec((1, 1, 1, d), lambda b, i: (b, seg(i), 0, 0)),
        ],
        out_specs=pl.BlockSpec((1, TM, d), lambda b, i: (b, i, 0)),
        out_shape=jax.ShapeDtypeStruct((bsz, lt, d), F32),
        compiler_params=_cparams("parallel", "parallel"),
        name="matmul_gate_residual",
    )(a, w, hs, gate)


FFN_HALO = 16
FFN_CHUNK = 256


def _ffn_kernel(x_ref, xp_ref, xn_ref, g_ref, sh_ref, sc_ref, gate_ref, wup_ref, cw_ref, cb_ref,
                wdn_ref, o_ref, u_scr, ha_scr, hg_scr, acc_scr, *, ncb, nb, nch):
    i = pl.program_id(1)
    g = g_ref[...]
    sh = sh_ref[0, 0]
    sc = sc_ref[0, 0]
    keep_prev = jnp.where((i == 0) | (i == ncb), 0.0, 1.0)
    keep_next = jnp.where((i == ncb - 1) | (i == nb - 1), 0.0, 1.0)
    u_scr[0:FFN_HALO, :] = (_norm_mod(xp_ref[0], g, sh, sc) * keep_prev).astype(BF16)
    u_scr[FFN_HALO:FFN_HALO + TM, :] = _norm_mod(x_ref[0], g, sh, sc).astype(BF16)
    u_scr[FFN_HALO + TM:, :] = (_norm_mod(xn_ref[0], g, sh, sc) * keep_next).astype(BF16)
    acc_scr[...] = jnp.zeros_like(acc_scr)

    def conv(h_scr, c):
        w = cw_ref[c]
        return (h_scr[FFN_HALO - 1:FFN_HALO - 1 + TM, :] * w[0:1]
                + h_scr[FFN_HALO:FFN_HALO + TM, :] * w[1:2]
                + h_scr[FFN_HALO + 1:FFN_HALO + 1 + TM, :] * w[2:3]
                + cb_ref[c])

    def body(c, carry):
        u = u_scr[...]
        ha_scr[...] = _dot(u, wup_ref[c])
        hg_scr[...] = _dot(u, wup_ref[nch + c])
        a = conv(ha_scr, c)
        gg = conv(hg_scr, nch + c)
        act = (a * (gg * _sigmoid(gg))).astype(BF16)
        acc_scr[...] += _dot(act, wdn_ref[c])
        return carry

    lax.fori_loop(0, nch, body, 0)
    o_ref[0] = x_ref[0] + gate_ref[0, 0] * acc_scr[...]


def _conv_ffn(hs, g, sh, sc, gate, w_up, conv_w, conv_b, w_down, *, ncb):
    bsz, lt, d = hs.shape
    f = w_down.shape[0]
    nch = f // FFN_CHUNK
    nb = lt // TM
    hb = TM // FFN_HALO
    nhalo = lt // FFN_HALO
    seg = _seg(ncb)
    wup = w_up.astype(BF16).reshape(d, 2 * nch, FFN_CHUNK).transpose(1, 0, 2)
    cw = conv_w.reshape(3, 2 * nch, FFN_CHUNK).transpose(1, 0, 2)
    cb = conv_b.reshape(2 * nch, 1, FFN_CHUNK)
    wdn = w_down.astype(BF16).reshape(nch, FFN_CHUNK, d)
    return pl.pallas_call(
        functools.partial(_ffn_kernel, ncb=ncb, nb=nb, nch=nch),
        grid=(bsz, nb),
        in_specs=[
            pl.BlockSpec((1, TM, d), lambda b, i: (b, i, 0)),
            pl.BlockSpec((1, FFN_HALO, d), lambda b, i: (b, jnp.maximum(i * hb - 1, 0), 0)),
            pl.BlockSpec((1, FFN_HALO, d), lambda b, i: (b, jnp.minimum((i + 1) * hb, nhalo - 1), 0)),
            pl.BlockSpec((1, d), lambda b, i: (0, 0)),
            pl.BlockSpec((1, 1, 1, d), lambda b, i: (b, seg(i), 0, 0)),
            pl.BlockSpec((1, 1, 1, d), lambda b, i: (b, seg(i), 0, 0)),
            pl.BlockSpec((1, 1, 1, d), lambda b, i: (b, seg(i), 0, 0)),
            pl.BlockSpec((2 * nch, d, FFN_CHUNK), lambda b, i: (0, 0, 0)),
            pl.BlockSpec((2 * nch, 3, FFN_CHUNK), lambda b, i: (0, 0, 0)),
            pl.BlockSpec((2 * nch, 1, FFN_CHUNK), lambda b, i: (0, 0, 0)),
            pl.BlockSpec((nch, FFN_CHUNK, d), lambda b, i: (0, 0, 0)),
        ],
        out_specs=pl.BlockSpec((1, TM, d), lambda b, i: (b, i, 0)),
        out_shape=jax.ShapeDtypeStruct((bsz, lt, d), F32),
        scratch_shapes=[
            pltpu.VMEM((TM + 2 * FFN_HALO, d), BF16),
            pltpu.VMEM((TM + 2 * FFN_HALO, FFN_CHUNK), F32),
            pltpu.VMEM((TM + 2 * FFN_HALO, FFN_CHUNK), F32),
            pltpu.VMEM((TM, d), F32),
        ],
        compiler_params=_cparams("parallel", "parallel"),
        name="conv_ffn",
    )(hs, hs, hs, g.reshape(1, d), sh, sc, gate, wup, cw, cb, wdn)


def _rope_tables(lc, l):
    half = MLA_ROPE // 4
    inv = 1.0 / (ROPE_THETA ** (jnp.arange(half, dtype=F32) / half))
    t = jnp.arange(l)
    ang_r = (t // GRID_W).astype(F32)[:, None] * inv[None, :]
    ang_c = (t % GRID_W).astype(F32)[:, None] * inv[None, :]
    cos = jnp.concatenate([jnp.cos(ang_r), jnp.cos(ang_r), jnp.cos(ang_c), jnp.cos(ang_c)], axis=-1)
    sin = jnp.concatenate([-jnp.sin(ang_r), jnp.sin(ang_r), -jnp.sin(ang_c), jnp.sin(ang_c)], axis=-1)
    cos_t = jnp.ones((lc + l, LANES), F32).at[lc:, MLA_NOPE:MLA_NOPE + MLA_ROPE].set(cos)
    sin_t = jnp.zeros((lc + l, LANES), F32).at[lc:, MLA_NOPE:MLA_NOPE + MLA_ROPE].set(sin)
    return cos_t, sin_t


def _mla_proj_kernel(x_ref, g_ref, sh_ref, sc_ref, win_ref, qg_ref, wq_ref, kvg_ref, wkn_ref, wv_ref,
                     cos_ref, sin_ref, q_ref, k_ref, v_ref):
    u = _norm_mod(x_ref[0], g_ref[...], sh_ref[0, 0], sc_ref[0, 0]).astype(BF16)
    lat = _dot(u, win_ref[...])
    q_lat = lat[:, :MLA_Q_RANK]
    kv_lat = lat[:, MLA_Q_RANK:MLA_Q_RANK + MLA_KV_RANK]
    kr = lat[:, MLA_Q_RANK + MLA_KV_RANK:]

    def rms(t, gg):
        return (t * lax.rsqrt(jnp.mean(t * t, axis=-1, keepdims=True) + EPS) * gg).astype(BF16)

    qn = rms(q_lat, qg_ref[...])
    kvn = rms(kv_lat, kvg_ref[...])
    cos = cos_ref[...]
    sin = sin_ref[...]
    lane = lax.broadcasted_iota(jnp.int32, (TM, LANES), 1)
    first_half = (lane % (MLA_ROPE // 2)) < (MLA_ROPE // 4)

    def rope(t):
        partner = jnp.where(first_half, pltpu.roll(t, LANES - MLA_ROPE // 4, 1), pltpu.roll(t, MLA_ROPE // 4, 1))
        return t * cos + partner * sin

    kr_rot = rope(kr)
    v_ref[0] = _dot(kvn, wv_ref[...]).astype(BF16)
    qscale = (MLA_QK ** -0.5) * LOG2E
    for h in range(MLA_HEADS):
        sl = slice(h * LANES, (h + 1) * LANES)
        q_ref[0, :, sl] = rope(_dot(qn, wq_ref[:, sl]) * qscale).astype(BF16)
        k_ref[0, :, sl] = (_dot(kvn, wkn_ref[:, sl]) + kr_rot).astype(BF16)


def _mla_attn_kernel(q_ref, k_ref, v_ref, o_ref, *, ncb, lc):
    i = pl.program_id(2)

    def attend(lk):
        q = q_ref[0]
        v = v_ref[0, 0:lk, :]
        outs = []
        for e in range(2):
            s = _dot_nt(q[:, e * LANES:(e + 1) * LANES], k_ref[0, 0:lk, e * LANES:(e + 1) * LANES])
            m = jnp.max(s, axis=-1, keepdims=True)
            p = jnp.exp2(s - m)
            l = jnp.sum(p, axis=-1, keepdims=True)
            outs.append(_dot(p.astype(BF16), v) / l)
        lane = lax.broadcasted_iota(jnp.int32, outs[0].shape, 1)
        o_ref[0] = jnp.where(lane < MLA_V, outs[0], outs[1]).astype(o_ref.dtype)

    @pl.when(i < ncb)
    def _():
        attend(lc)

    @pl.when(i >= ncb)
    def _():
        attend(k_ref.shape[1])


def _mla_layer(hs, g, sh, sc, gate, w_in, q_norm_g, w_q_up, kv_norm_g, w_kv_up, w_out, *, ncb, lc):
    bsz, lt, d = hs.shape
    l = lt - lc
    nh = MLA_HEADS
    seg = _seg(ncb)
    w_in_p = jnp.zeros((d, 768), F32)
    w_in_p = w_in_p.at[:, :MLA_Q_RANK + MLA_KV_RANK].set(w_in[:, :MLA_Q_RANK + MLA_KV_RANK])
    w_in_p = w_in_p.at[:, 640 + MLA_NOPE:640 + MLA_QK].set(w_in[:, MLA_Q_RANK + MLA_KV_RANK:])
    wq = w_q_up.reshape(MLA_Q_RANK, nh, MLA_QK)
    wq = jnp.pad(wq, ((0, 0), (0, 0), (0, LANES - MLA_QK))).reshape(MLA_Q_RANK, nh * LANES)
    wkv = w_kv_up.reshape(MLA_KV_RANK, nh, MLA_NOPE + MLA_V)
    wkn = jnp.pad(wkv[:, :, :MLA_NOPE], ((0, 0), (0, 0), (0, LANES - MLA_NOPE))).reshape(MLA_KV_RANK, nh * LANES)
    wv = wkv[:, :, MLA_NOPE:].reshape(MLA_KV_RANK, nh * MLA_V)
    cos_t, sin_t = _rope_tables(lc, l)

    q, k, v = pl.pallas_call(
        _mla_proj_kernel,
        grid=(bsz, lt // TM),
        in_specs=[
            pl.BlockSpec((1, TM, d), lambda b, i: (b, i, 0)),
            pl.BlockSpec((1, d), lambda b, i: (0, 0)),
            pl.BlockSpec((1, 1, 1, d), lambda b, i: (b, seg(i), 0, 0)),
            pl.BlockSpec((1, 1, 1, d), lambda b, i: (b, seg(i), 0, 0)),
            pl.BlockSpec((d, 768), lambda b, i: (0, 0)),
            pl.BlockSpec((1, MLA_Q_RANK), lambda b, i: (0, 0)),
            pl.BlockSpec((MLA_Q_RANK, nh * LANES), lambda b, i: (0, 0)),
            pl.BlockSpec((1, MLA_KV_RANK), lambda b, i: (0, 0)),
            pl.BlockSpec((MLA_KV_RANK, nh * LANES), lambda b, i: (0, 0)),
            pl.BlockSpec((MLA_KV_RANK, nh * MLA_V), lambda b, i: (0, 0)),
            pl.BlockSpec((TM, LANES), lambda b, i: (i, 0)),
            pl.BlockSpec((TM, LANES), lambda b, i: (i, 0)),
        ],
        out_specs=[
            pl.BlockSpec((1, TM, nh * LANES), lambda b, i: (b, i, 0)),
            pl.BlockSpec((1, TM, nh * LANES), lambda b, i: (b, i, 0)),
            pl.BlockSpec((1, TM, nh * MLA_V), lambda b, i: (b, i, 0)),
        ],
        out_shape=[
            jax.ShapeDtypeStruct((bsz, lt, nh * LANES), BF16),
            jax.ShapeDtypeStruct((bsz, lt, nh * LANES), BF16),
            jax.ShapeDtypeStruct((bsz, lt, nh * MLA_V), BF16),
        ],
        compiler_params=_cparams("parallel", "parallel"),
        name="mla_proj",
    )(hs, g.reshape(1, d), sh, sc, w_in_p.astype(BF16), q_norm_g.reshape(1, -1), wq.astype(BF16),
      kv_norm_g.reshape(1, -1), wkn.astype(BF16), wv.astype(BF16), cos_t, sin_t)

    o = pl.pallas_call(
        functools.partial(_mla_attn_kernel, ncb=ncb, lc=lc),
        grid=(bsz, nh // 2, lt // TM),
        in_specs=[
            pl.BlockSpec((1, TM, 2 * LANES), lambda b, p, i: (b, i, p)),
            pl.BlockSpec((1, lt, 2 * LANES), lambda b, p, i: (b, 0, p)),
            pl.BlockSpec((1, lt, 2 * MLA_V), lambda b, p, i: (b, 0, p)),
        ],
        out_specs=pl.BlockSpec((1, TM, 2 * MLA_V), lambda b, p, i: (b, i, p)),
        out_shape=jax.ShapeDtypeStruct((bsz, lt, nh * MLA_V), BF16),
        compiler_params=_cparams("parallel", "parallel", "parallel"),
        name="mla_attn",
    )(q, k, v)
    return _matmul_gate_residual(o, w_out.astype(BF16), hs, gate, ncb=ncb)


def _s5_matrices(lam_re, lam_im, log_dt, b_re, b_im, c_re, c_im, d_skip):
    t_len = S5_T
    n_grp = lam_re.shape[1]
    ch = S5_GROUP_CH
    n_st = S5_STATE
    dt = jnp.exp(log_dt.astype(F32))[..., None]
    ld_re = lam_re.astype(F32) * dt
    ld_im = lam_im.astype(F32) * dt
    tau = jnp.arange(t_len + 1, dtype=F32)[:, None, None, None]
    mag = jnp.exp(tau * ld_re[None])
    pw_re = mag * jnp.cos(tau * ld_im[None])
    pw_im = mag * jnp.sin(tau * ld_im[None])
    lb_re, lb_im = pw_re[1] - 1.0, pw_im[1]
    den = lam_re * lam_re + lam_im * lam_im
    f_re = (lb_re * lam_re + lb_im * lam_im) / den
    f_im = (lb_im * lam_re - lb_re * lam_im) / den
    bb_re = f_re[..., None] * b_re - f_im[..., None] * b_im
    bb_im = f_re[..., None] * b_im + f_im[..., None] * b_re
    cp_re = c_re[None] * pw_re[:t_len, :, :, None, :] - c_im[None] * pw_im[:t_len, :, :, None, :]
    cp_im = c_re[None] * pw_im[:t_len, :, :, None, :] + c_im[None] * pw_re[:t_len, :, :, None, :]
    taps = (jnp.einsum("tdgon,dgni->tdgoi", cp_re, bb_re, precision=HI)
            - jnp.einsum("tdgon,dgni->tdgoi", cp_im, bb_im, precision=HI))
    s_idx = jnp.arange(t_len)[:, None]
    t_idx = jnp.arange(t_len)[None, :]
    lag_f = t_idx - s_idx
    lag_b = s_idx - t_idx
    kf = jnp.where((lag_f >= 0)[:, :, None, None, None], taps[jnp.clip(lag_f, 0), 0], 0.0)
    kb = jnp.where((lag_b >= 0)[:, :, None, None, None], taps[jnp.clip(lag_b, 0), 1], 0.0)
    k_tot = (kf + kb).transpose(2, 0, 4, 1, 3).reshape(n_grp, t_len * ch, t_len * ch)
    s_ar = jnp.arange(t_len)

    def state_in(pw_r, pw_i, brr, bii):
        re = pw_r[..., None] * brr[None] - pw_i[..., None] * bii[None]
        im = pw_r[..., None] * bii[None] + pw_i[..., None] * brr[None]
        return (re.transpose(1, 0, 3, 2).reshape(n_grp, t_len * ch, n_st),
                im.transpose(1, 0, 3, 2).reshape(n_grp, t_len * ch, n_st))

    inf_re, inf_im = state_in(pw_re[t_len - 1 - s_ar, 0], pw_im[t_len - 1 - s_ar, 0], bb_re[0], bb_im[0])
    inb_re, inb_im = state_in(pw_re[s_ar, 1], pw_im[s_ar, 1], bb_re[1], bb_im[1])

    def state_out(pw_r, pw_i, crr, cii):
        re = crr[None] * pw_r[:, :, None, :] - cii[None] * pw_i[:, :, None, :]
        im = crr[None] * pw_i[:, :, None, :] + cii[None] * pw_r[:, :, None, :]
        return (re.transpose(1, 3, 0, 2).reshape(n_grp, n_st, t_len * ch),
                (-im).transpose(1, 3, 0, 2).reshape(n_grp, n_st, t_len * ch))

    outf_re, outf_im = state_out(pw_re[s_ar + 1, 0], pw_im[s_ar + 1, 0], c_re[0], c_im[0])
    outb_re, outb_im = state_out(pw_re[t_len - s_ar, 1], pw_im[t_len - s_ar, 1], c_re[1], c_im[1])
    odd = (jnp.arange(n_grp) % 2 == 1)[:, None, None]

    def pad_cols(m):
        z = jnp.zeros_like(m)
        return jnp.where(odd, jnp.concatenate([z, m], -1), jnp.concatenate([m, z], -1))

    def pad_rows(m):
        z = jnp.zeros_like(m)
        return jnp.where(odd, jnp.concatenate([z, m], 1), jnp.concatenate([m, z], 1))

    w_in = jnp.concatenate([pad_cols(inf_re), pad_cols(inf_im), pad_cols(inb_re), pad_cols(inb_im)], -1)
    m_out = jnp.stack([pad_rows(outf_re), pad_rows(outf_im), pad_rows(outb_re), pad_rows(outb_im)], 1)
    decay = jnp.stack([pw_re[t_len, 0], pw_im[t_len, 0], pw_re[t_len, 1], pw_im[t_len, 1]], 0)
    decay = decay.reshape(4, 1, n_grp * n_st)
    return w_in.astype(BF16), k_tot.astype(BF16), m_out.astype(BF16), decay


def _s5_in_kernel(z_ref, w_ref, u_ref):
    for k in range(4):
        sl = slice(k * LANES, (k + 1) * LANES)
        u_ref[k] = _dot(z_ref[0], w_ref[0, :, sl]) + _dot(z_ref[1], w_ref[1, :, sl])


def _s5_scan_kernel(u_ref, a_ref, p_ref, *, jc, jt):
    afr, afi, abr, abi = a_ref[0], a_ref[1], a_ref[2], a_ref[3]
    zero = jnp.zeros(p_ref.shape[2:], F32)

    def body(jj, carry):
        fr, fi, br, bi = carry
        jb = jnp.where(jj < jc, jc - 1 - jj, jt - 1 - jj + jc)
        p_ref[0, jj] = fr
        p_ref[1, jj] = fi
        p_ref[2, jb] = br
        p_ref[3, jb] = bi
        nfr = afr * fr - afi * fi + u_ref[0, jj]
        nfi = afr * fi + afi * fr + u_ref[1, jj]
        nbr = abr * br - abi * bi + u_ref[2, jb]
        nbi = abr * bi + abi * br + u_ref[3, jb]
        return nfr, nfi, nbr, nbi

    lax.fori_loop(0, jt, body, (zero, zero, zero, zero))


def _s5_out_kernel(z_ref, p_ref, k_ref, m_ref, y_ref):
    for e in range(2):
        acc = _dot(z_ref[e], k_ref[e])
        for k in range(4):
            acc += _dot(p_ref[k].astype(BF16), m_ref[e, k])
        y_ref[e] = acc


def _s5_glu_kernel(y_ref, z_ref, d_ref, w_ref, x_ref, gate_ref, o_ref):
    d = x_ref.shape[2]
    y = y_ref[0] + z_ref[0] * d_ref[...]
    ge = jax.nn.gelu(y).astype(BF16)
    ag = _dot(ge, w_ref[...])
    o_ref[0] = x_ref[0] + gate_ref[0, 0] * (ag[:, :d] * _sigmoid(ag[:, d:]))


def _s5_layer(hs, g, sh, sc, gate, w_in, lam_re, lam_im, log_dt, b_re, b_im, c_re, c_im, d_skip, w_glu,
              *, ncb, lc):
    bsz, lt, d = hs.shape
    width = w_in.shape[1]
    n_grp = width // S5_GROUP_CH
    jt = lt // S5_T
    jc = lc // S5_T
    rows = jt * bsz
    cols = S5_T * S5_GROUP_CH
    seg = _seg(ncb)
    z = _norm_mod_matmul(hs, g, sh, sc, w_in.astype(BF16), jnp.ones((1, width), F32),
                         ncb=ncb, out_dtype=F32, tn=width)
    w_si, k_tot, m_out, decay = _s5_matrices(lam_re, lam_im, log_dt, b_re, b_im, c_re, c_im, d_skip)
    zg = z.astype(BF16).reshape(bsz, jt, S5_T, n_grp, S5_GROUP_CH).transpose(3, 1, 0, 2, 4)
    zg = zg.reshape(n_grp, rows, cols)
    n_state = n_grp * S5_STATE
    u = pl.pallas_call(
        _s5_in_kernel,
        grid=(n_grp // 2,),
        in_specs=[
            pl.BlockSpec((2, rows, cols), lambda p: (p, 0, 0)),
            pl.BlockSpec((2, cols, 4 * LANES), lambda p: (p, 0, 0)),
        ],
        out_specs=pl.BlockSpec((4, rows, LANES), lambda p: (0, 0, p)),
        out_shape=jax.ShapeDtypeStruct((4, rows, n_state), F32),
        compiler_params=_cparams("parallel"),
        name="s5_state_in",
    )(zg, w_si)
    p_state = pl.pallas_call(
        functools.partial(_s5_scan_kernel, jc=jc, jt=jt),
        grid=(n_state // LANES,),
        in_specs=[
            pl.BlockSpec((4, jt, bsz, LANES), lambda p: (0, 0, 0, p)),
            pl.BlockSpec((4, 1, LANES), lambda p: (0, 0, p)),
        ],
        out_specs=pl.BlockSpec((4, jt, bsz, LANES), lambda p: (0, 0, 0, p)),
        out_shape=jax.ShapeDtypeStruct((4, jt, bsz, n_state), F32),
        compiler_params=_cparams("parallel"),
        name="s5_scan",
    )(u.reshape(4, jt, bsz, n_state), decay)
    yg = pl.pallas_call(
        _s5_out_kernel,
        grid=(n_grp // 2,),
        in_specs=[
            pl.BlockSpec((2, rows, cols), lambda p: (p, 0, 0)),
            pl.BlockSpec((4, rows, LANES), lambda p: (0, 0, p)),
            pl.BlockSpec((2, cols, cols), lambda p: (p, 0, 0)),
            pl.BlockSpec((2, 4, LANES, cols), lambda p: (p, 0, 0, 0)),
        ],
        out_specs=pl.BlockSpec((2, rows, cols), lambda p: (p, 0, 0)),
        out_shape=jax.ShapeDtypeStruct((n_grp, rows, cols), F32),
        compiler_params=_cparams("parallel"),
        name="s5_chunk_out",
    )(zg, p_state.reshape(4, rows, n_state), k_tot, m_out)
    y = yg.reshape(n_grp, jt, bsz, S5_T, S5_GROUP_CH).transpose(2, 1, 3, 0, 4).reshape(bsz, lt, width)
    return pl.pallas_call(
        _s5_glu_kernel,
        grid=(bsz, lt // TM),
        in_specs=[
            pl.BlockSpec((1, TM, width), lambda b, i: (b, i, 0)),
            pl.BlockSpec((1, TM, width), lambda b, i: (b, i, 0)),
            pl.BlockSpec((1, width), lambda b, i: (0, 0)),
            pl.BlockSpec((width, 2 * d), lambda b, i: (0, 0)),
            pl.BlockSpec((1, TM, d), lambda b, i: (b, i, 0)),
            pl.BlockSpec((1, 1, 1, d), lambda b, i: (b, seg(i), 0, 0)),
        ],
        out_specs=pl.BlockSpec((1, TM, d), lambda b, i: (b, i, 0)),
        out_shape=jax.ShapeDtypeStruct((bsz, lt, d), F32),
        compiler_params=_cparams("parallel", "parallel"),
        name="s5_glu",
    )(y, z, d_skip.reshape(1, width), w_glu.astype(BF16), hs, gate)


def _gla_kernel(zq_ref, zf_ref, zv_ref, lb_ref, o_ref, st_scr, *, rev):
    @pl.when(pl.program_id(2) == 0)
    def _():
        st_scr[...] = jnp.zeros_like(st_scr)

    lb = lb_ref[0]
    n = HG_CHUNK
    row = lax.broadcasted_iota(jnp.int32, (n, n), 0)
    col = lax.broadcasted_iota(jnp.int32, (n, n), 1)
    incl = (col >= row) if rev else (col <= row)
    tri = incl.astype(BF16)
    qscale = HG_HEAD_DIM ** -0.5
    nchunks = zq_ref.shape[1] // n
    for cc in range(nchunks):
        c = nchunks - 1 - cc if rev else cc
        sl = slice(c * n, (c + 1) * n)
        zq = zq_ref[0, sl, :]
        forget = lb + (1.0 - lb) * _sigmoid(zf_ref[0, sl, :])
        lf = jnp.log(forget)
        kk = 1.0 - forget
        v = zv_ref[0, sl, :].astype(BF16)
        hi = lf.astype(BF16)
        r1 = lf - hi.astype(F32)
        mid = r1.astype(BF16)
        lo = (r1 - mid.astype(F32)).astype(BF16)
        bcum = _dot(tri, hi) + _dot(tri, mid) + _dot(tri, lo)
        b_last = bcum[0:1, :] if rev else bcum[n - 1:n, :]
        q_in = ((zq * _sigmoid(zq)) * qscale * jnp.exp(bcum)).astype(BF16)
        k_in = (kk * jnp.exp(-bcum)).astype(BF16)
        k_out = (kk * jnp.exp(b_last - bcum)).astype(BF16)
        att = jnp.where(incl, _dot_nt(q_in, k_in), 0.0)
        st = st_scr[...]
        o_ref[0, sl, :] = _dot(att.astype(BF16), v) + _dot_nt(q_in, st.astype(BF16))
        st_scr[...] = st * jnp.exp(b_last) + _dot_tn(v, k_out)


def _hg_out_kernel(of_ref, ob_ref, zg_ref, ng_ref, w_ref, x_ref, gate_ref, o_ref):
    o = of_ref[0] + ob_ref[0]
    gsig = zg_ref[0]
    gsig = gsig * _sigmoid(gsig)
    ng = ng_ref[...]
    parts = []
    for h in range(o.shape[1] // HG_HEAD_DIM):
        sl = slice(h * HG_HEAD_DIM, (h + 1) * HG_HEAD_DIM)
        oh = o[:, sl]
        on = oh * lax.rsqrt(jnp.mean(oh * oh, axis=-1, keepdims=True) + EPS)
        parts.append((on * ng[:, sl] * gsig[:, sl]).astype(BF16))
    a = jnp.concatenate(parts, axis=-1)
    o_ref[0] = x_ref[0] + gate_ref[0, 0] * _dot(a, w_ref[...])


def _hgrn2_layer(hs, g, sh, sc, gate, w_in, lower_bound, norm_g, w_out, *, ncb, lc):
    bsz, lt, d = hs.shape
    nh = d // HG_HEAD_DIM
    nb = lt // TM
    seg = _seg(ncb)
    z = _norm_mod_matmul(hs, g, sh, sc, w_in.astype(BF16), jnp.ones((1, 5 * d), F32),
                         ncb=ncb, out_dtype=F32, tn=1024)
    lb = lower_bound.astype(F32).reshape(nh, 1, HG_HEAD_DIM)
    outs = []
    for direction in range(2):
        rev = direction == 1
        if rev:
            blk = lambda s: jnp.where(s < ncb, ncb - 1 - s, nb - 1 - s + ncb)
        else:
            blk = lambda s: s
        fcol = (1 + direction) * nh
        outs.append(pl.pallas_call(
            functools.partial(_gla_kernel, rev=rev),
            grid=(bsz, nh, nb),
            in_specs=[
                pl.BlockSpec((1, TM, HG_HEAD_DIM), lambda b, h, s: (b, blk(s), h)),
                pl.BlockSpec((1, TM, HG_HEAD_DIM), lambda b, h, s: (b, blk(s), fcol + h)),
                pl.BlockSpec((1, TM, HG_HEAD_DIM), lambda b, h, s: (b, blk(s), 3 * nh + h)),
                pl.BlockSpec((1, 1, HG_HEAD_DIM), lambda b, h, s: (h, 0, 0)),
            ],
            out_specs=pl.BlockSpec((1, TM, HG_HEAD_DIM), lambda b, h, s: (b, blk(s), h)),
            out_shape=jax.ShapeDtypeStruct((bsz, lt, d), F32),
            scratch_shapes=[pltpu.VMEM((HG_HEAD_DIM, HG_HEAD_DIM), F32)],
            compiler_params=_cparams("parallel", "parallel", "arbitrary"),
            name="hgrn2_gla_bwd" if rev else "hgrn2_gla_fwd",
        )(z, z, z, lb))
    return pl.pallas_call(
        _hg_out_kernel,
        grid=(bsz, nb),
        in_specs=[
            pl.BlockSpec((1, TM, d), lambda b, i: (b, i, 0)),
            pl.BlockSpec((1, TM, d), lambda b, i: (b, i, 0)),
            pl.BlockSpec((1, TM, d), lambda b, i: (b, i, 4)),
            pl.BlockSpec((1, d), lambda b, i: (0, 0)),
            pl.BlockSpec((d, d), lambda b, i: (0, 0)),
            pl.BlockSpec((1, TM, d), lambda b, i: (b, i, 0)),
            pl.BlockSpec((1, 1, 1, d), lambda b, i: (b, seg(i), 0, 0)),
        ],
        out_specs=pl.BlockSpec((1, TM, d), lambda b, i: (b, i, 0)),
        out_shape=jax.ShapeDtypeStruct((bsz, lt, d), F32),
        compiler_params=_cparams("parallel", "parallel"),
        name="hgrn2_out",
    )(outs[0], outs[1], z, norm_g.reshape(1, d), w_out.astype(BF16), hs, gate)


def _na_bias_table(rpb):
    vi = jnp.arange(NA_KH)[:, None, None, None]
    w = jnp.arange(GRID_W)[None, :, None, None]
    kr = jnp.arange(NA_KH)[None, None, :, None]
    kc = jnp.arange(GRID_W)[None, None, None, :]
    c0 = jnp.clip(w - NA_KW // 2, 0, GRID_W - NA_KW)
    valid = (kc >= c0) & (kc < c0 + NA_KW)
    row_off = jnp.broadcast_to(kr - vi + (NA_KH - 1), (NA_KH, GRID_W, NA_KH, GRID_W))
    col_off = jnp.broadcast_to(jnp.clip(kc - w + (NA_KW - 1), 0, 2 * NA_KW - 2), row_off.shape)
    bias = rpb[:, row_off, col_off].astype(F32) * LOG2E
    bias = jnp.where(jnp.broadcast_to(valid, row_off.shape)[None], bias, NEG_BIG)
    return bias.transpose(1, 0, 2, 3, 4).reshape(NA_KH, rpb.shape[0], GRID_W, NA_KH * GRID_W)


def _na_kernel(q_ref, k_ref, v_ref, bias_ref, o_ref, *, lc, rows):
    nwin = NA_KH * GRID_W
    o_ref[0, 0:lc, :] = jnp.zeros((lc, LANES), o_ref.dtype)
    kc = k_ref[0, 0:lc, :]
    vc = v_ref[0, 0:lc, :]
    lane = lax.broadcasted_iota(jnp.int32, (GRID_W, LANES), 1)
    lo_half = lane < NA_HEAD_DIM

    def body(r, carry):
        r0 = jnp.clip(r - NA_KH // 2, 0, rows - NA_KH)
        vi = r - r0
        q = q_ref[0, pl.ds(pl.multiple_of(lc + r * GRID_W, GRID_W), GRID_W), :]
        kw = k_ref[0, pl.ds(pl.multiple_of(lc + r0 * GRID_W, GRID_W), nwin), :]
        vw = v_ref[0, pl.ds(pl.multiple_of(lc + r0 * GRID_W, GRID_W), nwin), :]
        outs = []
        for e in range(2):
            qe = jnp.where(lo_half if e == 0 else ~lo_half, q, jnp.zeros_like(q))
            s_loc = _dot_nt(qe, kw) + bias_ref[vi, e]
            s_ctx = _dot_nt(qe, kc)
            m = jnp.maximum(jnp.max(s_loc, axis=-1, keepdims=True), jnp.max(s_ctx, axis=-1, keepdims=True))
            p_loc = jnp.exp2(s_loc - m)
            p_ctx = jnp.exp2(s_ctx - m)
            l = jnp.sum(p_loc, axis=-1, keepdims=True) + jnp.sum(p_ctx, axis=-1, keepdims=True)
            outs.append((_dot(p_loc.astype(BF16), vw) + _dot(p_ctx.astype(BF16), vc)) / l)
        o = jnp.where(lo_half, outs[0], outs[1])
        o_ref[0, pl.ds(pl.multiple_of(lc + r * GRID_W, GRID_W), GRID_W), :] = o.astype(o_ref.dtype)
        return carry

    lax.fori_loop(0, rows, body, 0)


def _natten_layer(hs, g, sh, sc, gate, w_qkv, rpb, w_out, *, ncb, lc):
    bsz, lt, d = hs.shape
    rows = (lt - lc) // GRID_W
    npair = NA_HEADS // 2
    col_scale = jnp.concatenate([jnp.full((1, d), (NA_HEAD_DIM ** -0.5) * LOG2E, F32),
                                 jnp.ones((1, 2 * d), F32)], axis=-1)
    qkv = _norm_mod_matmul(hs, g, sh, sc, w_qkv.astype(BF16), col_scale, ncb=ncb, out_dtype=BF16, tn=1024)
    bias = _na_bias_table(rpb)
    o = pl.pallas_call(
        functools.partial(_na_kernel, lc=lc, rows=rows),
        grid=(npair, bsz),
        in_specs=[
            pl.BlockSpec((1, lt, LANES), lambda p, b: (b, 0, p)),
            pl.BlockSpec((1, lt, LANES), lambda p, b: (b, 0, npair + p)),
            pl.BlockSpec((1, lt, LANES), lambda p, b: (b, 0, 2 * npair + p)),
            pl.BlockSpec((NA_KH, 2, GRID_W, NA_KH * GRID_W), lambda p, b: (0, p, 0, 0)),
        ],
        out_specs=pl.BlockSpec((1, lt, LANES), lambda p, b: (b, 0, p)),
        out_shape=jax.ShapeDtypeStruct((bsz, lt, d), BF16),
        compiler_params=_cparams("parallel", "parallel"),
        name="natten",
    )(qkv, qkv, qkv, bias)
    return _matmul_gate_residual(o, w_out.astype(BF16), hs, gate, ncb=ncb)


def _final_norm_kernel(x_ref, g_ref, o_ref):
    x = x_ref[0]
    o_ref[0] = x * lax.rsqrt(jnp.mean(x * x, axis=-1, keepdims=True) + EPS) * g_ref[...]


def _final_norm(hs, g, *, ncb):
    bsz, lt, d = hs.shape
    nb = lt // TM - ncb
    return pl.pallas_call(
        _final_norm_kernel,
        grid=(bsz, nb),
        in_specs=[
            pl.BlockSpec((1, TM, d), lambda b, i: (b, i + ncb, 0)),
            pl.BlockSpec((1, d), lambda b, i: (0, 0)),
        ],
        out_specs=pl.BlockSpec((1, TM, d), lambda b, i: (b, i, 0)),
        out_shape=jax.ShapeDtypeStruct((bsz, nb * TM, d), F32),
        compiler_params=_cparams("parallel", "parallel"),
        name="final_norm",
    )(hs, g.reshape(1, d))


def _layer_mods(mod_rows, bsz, d):
    mx = mod_rows[:bsz].reshape(bsz, 6, d)
    mc = jnp.broadcast_to(mod_rows[bsz].reshape(1, 6, d), (bsz, 6, d))
    m = jnp.stack([mc, mx], axis=1)
    return [m[:, :, k, None, :] for k in range(6)]


def kernel(x, c, ctx, c_ctx, ada_w, ada_b, norm1_g, norm2_g, mla_w_in, mla_q_norm_g, mla_w_q_up, mla_kv_norm_g, mla_w_kv_up, mla_w_out, s5_w_in, s5_lambda_re, s5_lambda_im, s5_log_dt, s5_b_re, s5_b_im, s5_c_re, s5_c_im, s5_d, s5_w_glu, hg_w_in, hg_lower_bound, hg_norm_g, hg_w_out, na_w_qkv, na_rpb, na_w_out, ffn_w_up, ffn_conv_w, ffn_conv_b, ffn_w_down, final_g):
    bsz, l, d = x.shape
    lc = ctx.shape[1]
    depth = ada_w.shape[0]
    assert lc % TM == 0 and l % TM == 0 and l % GRID_W == 0 and bsz + 1 <= 16
    ncb = lc // TM
    hs = jnp.concatenate([ctx, x], axis=1)
    cond_rows = jnp.zeros((16, d), F32).at[:bsz].set(c).at[bsz].set(c_ctx)
    mod_all = _ada_mod(cond_rows, ada_w, ada_b)
    lb_cum = jnp.cumsum(jax.nn.softmax(hg_lower_bound.astype(F32), axis=0), axis=0)
    lower_bounds = lb_cum - lb_cum[0]
    for i in range(depth):
        kind, j = i % 4, i // 4
        sh1, sc1, g1, sh2, sc2, g2 = _layer_mods(mod_all[i], bsz, d)
        if kind == 0:
            hs = _mla_layer(hs, norm1_g[i], sh1, sc1, g1, mla_w_in[j], mla_q_norm_g[j], mla_w_q_up[j],
                            mla_kv_norm_g[j], mla_w_kv_up[j], mla_w_out[j], ncb=ncb, lc=lc)
        elif kind == 1:
            hs = _s5_layer(hs, norm1_g[i], sh1, sc1, g1, s5_w_in[j], s5_lambda_re[j], s5_lambda_im[j],
                           s5_log_dt[j], s5_b_re[j], s5_b_im[j], s5_c_re[j], s5_c_im[j], s5_d[j], s5_w_glu[j],
                           ncb=ncb, lc=lc)
        elif kind == 2:
            hs = _hgrn2_layer(hs, norm1_g[i], sh1, sc1, g1, hg_w_in[j], lower_bounds[i], hg_norm_g[j],
                              hg_w_out[j], ncb=ncb, lc=lc)
        else:
            hs = _natten_layer(hs, norm1_g[i], sh1, sc1, g1, na_w_qkv[j], na_rpb[j], na_w_out[j],
                               ncb=ncb, lc=lc)
        hs = _conv_ffn(hs, norm2_g[i], sh2, sc2, g2, ffn_w_up[i], ffn_conv_w[i], ffn_conv_b[i],
                       ffn_w_down[i], ncb=ncb)
    return _final_norm(hs, final_g, ncb=ncb)
```

```python
import functools
import math

import jax
import jax.numpy as jnp
import numpy as np
from jax import lax
from jax.experimental import pallas as pl
from jax.experimental.pallas import tpu as pltpu

F32 = jnp.float32
BF16 = jnp.bfloat16
HI = lax.Precision.HIGHEST

EPS = 1e-6
GRID_W = 64
ROPE_THETA = 10000.0
LOG2E = math.log2(math.e)

LANES = 128
VMEM_LIMIT_BYTES = 56 * 1024 * 1024
TM = 256

MLA_HEADS = 16
MLA_Q_RANK = 384
MLA_KV_RANK = 256
MLA_NOPE = 64
MLA_ROPE = 32
MLA_V = 64
MLA_QK = MLA_NOPE + MLA_ROPE
MLA_HEADS_PER_STEP = 4

S5_GROUP_CH = 16
S5_STATE = 64
S5_T = 16

HG_HEAD_DIM = 128
HG_CHUNK = 64
HG_HEADS_PER_STEP = 4

NA_HEADS = 16
NA_HEAD_DIM = 64
NA_KH = 8
NA_KW = 16
NEG_BIG = -1e30
NA_ROW_UNROLL = 4


def _cparams(*sem):
    return pltpu.CompilerParams(dimension_semantics=sem, vmem_limit_bytes=VMEM_LIMIT_BYTES)


def _seg(ncb):
    return lambda i: jnp.where(i >= ncb, 1, 0)


def _norm_mod(x, g, sh, sc):
    ms = jnp.mean(x * x, axis=-1, keepdims=True)
    return (x * lax.rsqrt(ms + EPS) * g) * (1.0 + sc) + sh


def _sigmoid(x):
    return 1.0 / (1.0 + jnp.exp(-x))


def _dot(a, b):
    return jnp.dot(a, b, preferred_element_type=F32)


def _dot_nt(a, b):
    return lax.dot_general(a, b, (((1,), (1,)), ((), ())), preferred_element_type=F32)


def _dot_tn(a, b):
    return lax.dot_general(a, b, (((0,), (0,)), ((), ())), preferred_element_type=F32)


def _ada_kernel(cond_ref, w_ref, b_ref, o_ref):
    cond = cond_ref[...]
    a = (cond * _sigmoid(cond)).astype(BF16)
    o_ref[0] = _dot(a, w_ref[0].astype(BF16)) + b_ref[0]


def _ada_mod(cond_rows, ada_w, ada_b):
    depth, d, n = ada_w.shape
    rows = cond_rows.shape[0]
    tn = 1536
    return pl.pallas_call(
        _ada_kernel,
        grid=(depth, n // tn),
        in_specs=[
            pl.BlockSpec((rows, d), lambda i, j: (0, 0)),
            pl.BlockSpec((1, d, tn), lambda i, j: (i, 0, j)),
            pl.BlockSpec((1, 1, tn), lambda i, j: (i, 0, j)),
        ],
        out_specs=pl.BlockSpec((1, rows, tn), lambda i, j: (i, 0, j)),
        out_shape=jax.ShapeDtypeStruct((depth, rows, n), F32),
        compiler_params=_cparams("parallel", "parallel"),
        name="ada_mod",
    )(cond_rows, ada_w, ada_b.reshape(depth, 1, n))


def _nmm_kernel(x_ref, g_ref, sh_ref, sc_ref, w_ref, cs_ref, o_ref, *, tn):
    u = _norm_mod(x_ref[0], g_ref[...], sh_ref[0, 0], sc_ref[0, 0]).astype(BF16)
    n = w_ref.shape[1]
    for j in range(n // tn):
        sl = slice(j * tn, (j + 1) * tn)
        o_ref[0, :, sl] = (_dot(u, w_ref[:, sl]) * cs_ref[:, sl]).astype(o_ref.dtype)


def _norm_mod_matmul(hs, g, sh, sc, w, col_scale, *, ncb, out_dtype, tn):
    bsz, lt, d = hs.shape
    n = w.shape[1]
    seg = _seg(ncb)
    return pl.pallas_call(
        functools.partial(_nmm_kernel, tn=tn),
        grid=(bsz, lt // TM),
        in_specs=[
            pl.BlockSpec((1, TM, d), lambda b, i: (b, i, 0)),
            pl.BlockSpec((1, d), lambda b, i: (0, 0)),
            pl.BlockSpec((1, 1, 1, d), lambda b, i: (b, seg(i), 0, 0)),
            pl.BlockSpec((1, 1, 1, d), lambda b, i: (b, seg(i), 0, 0)),
            pl.BlockSpec((d, n), lambda b, i: (0, 0)),
            pl.BlockSpec((1, n), lambda b, i: (0, 0)),
        ],
        out_specs=pl.BlockSpec((1, TM, n), lambda b, i: (b, i, 0)),
        out_shape=jax.ShapeDtypeStruct((bsz, lt, n), out_dtype),
        compiler_params=_cparams("parallel", "parallel"),
        name="norm_mod_matmul",
    )(hs, g.reshape(1, d), sh, sc, w, col_scale)


def _mgr_kernel(a_ref, w_ref, x_ref, gate_ref, o_ref):
    o_ref[0] = x_ref[0] + gate_ref[0, 0] * _dot(a_ref[0], w_ref[...])


def _matmul_gate_residual(a, w, hs, gate, *, ncb):
    bsz, lt, d = hs.shape
    k = a.shape[2]
    seg = _seg(ncb)
    return pl.pallas_call(
        _mgr_kernel,
        grid=(bsz, lt // TM),
        in_specs=[
            pl.BlockSpec((1, TM, k), lambda b, i: (b, i, 0)),
            pl.BlockSpec((k, d), lambda b, i: (0, 0)),
            pl.BlockSpec((1, TM, d), lambda b, i: (b, i, 0)),
            pl.BlockSpec((1, 1, 1, d), lambda b, i: (b, seg(i), 0, 0)),
        ],
        out_specs=pl.BlockSpec((1, TM, d), lambda b, i: (b, i, 0)),
        out_shape=jax.ShapeDtypeStruct((bsz, lt, d), F32),
        compiler_params=_cparams("parallel", "parallel"),
        name="matmul_gate_residual",
    )(a, w, hs, gate)


FFN_HALO = 16
FFN_CHUNK = 2816


def _ffn_kernel(x_ref, xp_ref, xn_ref, g_ref, sh_ref, sc_ref, gate_ref, wup_ref, cw_ref, cb_ref,
                wdn_ref, o_ref, u_scr, ha_scr, hg_scr, acc_scr, *, ncb, nb, nch):
    i = pl.program_id(1)
    g = g_ref[...]
    sh = sh_ref[0, 0]
    sc = sc_ref[0, 0]
    keep_prev = jnp.where((i == 0) | (i == ncb), 0.0, 1.0)
    keep_next = jnp.where((i == ncb - 1) | (i == nb - 1), 0.0, 1.0)
    u_scr[0:FFN_HALO, :] = (_norm_mod(xp_ref[0], g, sh, sc) * keep_prev).astype(BF16)
    u_scr[FFN_HALO:FFN_HALO + TM, :] = _norm_mod(x_ref[0], g, sh, sc).astype(BF16)
    u_scr[FFN_HALO + TM:, :] = (_norm_mod(xn_ref[0], g, sh, sc) * keep_next).astype(BF16)
    acc_scr[...] = jnp.zeros_like(acc_scr)

    def conv(h_scr, c):
        w = cw_ref[c]
        return (h_scr[FFN_HALO - 1:FFN_HALO - 1 + TM, :] * w[0:1]
                + h_scr[FFN_HALO:FFN_HALO + TM, :] * w[1:2]
                + h_scr[FFN_HALO + 1:FFN_HALO + 1 + TM, :] * w[2:3]
                + cb_ref[c])

    def body(c, carry):
        u = u_scr[...]
        ha_scr[...] = _dot(u, wup_ref[c])
        hg_scr[...] = _dot(u, wup_ref[nch + c])
        a = conv(ha_scr, c)
        gg = conv(hg_scr, nch + c)
        act = (a * (gg * _sigmoid(gg))).astype(BF16)
        acc_scr[...] += _dot(act, wdn_ref[c])
        return carry

    lax.fori_loop(0, nch, body, 0)
    o_ref[0] = x_ref[0] + gate_ref[0, 0] * acc_scr[...]


def _conv_ffn(hs, g, sh, sc, gate, w_up, conv_w, conv_b, w_down, *, ncb):
    bsz, lt, d = hs.shape
    f = w_down.shape[0]
    nch = f // FFN_CHUNK
    nb = lt // TM
    hb = TM // FFN_HALO
    nhalo = lt // FFN_HALO
    seg = _seg(ncb)
    wup = w_up.astype(BF16).reshape(d, 2 * nch, FFN_CHUNK).transpose(1, 0, 2)
    cw = conv_w.reshape(3, 2 * nch, FFN_CHUNK).transpose(1, 0, 2)
    cb = conv_b.reshape(2 * nch, 1, FFN_CHUNK)
    wdn = w_down.astype(BF16).reshape(nch, FFN_CHUNK, d)
    return pl.pallas_call(
        functools.partial(_ffn_kernel, ncb=ncb, nb=nb, nch=nch),
        grid=(bsz, nb),
        in_specs=[
            pl.BlockSpec((1, TM, d), lambda b, i: (b, i, 0)),
            pl.BlockSpec((1, FFN_HALO, d), lambda b, i: (b, jnp.maximum(i * hb - 1, 0), 0)),
            pl.BlockSpec((1, FFN_HALO, d), lambda b, i: (b, jnp.minimum((i + 1) * hb, nhalo - 1), 0)),
            pl.BlockSpec((1, d), lambda b, i: (0, 0)),
            pl.BlockSpec((1, 1, 1, d), lambda b, i: (b, seg(i), 0, 0)),
            pl.BlockSpec((1, 1, 1, d), lambda b, i: (b, seg(i), 0, 0)),
            pl.BlockSpec((1, 1, 1, d), lambda b, i: (b, seg(i), 0, 0)),
            pl.BlockSpec((2 * nch, d, FFN_CHUNK), lambda b, i: (0, 0, 0)),
            pl.BlockSpec((2 * nch, 3, FFN_CHUNK), lambda b, i: (0, 0, 0)),
            pl.BlockSpec((2 * nch, 1, FFN_CHUNK), lambda b, i: (0, 0, 0)),
            pl.BlockSpec((nch, FFN_CHUNK, d), lambda b, i: (0, 0, 0)),
        ],
        out_specs=pl.BlockSpec((1, TM, d), lambda b, i: (b, i, 0)),
        out_shape=jax.ShapeDtypeStruct((bsz, lt, d), F32),
        scratch_shapes=[
            pltpu.VMEM((TM + 2 * FFN_HALO, d), BF16),
            pltpu.VMEM((TM + 2 * FFN_HALO, FFN_CHUNK), F32),
            pltpu.VMEM((TM + 2 * FFN_HALO, FFN_CHUNK), F32),
            pltpu.VMEM((TM, d), F32),
        ],
        compiler_params=_cparams("parallel", "parallel"),
        name="conv_ffn",
    )(hs, hs, hs, g.reshape(1, d), sh, sc, gate, wup, cw, cb, wdn)


def _rope_tables(lc, l):
    half = MLA_ROPE // 4
    inv = 1.0 / (ROPE_THETA ** (jnp.arange(half, dtype=F32) / half))
    t = jnp.arange(l)
    ang_r = (t // GRID_W).astype(F32)[:, None] * inv[None, :]
    ang_c = (t % GRID_W).astype(F32)[:, None] * inv[None, :]
    cos = jnp.concatenate([jnp.cos(ang_r), jnp.cos(ang_r), jnp.cos(ang_c), jnp.cos(ang_c)], axis=-1)
    sin = jnp.concatenate([-jnp.sin(ang_r), jnp.sin(ang_r), -jnp.sin(ang_c), jnp.sin(ang_c)], axis=-1)
    cos_t = jnp.ones((lc + l, LANES), F32).at[lc:, MLA_NOPE:MLA_NOPE + MLA_ROPE].set(cos)
    sin_t = jnp.zeros((lc + l, LANES), F32).at[lc:, MLA_NOPE:MLA_NOPE + MLA_ROPE].set(sin)
    return cos_t, sin_t


def _mla_proj_kernel(x_ref, g_ref, sh_ref, sc_ref, win_ref, qg_ref, wq_ref, kvg_ref, wkn_ref, wv_ref,
                     cos_ref, sin_ref, q_ref, k_ref, v_ref):
    u = _norm_mod(x_ref[0], g_ref[...], sh_ref[0, 0], sc_ref[0, 0]).astype(BF16)
    lat = _dot(u, win_ref[...])
    q_lat = lat[:, :MLA_Q_RANK]
    kv_lat = lat[:, MLA_Q_RANK:MLA_Q_RANK + MLA_KV_RANK]
    kr = lat[:, MLA_Q_RANK + MLA_KV_RANK:]

    def rms(t, gg):
        return (t * lax.rsqrt(jnp.mean(t * t, axis=-1, keepdims=True) + EPS) * gg).astype(BF16)

    qn = rms(q_lat, qg_ref[...])
    kvn = rms(kv_lat, kvg_ref[...])
    cos = cos_ref[...]
    sin = sin_ref[...]
    lane = lax.broadcasted_iota(jnp.int32, (TM, LANES), 1)
    first_half = (lane % (MLA_ROPE // 2)) < (MLA_ROPE // 4)

    def rope(t):
        partner = jnp.where(first_half, pltpu.roll(t, LANES - MLA_ROPE // 4, 1), pltpu.roll(t, MLA_ROPE // 4, 1))
        return t * cos + partner * sin

    kr_rot = rope(kr)
    v_ref[0] = _dot(kvn, wv_ref[...]).astype(BF16)
    qscale = (MLA_QK ** -0.5) * LOG2E
    for h in range(MLA_HEADS):
        sl = slice(h * LANES, (h + 1) * LANES)
        q_ref[0, :, sl] = rope(_dot(qn, wq_ref[:, sl]) * qscale).astype(BF16)
        k_ref[0, :, sl] = (_dot(kvn, wkn_ref[:, sl]) + kr_rot).astype(BF16)


def _mla_attn_kernel(q_ref, k_ref, v_ref, o_ref, *, ncb, lc):
    i = pl.program_id(2)

    def attend(lk):
        q = q_ref[0]
        lane = lax.broadcasted_iota(jnp.int32, (q.shape[0], LANES), 1)
        for pair in range(q.shape[1] // (2 * LANES)):
            v = v_ref[0, 0:lk, pair * LANES:(pair + 1) * LANES]
            outs = []
            for e in range(2):
                sl = slice((2 * pair + e) * LANES, (2 * pair + e + 1) * LANES)
                s = _dot_nt(q[:, sl], k_ref[0, 0:lk, sl])
                m = jnp.max(s, axis=-1, keepdims=True)
                p = jnp.exp2(s - m)
                l = jnp.sum(p, axis=-1, keepdims=True)
                outs.append(_dot(p.astype(BF16), v) / l)
            o_ref[0, :, pair * LANES:(pair + 1) * LANES] = jnp.where(lane < MLA_V, outs[0], outs[1]).astype(o_ref.dtype)

    @pl.when(i < ncb)
    def _():
        attend(lc)

    @pl.when(i >= ncb)
    def _():
        attend(k_ref.shape[1])


def _mla_layer(hs, g, sh, sc, gate, w_in, q_norm_g, w_q_up, kv_norm_g, w_kv_up, w_out, *, ncb, lc):
    bsz, lt, d = hs.shape
    l = lt - lc
    nh = MLA_HEADS
    seg = _seg(ncb)
    w_in_p = jnp.zeros((d, 768), F32)
    w_in_p = w_in_p.at[:, :MLA_Q_RANK + MLA_KV_RANK].set(w_in[:, :MLA_Q_RANK + MLA_KV_RANK])
    w_in_p = w_in_p.at[:, 640 + MLA_NOPE:640 + MLA_QK].set(w_in[:, MLA_Q_RANK + MLA_KV_RANK:])
    wq = w_q_up.reshape(MLA_Q_RANK, nh, MLA_QK)
    wq = jnp.pad(wq, ((0, 0), (0, 0), (0, LANES - MLA_QK))).reshape(MLA_Q_RANK, nh * LANES)
    wkv = w_kv_up.reshape(MLA_KV_RANK, nh, MLA_NOPE + MLA_V)
    wkn = jnp.pad(wkv[:, :, :MLA_NOPE], ((0, 0), (0, 0), (0, LANES - MLA_NOPE))).reshape(MLA_KV_RANK, nh * LANES)
    wv = wkv[:, :, MLA_NOPE:].reshape(MLA_KV_RANK, nh * MLA_V)
    cos_t, sin_t = _rope_tables(lc, l)

    q, k, v = pl.pallas_call(
        _mla_proj_kernel,
        grid=(bsz, lt // TM),
        in_specs=[
            pl.BlockSpec((1, TM, d), lambda b, i: (b, i, 0)),
            pl.BlockSpec((1, d), lambda b, i: (0, 0)),
            pl.BlockSpec((1, 1, 1, d), lambda b, i: (b, seg(i), 0, 0)),
            pl.BlockSpec((1, 1, 1, d), lambda b, i: (b, seg(i), 0, 0)),
            pl.BlockSpec((d, 768), lambda b, i: (0, 0)),
            pl.BlockSpec((1, MLA_Q_RANK), lambda b, i: (0, 0)),
            pl.BlockSpec((MLA_Q_RANK, nh * LANES), lambda b, i: (0, 0)),
            pl.BlockSpec((1, MLA_KV_RANK), lambda b, i: (0, 0)),
            pl.BlockSpec((MLA_KV_RANK, nh * LANES), lambda b, i: (0, 0)),
            pl.BlockSpec((MLA_KV_RANK, nh * MLA_V), lambda b, i: (0, 0)),
            pl.BlockSpec((TM, LANES), lambda b, i: (i, 0)),
            pl.BlockSpec((TM, LANES), lambda b, i: (i, 0)),
        ],
        out_specs=[
            pl.BlockSpec((1, TM, nh * LANES), lambda b, i: (b, i, 0)),
            pl.BlockSpec((1, TM, nh * LANES), lambda b, i: (b, i, 0)),
            pl.BlockSpec((1, TM, nh * MLA_V), lambda b, i: (b, i, 0)),
        ],
        out_shape=[
            jax.ShapeDtypeStruct((bsz, lt, nh * LANES), BF16),
            jax.ShapeDtypeStruct((bsz, lt, nh * LANES), BF16),
            jax.ShapeDtypeStruct((bsz, lt, nh * MLA_V), BF16),
        ],
        compiler_params=_cparams("parallel", "parallel"),
        name="mla_proj",
    )(hs, g.reshape(1, d), sh, sc, w_in_p.astype(BF16), q_norm_g.reshape(1, -1), wq.astype(BF16),
      kv_norm_g.reshape(1, -1), wkn.astype(BF16), wv.astype(BF16), cos_t, sin_t)

    o = pl.pallas_call(
        functools.partial(_mla_attn_kernel, ncb=ncb, lc=lc),
        grid=(bsz, nh // MLA_HEADS_PER_STEP, lt // TM),
        in_specs=[
            pl.BlockSpec((1, TM, MLA_HEADS_PER_STEP * LANES), lambda b, p, i: (b, i, p)),
            pl.BlockSpec((1, lt, MLA_HEADS_PER_STEP * LANES), lambda b, p, i: (b, 0, p)),
            pl.BlockSpec((1, lt, MLA_HEADS_PER_STEP * MLA_V), lambda b, p, i: (b, 0, p)),
        ],
        out_specs=pl.BlockSpec((1, TM, MLA_HEADS_PER_STEP * MLA_V), lambda b, p, i: (b, i, p)),
        out_shape=jax.ShapeDtypeStruct((bsz, lt, nh * MLA_V), BF16),
        compiler_params=_cparams("parallel", "parallel", "parallel"),
        name="mla_attn",
    )(q, k, v)
    return _matmul_gate_residual(o, w_out.astype(BF16), hs, gate, ncb=ncb)


def _s5_matrices(lam_re, lam_im, log_dt, b_re, b_im, c_re, c_im):
    t_len = S5_T
    n_grp = lam_re.shape[1]
    ch = S5_GROUP_CH
    n_st = S5_STATE
    dt = jnp.exp(log_dt.astype(F32))[..., None]
    ld_re = lam_re.astype(F32) * dt
    ld_im = lam_im.astype(F32) * dt
    tau = jnp.arange(t_len + 1, dtype=F32)[:, None, None, None]
    mag = jnp.exp(tau * ld_re[None])
    pw_re = mag * jnp.cos(tau * ld_im[None])
    pw_im = mag * jnp.sin(tau * ld_im[None])
    lb_re, lb_im = pw_re[1] - 1.0, pw_im[1]
    den = lam_re * lam_re + lam_im * lam_im
    f_re = (lb_re * lam_re + lb_im * lam_im) / den
    f_im = (lb_im * lam_re - lb_re * lam_im) / den
    bb_re = f_re[..., None] * b_re - f_im[..., None] * b_im
    bb_im = f_re[..., None] * b_im + f_im[..., None] * b_re
    cp_re = c_re[None] * pw_re[:t_len, :, :, None, :] - c_im[None] * pw_im[:t_len, :, :, None, :]
    cp_im = c_re[None] * pw_im[:t_len, :, :, None, :] + c_im[None] * pw_re[:t_len, :, :, None, :]
    taps = (jnp.einsum("tdgon,dgni->tdgoi", cp_re, bb_re, precision=HI)
            - jnp.einsum("tdgon,dgni->tdgoi", cp_im, bb_im, precision=HI))
    s_idx = np.arange(t_len)[:, None, None]
    t_idx = np.arange(t_len)[None, :, None]
    u_idx = np.arange(t_len)[None, None, :]
    sel_f = jnp.asarray((t_idx - s_idx == u_idx).astype(np.float32))
    sel_b = jnp.asarray((s_idx - t_idx == u_idx).astype(np.float32))
    kf = jnp.einsum("stu,ugoi->stgoi", sel_f, taps[:, 0], precision=HI)
    kb = jnp.einsum("stu,ugoi->stgoi", sel_b, taps[:, 1], precision=HI)
    k_tot = (kf + kb).transpose(2, 0, 4, 1, 3).reshape(n_grp, t_len * ch, t_len * ch)

    def state_in(pw_r, pw_i, brr, bii):
        re = pw_r[..., None] * brr[None] - pw_i[..., None] * bii[None]
        im = pw_r[..., None] * bii[None] + pw_i[..., None] * brr[None]
        return (re.transpose(1, 0, 3, 2).reshape(n_grp, t_len * ch, n_st),
                im.transpose(1, 0, 3, 2).reshape(n_grp, t_len * ch, n_st))

    inf_re, inf_im = state_in(jnp.flip(pw_re[:t_len, 0], 0), jnp.flip(pw_im[:t_len, 0], 0), bb_re[0], bb_im[0])
    inb_re, inb_im = state_in(pw_re[:t_len, 1], pw_im[:t_len, 1], bb_re[1], bb_im[1])

    def state_out(pw_r, pw_i, crr, cii):
        re = crr[None] * pw_r[:, :, None, :] - cii[None] * pw_i[:, :, None, :]
        im = crr[None] * pw_i[:, :, None, :] + cii[None] * pw_r[:, :, None, :]
        return (re.transpose(1, 3, 0, 2).reshape(n_grp, n_st, t_len * ch),
                (-im).transpose(1, 3, 0, 2).reshape(n_grp, n_st, t_len * ch))

    outf_re, outf_im = state_out(pw_re[1:, 0], pw_im[1:, 0], c_re[0], c_im[0])
    outb_re, outb_im = state_out(jnp.flip(pw_re[1:, 1], 0), jnp.flip(pw_im[1:, 1], 0), c_re[1], c_im[1])
    odd = (jnp.arange(n_grp) % 2 == 1)[:, None, None]

    def pad_cols(m):
        z = jnp.zeros_like(m)
        return jnp.where(odd, jnp.concatenate([z, m], -1), jnp.concatenate([m, z], -1))

    def pad_rows(m):
        z = jnp.zeros_like(m)
        return jnp.where(odd, jnp.concatenate([z, m], 1), jnp.concatenate([m, z], 1))

    w_in = jnp.concatenate([pad_cols(inf_re), pad_cols(inf_im), pad_cols(inb_re), pad_cols(inb_im)], -1)
    m_out = jnp.stack([pad_rows(outf_re), pad_rows(outf_im), pad_rows(outb_re), pad_rows(outb_im)], 1)
    decay = jnp.stack([pw_re[t_len, 0], pw_im[t_len, 0], pw_re[t_len, 1], pw_im[t_len, 1]], 0)
    decay = decay.reshape(4, 1, n_grp * n_st)
    return w_in.astype(BF16), k_tot.astype(BF16), m_out.astype(BF16), decay


def _s5_in_kernel(z_ref, w_ref, u_ref):
    for k in range(4):
        sl = slice(k * LANES, (k + 1) * LANES)
        u_ref[k] = _dot(z_ref[0], w_ref[0, :, sl]) + _dot(z_ref[1], w_ref[1, :, sl])


def _s5_scan_kernel(u_ref, a_ref, p_ref, *, jc, jt):
    afr, afi, abr, abi = a_ref[0], a_ref[1], a_ref[2], a_ref[3]
    zero = jnp.zeros(p_ref.shape[2:], F32)

    def body(jj, carry):
        fr, fi, br, bi = carry
        jb = jnp.where(jj < jc, jc - 1 - jj, jt - 1 - jj + jc)
        p_ref[0, jj] = fr
        p_ref[1, jj] = fi
        p_ref[2, jb] = br
        p_ref[3, jb] = bi
        nfr = afr * fr - afi * fi + u_ref[0, jj]
        nfi = afr * fi + afi * fr + u_ref[1, jj]
        nbr = abr * br - abi * bi + u_ref[2, jb]
        nbi = abr * bi + abi * br + u_ref[3, jb]
        return nfr, nfi, nbr, nbi

    lax.fori_loop(0, jt, body, (zero, zero, zero, zero))


def _s5_out_kernel(z_ref, p_ref, k_ref, m_ref, y_ref):
    for e in range(2):
        acc = _dot(z_ref[e], k_ref[e])
        for k in range(4):
            acc += _dot(p_ref[k].astype(BF16), m_ref[e, k])
        y_ref[e] = acc


def _s5_glu_kernel(y_ref, z_ref, d_ref, w_ref, x_ref, gate_ref, o_ref):
    d = x_ref.shape[2]
    y = y_ref[0] + z_ref[0] * d_ref[...]
    ge = jax.nn.gelu(y).astype(BF16)
    ag = _dot(ge, w_ref[...])
    o_ref[0] = x_ref[0] + gate_ref[0, 0] * (ag[:, :d] * _sigmoid(ag[:, d:]))


def _s5_layer(hs, g, sh, sc, gate, w_in, lam_re, lam_im, log_dt, b_re, b_im, c_re, c_im, d_skip, w_glu,
              *, ncb, lc):
    bsz, lt, d = hs.shape
    width = w_in.shape[1]
    n_grp = width // S5_GROUP_CH
    jt = lt // S5_T
    jc = lc // S5_T
    rows = jt * bsz
    cols = S5_T * S5_GROUP_CH
    seg = _seg(ncb)
    z = _norm_mod_matmul(hs, g, sh, sc, w_in.astype(BF16), jnp.ones((1, width), F32),
                         ncb=ncb, out_dtype=F32, tn=width)
    w_si, k_tot, m_out, decay = _s5_matrices(lam_re, lam_im, log_dt, b_re, b_im, c_re, c_im)
    zg = z.astype(BF16).reshape(bsz, jt, S5_T, n_grp, S5_GROUP_CH).transpose(3, 1, 0, 2, 4)
    zg = zg.reshape(n_grp, rows, cols)
    n_state = n_grp * S5_STATE
    u = pl.pallas_call(
        _s5_in_kernel,
        grid=(n_grp // 2,),
        in_specs=[
            pl.BlockSpec((2, rows, cols), lambda p: (p, 0, 0)),
            pl.BlockSpec((2, cols, 4 * LANES), lambda p: (p, 0, 0)),
        ],
        out_specs=pl.BlockSpec((4, rows, LANES), lambda p: (0, 0, p)),
        out_shape=jax.ShapeDtypeStruct((4, rows, n_state), F32),
        compiler_params=_cparams("parallel"),
        name="s5_state_in",
    )(zg, w_si)
    p_state = pl.pallas_call(
        functools.partial(_s5_scan_kernel, jc=jc, jt=jt),
        grid=(n_state // LANES,),
        in_specs=[
            pl.BlockSpec((4, jt, bsz, LANES), lambda p: (0, 0, 0, p)),
            pl.BlockSpec((4, 1, LANES), lambda p: (0, 0, p)),
        ],
        out_specs=pl.BlockSpec((4, jt, bsz, LANES), lambda p: (0, 0, 0, p)),
        out_shape=jax.ShapeDtypeStruct((4, jt, bsz, n_state), F32),
        compiler_params=_cparams("parallel"),
        name="s5_scan",
    )(u.reshape(4, jt, bsz, n_state), decay)
    yg = pl.pallas_call(
        _s5_out_kernel,
        grid=(n_grp // 2,),
        in_specs=[
            pl.BlockSpec((2, rows, cols), lambda p: (p, 0, 0)),
            pl.BlockSpec((4, rows, LANES), lambda p: (0, 0, p)),
            pl.BlockSpec((2, cols, cols), lambda p: (p, 0, 0)),
            pl.BlockSpec((2, 4, LANES, cols), lambda p: (p, 0, 0, 0)),
        ],
        out_specs=pl.BlockSpec((2, rows, cols), lambda p: (p, 0, 0)),
        out_shape=jax.ShapeDtypeStruct((n_grp, rows, cols), F32),
        compiler_params=_cparams("parallel"),
        name="s5_chunk_out",
    )(zg, p_state.reshape(4, rows, n_state), k_tot, m_out)
    y = yg.reshape(n_grp, jt, bsz, S5_T, S5_GROUP_CH).transpose(2, 1, 3, 0, 4).reshape(bsz, lt, width)
    return pl.pallas_call(
        _s5_glu_kernel,
        grid=(bsz, lt // TM),
        in_specs=[
            pl.BlockSpec((1, TM, width), lambda b, i: (b, i, 0)),
            pl.BlockSpec((1, TM, width), lambda b, i: (b, i, 0)),
            pl.BlockSpec((1, width), lambda b, i: (0, 0)),
            pl.BlockSpec((width, 2 * d), lambda b, i: (0, 0)),
            pl.BlockSpec((1, TM, d), lambda b, i: (b, i, 0)),
            pl.BlockSpec((1, 1, 1, d), lambda b, i: (b, seg(i), 0, 0)),
        ],
        out_specs=pl.BlockSpec((1, TM, d), lambda b, i: (b, i, 0)),
        out_shape=jax.ShapeDtypeStruct((bsz, lt, d), F32),
        compiler_params=_cparams("parallel", "parallel"),
        name="s5_glu",
    )(y, z, d_skip.reshape(1, width), w_glu.astype(BF16), hs, gate)


def _gla_dir(zq, zf, zv, lb, states, incl, tri, rev):
    dk = HG_HEAD_DIM
    width = zq.shape[1]
    nchunks = zq.shape[0] // HG_CHUNK
    forget = lb + (1.0 - lb) * _sigmoid(zf)
    lf = jnp.log(forget)
    kk = 1.0 - forget
    v_all = zv.astype(BF16)
    hi = lf.astype(BF16)
    r1 = lf - hi.astype(F32)
    mid = r1.astype(BF16)
    lo = (r1 - mid.astype(F32)).astype(BF16)
    parts = _dot(tri, jnp.concatenate([hi, mid, lo], axis=1))
    bcum = parts[:, :width] + parts[:, width:2 * width] + parts[:, 2 * width:]
    btot = jnp.concatenate(
        [jnp.broadcast_to(bcum[c * HG_CHUNK:c * HG_CHUNK + 1] if rev else bcum[(c + 1) * HG_CHUNK - 1:(c + 1) * HG_CHUNK],
                          (HG_CHUNK, width)) for c in range(nchunks)], axis=0)
    q_in_all = (zq * _sigmoid(zq)) * (dk ** -0.5) * jnp.exp(bcum)
    k_in_all = (kk * jnp.exp(-bcum)).astype(BF16)
    k_out_all = kk * jnp.exp(btot - bcum)
    chunk_of_row = lax.broadcasted_iota(jnp.int32, (zq.shape[0], dk), 0) // HG_CHUNK
    zero = jnp.zeros((zq.shape[0], dk), F32)
    outs, new_states = [], []
    for h, st in enumerate(states):
        sl = slice(h * dk, (h + 1) * dk)
        q_in, k_out, v = q_in_all[:, sl], k_out_all[:, sl], v_all[:, sl]
        att = jnp.where(incl, _dot_nt(q_in.astype(BF16), k_in_all[:, sl]), 0.0)
        o_intra = _dot(att.astype(BF16), v)
        k_cat = jnp.concatenate([jnp.where(chunk_of_row == c, k_out, zero) for c in range(nchunks)], axis=1)
        q_cat = jnp.concatenate([jnp.where(chunk_of_row == c, q_in, zero) for c in range(nchunks)], axis=1)
        ds = _dot_tn(v, k_cat.astype(BF16))
        entering = [None] * nchunks
        for cc in range(nchunks):
            c = nchunks - 1 - cc if rev else cc
            entering[c] = st
            st = st * jnp.exp(btot[c * HG_CHUNK:c * HG_CHUNK + 1, sl]) + ds[:, c * dk:(c + 1) * dk]
        scat = jnp.concatenate(entering, axis=1).astype(BF16)
        outs.append(o_intra + _dot_nt(q_cat.astype(BF16), scat))
        new_states.append(st)
    return outs, new_states


def _gla_kernel(zqf_ref, zff_ref, zvf_ref, zqb_ref, zfb_ref, zvb_ref, lb_ref, of_ref, ob_ref, st_scr):
    @pl.when(pl.program_id(2) == 0)
    def _():
        st_scr[...] = jnp.zeros_like(st_scr)

    n = zqf_ref.shape[1]
    row = lax.broadcasted_iota(jnp.int32, (n, n), 0)
    col = lax.broadcasted_iota(jnp.int32, (n, n), 1)
    same = (row // HG_CHUNK) == (col // HG_CHUNK)
    dk = HG_HEAD_DIM
    for direction, (zq_ref, zf_ref, zv_ref, o_ref) in enumerate(
            ((zqf_ref, zff_ref, zvf_ref, of_ref), (zqb_ref, zfb_ref, zvb_ref, ob_ref))):
        rev = direction == 1
        incl = same & ((col >= row) if rev else (col <= row))
        tri = incl.astype(BF16)
        nheads = zq_ref.shape[2] // dk
        outs, states = _gla_dir(zq_ref[0], zf_ref[0], zv_ref[0], lb_ref[0],
                                [st_scr[direction, h] for h in range(nheads)], incl, tri, rev)
        for h in range(nheads):
            o_ref[0, :, h * dk:(h + 1) * dk] = outs[h]
            st_scr[direction, h] = states[h]


def _hg_out_kernel(of_ref, ob_ref, zg_ref, ng_ref, w_ref, x_ref, gate_ref, o_ref):
    o = of_ref[0] + ob_ref[0]
    gsig = zg_ref[0]
    gsig = gsig * _sigmoid(gsig)
    ng = ng_ref[...]
    parts = []
    for h in range(o.shape[1] // HG_HEAD_DIM):
        sl = slice(h * HG_HEAD_DIM, (h + 1) * HG_HEAD_DIM)
        oh = o[:, sl]
        on = oh * lax.rsqrt(jnp.mean(oh * oh, axis=-1, keepdims=True) + EPS)
        parts.append((on * ng[:, sl] * gsig[:, sl]).astype(BF16))
    a = jnp.concatenate(parts, axis=-1)
    o_ref[0] = x_ref[0] + gate_ref[0, 0] * _dot(a, w_ref[...])


def _hgrn2_layer(hs, g, sh, sc, gate, w_in, lower_bound, norm_g, w_out, *, ncb, lc):
    bsz, lt, d = hs.shape
    nh = d // HG_HEAD_DIM
    nb = lt // TM
    seg = _seg(ncb)
    z = _norm_mod_matmul(hs, g, sh, sc, w_in.astype(BF16), jnp.ones((1, 5 * d), F32),
                         ncb=ncb, out_dtype=F32, tn=1024)
    lb = lower_bound.astype(F32).reshape(1, 1, d)
    hps = HG_HEADS_PER_STEP
    ng = nh // hps
    wb = hps * HG_HEAD_DIM

    def rblk(s):
        return jnp.where(s < ncb, ncb - 1 - s, nb - 1 - s + ncb)

    outs = pl.pallas_call(
        _gla_kernel,
        grid=(bsz, ng, nb),
        in_specs=[
            pl.BlockSpec((1, TM, wb), lambda b, h, s: (b, s, h)),
            pl.BlockSpec((1, TM, wb), lambda b, h, s: (b, s, ng + h)),
            pl.BlockSpec((1, TM, wb), lambda b, h, s: (b, s, 3 * ng + h)),
            pl.BlockSpec((1, TM, wb), lambda b, h, s: (b, rblk(s), h)),
            pl.BlockSpec((1, TM, wb), lambda b, h, s: (b, rblk(s), 2 * ng + h)),
            pl.BlockSpec((1, TM, wb), lambda b, h, s: (b, rblk(s), 3 * ng + h)),
            pl.BlockSpec((1, 1, wb), lambda b, h, s: (0, 0, h)),
        ],
        out_specs=[
            pl.BlockSpec((1, TM, wb), lambda b, h, s: (b, s, h)),
            pl.BlockSpec((1, TM, wb), lambda b, h, s: (b, rblk(s), h)),
        ],
        out_shape=[jax.ShapeDtypeStruct((bsz, lt, d), F32), jax.ShapeDtypeStruct((bsz, lt, d), F32)],
        scratch_shapes=[pltpu.VMEM((2, hps, HG_HEAD_DIM, HG_HEAD_DIM), F32)],
        compiler_params=_cparams("parallel", "parallel", "arbitrary"),
        name="hgrn2_gla",
    )(z, z, z, z, z, z, lb)
    return pl.pallas_call(
        _hg_out_kernel,
        grid=(bsz, nb),
        in_specs=[
            pl.BlockSpec((1, TM, d), lambda b, i: (b, i, 0)),
            pl.BlockSpec((1, TM, d), lambda b, i: (b, i, 0)),
            pl.BlockSpec((1, TM, d), lambda b, i: (b, i, 4)),
            pl.BlockSpec((1, d), lambda b, i: (0, 0)),
            pl.BlockSpec((d, d), lambda b, i: (0, 0)),
            pl.BlockSpec((1, TM, d), lambda b, i: (b, i, 0)),
            pl.BlockSpec((1, 1, 1, d), lambda b, i: (b, seg(i), 0, 0)),
        ],
        out_specs=pl.BlockSpec((1, TM, d), lambda b, i: (b, i, 0)),
        out_shape=jax.ShapeDtypeStruct((bsz, lt, d), F32),
        compiler_params=_cparams("parallel", "parallel"),
        name="hgrn2_out",
    )(outs[0], outs[1], z, norm_g.reshape(1, d), w_out.astype(BF16), hs, gate)


def _na_bias_table(rpb):
    nh = rpb.shape[0]
    w = np.arange(GRID_W)[:, None, None]
    kc = np.arange(GRID_W)[None, :, None]
    co = np.arange(2 * NA_KW - 1)[None, None, :]
    c0 = np.clip(w - NA_KW // 2, 0, GRID_W - NA_KW)
    valid = (kc >= c0) & (kc < c0 + NA_KW)
    onehot = jnp.asarray((valid & (kc - w + (NA_KW - 1) == co)).astype(np.float32))
    t = jnp.einsum("hrc,wkc->hrwk", rpb.astype(F32), onehot, precision=HI) * LOG2E
    t = jnp.where(jnp.asarray(valid[None, None, :, :, 0]), t, NEG_BIG)
    bias = jnp.stack([t[:, NA_KH - 1 - vi:2 * NA_KH - 1 - vi] for vi in range(NA_KH)], axis=0)
    bias = bias.transpose(0, 1, 3, 2, 4).reshape(NA_KH, nh, GRID_W, NA_KH * GRID_W)
    return bias.reshape(NA_KH, nh // 2, 2 * GRID_W, NA_KH * GRID_W)


def _na_kernel(q_ref, k_ref, v_ref, bias_ref, o_ref, *, lc, rows):
    nwin = NA_KH * GRID_W
    o_ref[0, 0:lc, :] = jnp.zeros((lc, LANES), o_ref.dtype)
    kc = k_ref[0, 0:lc, :]
    vc = v_ref[0, 0:lc, :]
    lane = lax.broadcasted_iota(jnp.int32, (GRID_W, LANES), 1)
    lo_half = lane < NA_HEAD_DIM

    def one_row(r):
        r0 = jnp.clip(r - NA_KH // 2, 0, rows - NA_KH)
        vi = r - r0
        q = q_ref[0, pl.ds(pl.multiple_of(lc + r * GRID_W, GRID_W), GRID_W), :]
        kw = k_ref[0, pl.ds(pl.multiple_of(lc + r0 * GRID_W, GRID_W), nwin), :]
        vw = v_ref[0, pl.ds(pl.multiple_of(lc + r0 * GRID_W, GRID_W), nwin), :]
        zq = jnp.zeros_like(q)
        qs = jnp.concatenate([jnp.where(lo_half, q, zq), jnp.where(lo_half, zq, q)], axis=0)
        s_loc = _dot_nt(qs, kw) + bias_ref[vi, 0]
        s_ctx = _dot_nt(qs, kc)
        m = jnp.maximum(jnp.max(s_loc, axis=-1, keepdims=True), jnp.max(s_ctx, axis=-1, keepdims=True))
        p_loc = jnp.exp2(s_loc - m)
        p_ctx = jnp.exp2(s_ctx - m)
        l = jnp.sum(p_loc, axis=-1, keepdims=True) + jnp.sum(p_ctx, axis=-1, keepdims=True)
        pv = (_dot(p_loc.astype(BF16), vw) + _dot(p_ctx.astype(BF16), vc)) / l
        o = jnp.where(lo_half, pv[0:GRID_W], pv[GRID_W:])
        o_ref[0, pl.ds(pl.multiple_of(lc + r * GRID_W, GRID_W), GRID_W), :] = o.astype(o_ref.dtype)

    def body(rr, carry):
        for j in range(NA_ROW_UNROLL):
            one_row(rr * NA_ROW_UNROLL + j)
        return carry

    lax.fori_loop(0, rows // NA_ROW_UNROLL, body, 0)


def _natten_layer(hs, g, sh, sc, gate, w_qkv, rpb, w_out, *, ncb, lc):
    bsz, lt, d = hs.shape
    rows = (lt - lc) // GRID_W
    npair = NA_HEADS // 2
    col_scale = jnp.concatenate([jnp.full((1, d), (NA_HEAD_DIM ** -0.5) * LOG2E, F32),
                                 jnp.ones((1, 2 * d), F32)], axis=-1)
    qkv = _norm_mod_matmul(hs, g, sh, sc, w_qkv.astype(BF16), col_scale, ncb=ncb, out_dtype=BF16, tn=1024)
    bias = _na_bias_table(rpb)
    o = pl.pallas_call(
        functools.partial(_na_kernel, lc=lc, rows=rows),
        grid=(npair, bsz),
        in_specs=[
            pl.BlockSpec((1, lt, LANES), lambda p, b: (b, 0, p)),
            pl.BlockSpec((1, lt, LANES), lambda p, b: (b, 0, npair + p)),
            pl.BlockSpec((1, lt, LANES), lambda p, b: (b, 0, 2 * npair + p)),
            pl.BlockSpec((NA_KH, 1, 2 * GRID_W, NA_KH * GRID_W), lambda p, b: (0, p, 0, 0)),
        ],
        out_specs=pl.BlockSpec((1, lt, LANES), lambda p, b: (b, 0, p)),
        out_shape=jax.ShapeDtypeStruct((bsz, lt, d), BF16),
        compiler_params=_cparams("parallel", "parallel"),
        name="natten",
    )(qkv, qkv, qkv, bias)
    return _matmul_gate_residual(o, w_out.astype(BF16), hs, gate, ncb=ncb)


def _final_norm_kernel(x_ref, g_ref, o_ref):
    x = x_ref[0]
    o_ref[0] = x * lax.rsqrt(jnp.mean(x * x, axis=-1, keepdims=True) + EPS) * g_ref[...]


def _final_norm(hs, g, *, ncb):
    bsz, lt, d = hs.shape
    nb = lt // TM - ncb
    return pl.pallas_call(
        _final_norm_kernel,
        grid=(bsz, nb),
        in_specs=[
            pl.BlockSpec((1, TM, d), lambda b, i: (b, i + ncb, 0)),
            pl.BlockSpec((1, d), lambda b, i: (0, 0)),
        ],
        out_specs=pl.BlockSpec((1, TM, d), lambda b, i: (b, i, 0)),
        out_shape=jax.ShapeDtypeStruct((bsz, nb * TM, d), F32),
        compiler_params=_cparams("parallel", "parallel"),
        name="final_norm",
    )(hs, g.reshape(1, d))


def _layer_mods(mod_rows, bsz, d):
    mx = mod_rows[:bsz].reshape(bsz, 6, d)
    mc = jnp.broadcast_to(mod_rows[bsz].reshape(1, 6, d), (bsz, 6, d))
    m = jnp.stack([mc, mx], axis=1)
    return [m[:, :, k, None, :] for k in range(6)]


def kernel(x, c, ctx, c_ctx, ada_w, ada_b, norm1_g, norm2_g, mla_w_in, mla_q_norm_g, mla_w_q_up, mla_kv_norm_g, mla_w_kv_up, mla_w_out, s5_w_in, s5_lambda_re, s5_lambda_im, s5_log_dt, s5_b_re, s5_b_im, s5_c_re, s5_c_im, s5_d, s5_w_glu, hg_w_in, hg_lower_bound, hg_norm_g, hg_w_out, na_w_qkv, na_rpb, na_w_out, ffn_w_up, ffn_conv_w, ffn_conv_b, ffn_w_down, final_g):
    bsz, l, d = x.shape
    lc = ctx.shape[1]
    depth = ada_w.shape[0]
    assert lc % TM == 0 and l % TM == 0 and l % GRID_W == 0 and bsz + 1 <= 16
    ncb = lc // TM
    hs = jnp.concatenate([ctx, x], axis=1)
    cond_rows = jnp.zeros((16, d), F32).at[:bsz].set(c).at[bsz].set(c_ctx)
    mod_all = _ada_mod(cond_rows, ada_w, ada_b)
    lb_cum = jnp.cumsum(jax.nn.softmax(hg_lower_bound.astype(F32), axis=0), axis=0)
    lower_bounds = lb_cum - lb_cum[0]
    for i in range(depth):
        kind, j = i % 4, i // 4
        sh1, sc1, g1, sh2, sc2, g2 = _layer_mods(mod_all[i], bsz, d)
        if kind == 0:
            hs = _mla_layer(hs, norm1_g[i], sh1, sc1, g1, mla_w_in[j], mla_q_norm_g[j], mla_w_q_up[j],
                            mla_kv_norm_g[j], mla_w_kv_up[j], mla_w_out[j], ncb=ncb, lc=lc)
        elif kind == 1:
            hs = _s5_layer(hs, norm1_g[i], sh1, sc1, g1, s5_w_in[j], s5_lambda_re[j], s5_lambda_im[j],
                           s5_log_dt[j], s5_b_re[j], s5_b_im[j], s5_c_re[j], s5_c_im[j], s5_d[j], s5_w_glu[j],
                           ncb=ncb, lc=lc)
        elif kind == 2:
            hs = _hgrn2_layer(hs, norm1_g[i], sh1, sc1, g1, hg_w_in[j], lower_bounds[i], hg_norm_g[j],
                              hg_w_out[j], ncb=ncb, lc=lc)
        else:
            hs = _natten_layer(hs, norm1_g[i], sh1, sc1, g1, na_w_qkv[j], na_rpb[j], na_w_out[j],
                               ncb=ncb, lc=lc)
        hs = _conv_ffn(hs, norm2_g[i], sh2, sc2, g2, ffn_w_up[i], ffn_conv_w[i], ffn_conv_b[i],
                       ffn_w_down[i], ncb=ncb)
    return _final_norm(hs, final_g, ncb=ncb)
```

```python
import functools
import math

import jax
import jax.numpy as jnp
import numpy as np
from jax import lax
from jax.experimental import pallas as pl
from jax.experimental.pallas import tpu as pltpu

F32 = jnp.float32
BF16 = jnp.bfloat16
HI = lax.Precision.HIGHEST

EPS = 1e-6
GRID_W = 64
ROPE_THETA = 10000.0
LOG2E = math.log2(math.e)

LANES = 128
VMEM_LIMIT_BYTES = 56 * 1024 * 1024
TM = 256

MLA_HEADS = 16
MLA_Q_RANK = 384
MLA_KV_RANK = 256
MLA_NOPE = 64
MLA_ROPE = 32
MLA_V = 64
MLA_QK = MLA_NOPE + MLA_ROPE
MLA_HEADS_PER_STEP = 4

S5_GROUP_CH = 16
S5_STATE = 64
S5_T = 16

HG_HEAD_DIM = 128
HG_CHUNK = 64
HG_HEADS_PER_STEP = 4

NA_HEADS = 16
NA_HEAD_DIM = 64
NA_KH = 8
NA_KW = 16
NEG_BIG = -1e30
NA_BLK_ROWS = 8
NA_WIN_ROWS = 16
NA_BLK_UNROLL = 2


def _cparams(*sem):
    return pltpu.CompilerParams(dimension_semantics=sem, vmem_limit_bytes=VMEM_LIMIT_BYTES)


def _seg(ncb):
    return lambda i: jnp.where(i >= ncb, 1, 0)


def _norm_mod(x, g, sh, sc):
    ms = jnp.mean(x * x, axis=-1, keepdims=True)
    return (x * lax.rsqrt(ms + EPS) * g) * (1.0 + sc) + sh


def _sigmoid(x):
    return 1.0 / (1.0 + jnp.exp(-x))


def _dot(a, b):
    return jnp.dot(a, b, preferred_element_type=F32)


def _dot_nt(a, b):
    return lax.dot_general(a, b, (((1,), (1,)), ((), ())), preferred_element_type=F32)


def _dot_tn(a, b):
    return lax.dot_general(a, b, (((0,), (0,)), ((), ())), preferred_element_type=F32)


def _ada_kernel(cond_ref, w_ref, b_ref, o_ref):
    cond = cond_ref[...]
    a = (cond * _sigmoid(cond)).astype(BF16)
    o_ref[0] = _dot(a, w_ref[0].astype(BF16)) + b_ref[0]


def _ada_mod(cond_rows, ada_w, ada_b):
    depth, d, n = ada_w.shape
    rows = cond_rows.shape[0]
    tn = 1536
    return pl.pallas_call(
        _ada_kernel,
        grid=(depth, n // tn),
        in_specs=[
            pl.BlockSpec((rows, d), lambda i, j: (0, 0)),
            pl.BlockSpec((1, d, tn), lambda i, j: (i, 0, j)),
            pl.BlockSpec((1, 1, tn), lambda i, j: (i, 0, j)),
        ],
        out_specs=pl.BlockSpec((1, rows, tn), lambda i, j: (i, 0, j)),
        out_shape=jax.ShapeDtypeStruct((depth, rows, n), F32),
        compiler_params=_cparams("parallel", "parallel"),
        name="ada_mod",
    )(cond_rows, ada_w, ada_b.reshape(depth, 1, n))


def _nmm_kernel(x_ref, g_ref, sh_ref, sc_ref, w_ref, cs_ref, *o_refs, tn):
    u = _norm_mod(x_ref[0], g_ref[...], sh_ref[0, 0], sc_ref[0, 0]).astype(BF16)
    n = w_ref.shape[1]
    for j in range(n // tn):
        sl = slice(j * tn, (j + 1) * tn)
        r = _dot(u, w_ref[:, sl]) * cs_ref[:, sl]
        for o_ref in o_refs:
            o_ref[0, :, sl] = r.astype(o_ref.dtype)


def _norm_mod_matmul(hs, g, sh, sc, w, col_scale, *, ncb, out_dtype, tn):
    bsz, lt, d = hs.shape
    n = w.shape[1]
    seg = _seg(ncb)
    multi = isinstance(out_dtype, tuple)
    dtypes = out_dtype if multi else (out_dtype,)
    outs = pl.pallas_call(
        functools.partial(_nmm_kernel, tn=tn),
        grid=(bsz, lt // TM),
        in_specs=[
            pl.BlockSpec((1, TM, d), lambda b, i: (b, i, 0)),
            pl.BlockSpec((1, d), lambda b, i: (0, 0)),
            pl.BlockSpec((1, 1, 1, d), lambda b, i: (b, seg(i), 0, 0)),
            pl.BlockSpec((1, 1, 1, d), lambda b, i: (b, seg(i), 0, 0)),
            pl.BlockSpec((d, n), lambda b, i: (0, 0)),
            pl.BlockSpec((1, n), lambda b, i: (0, 0)),
        ],
        out_specs=[pl.BlockSpec((1, TM, n), lambda b, i: (b, i, 0)) for _ in dtypes],
        out_shape=[jax.ShapeDtypeStruct((bsz, lt, n), dt) for dt in dtypes],
        compiler_params=_cparams("parallel", "parallel"),
        name="norm_mod_matmul",
    )(hs, g.reshape(1, d), sh, sc, w, col_scale)
    return tuple(outs) if multi else outs[0]


def _mgr_kernel(a_ref, w_ref, x_ref, gate_ref, o_ref):
    o_ref[0] = x_ref[0] + gate_ref[0, 0] * _dot(a_ref[0], w_ref[...])


def _matmul_gate_residual(a, w, hs, gate, *, ncb):
    bsz, lt, d = hs.shape
    k = a.shape[2]
    seg = _seg(ncb)
    return pl.pallas_call(
        _mgr_kernel,
        grid=(bsz, lt // TM),
        in_specs=[
            pl.BlockSpec((1, TM, k), lambda b, i: (b, i, 0)),
            pl.BlockSpec((k, d), lambda b, i: (0, 0)),
            pl.BlockSpec((1, TM, d), lambda b, i: (b, i, 0)),
            pl.BlockSpec((1, 1, 1, d), lambda b, i: (b, seg(i), 0, 0)),
        ],
        out_specs=pl.BlockSpec((1, TM, d), lambda b, i: (b, i, 0)),
        out_shape=jax.ShapeDtypeStruct((bsz, lt, d), F32),
        compiler_params=_cparams("parallel", "parallel"),
        name="matmul_gate_residual",
    )(a, w, hs, gate)


FFN_HALO = 16
FFN_PAD = 8
FFN_TILE = 256


def _ffn_kernel(x_ref, xp_ref, xn_ref, g_ref, sh_ref, sc_ref, gate_ref, wup_ref, cw_ref, cb_ref,
                wdn_ref, fg_ref, o_ref, u_scr, h_scr, *, ncb, nb, off, final):
    i = pl.program_id(1) + off
    g = g_ref[...]
    sh = sh_ref[0, 0]
    sc = sc_ref[0, 0]
    f = wdn_ref.shape[0]
    keep_prev = jnp.where((i == 0) | (i == ncb), 0.0, 1.0)
    keep_next = jnp.where((i == ncb - 1) | (i == nb - 1), 0.0, 1.0)
    u_prev = _norm_mod(xp_ref[0], g, sh, sc)[FFN_HALO - 1:FFN_HALO] * keep_prev
    u_next = _norm_mod(xn_ref[0], g, sh, sc)[0:1] * keep_next
    row = lax.broadcasted_iota(jnp.int32, (FFN_HALO, u_prev.shape[1]), 0)
    extra = jnp.where(row == 0, u_prev, jnp.where(row == 1, u_next, 0.0))
    u_scr[0:TM, :] = _norm_mod(x_ref[0], g, sh, sc).astype(BF16)
    u_scr[TM:, :] = extra.astype(BF16)
    h = _dot(u_scr[...], wup_ref[...])
    h_scr[FFN_PAD:FFN_PAD + TM, :] = h[0:TM]
    h_scr[FFN_PAD - 1:FFN_PAD, :] = h[TM:TM + 1]
    h_scr[FFN_PAD + TM:FFN_PAD + TM + 1, :] = h[TM + 1:TM + 2]
    w = cw_ref[...]
    hc = (h_scr[FFN_PAD - 1:FFN_PAD - 1 + TM, :] * w[0:1] + h_scr[FFN_PAD:FFN_PAD + TM, :] * w[1:2]
          + h_scr[FFN_PAD + 1:FFN_PAD + 1 + TM, :] * w[2:3] + cb_ref[...])
    pieces = []
    for j in range(f // FFN_TILE):
        a = hc[:, 2 * j * FFN_TILE:(2 * j + 1) * FFN_TILE]
        gg = hc[:, (2 * j + 1) * FFN_TILE:(2 * j + 2) * FFN_TILE]
        pieces.append((a * (gg * _sigmoid(gg))).astype(BF16))
    act = jnp.concatenate(pieces, axis=1)
    y = x_ref[0] + gate_ref[0, 0] * _dot(act, wdn_ref[...])
    if final:
        y = y * lax.rsqrt(jnp.mean(y * y, axis=-1, keepdims=True) + EPS) * fg_ref[...]
    o_ref[0] = y


def _conv_ffn(hs, g, sh, sc, gate, w_up, conv_w, conv_b, w_down, final_g=None, *, ncb):
    bsz, lt, d = hs.shape
    f = w_down.shape[0]
    nb = lt // TM
    final = final_g is not None
    off = ncb if final else 0
    hb = TM // FFN_HALO
    nhalo = lt // FFN_HALO
    seg = _seg(ncb)
    fg = (final_g if final else jnp.ones((d,), F32)).reshape(1, d)

    def il(t):
        r = t.shape[0]
        return t.reshape(r, 2, f // FFN_TILE, FFN_TILE).transpose(0, 2, 1, 3).reshape(r, 2 * f)
    return pl.pallas_call(
        functools.partial(_ffn_kernel, ncb=ncb, nb=nb, off=off, final=final),
        grid=(bsz, nb - off),
        in_specs=[
            pl.BlockSpec((1, TM, d), lambda b, i: (b, i + off, 0)),
            pl.BlockSpec((1, FFN_HALO, d), lambda b, i: (b, jnp.maximum((i + off) * hb - 1, 0), 0)),
            pl.BlockSpec((1, FFN_HALO, d), lambda b, i: (b, jnp.minimum((i + off + 1) * hb, nhalo - 1), 0)),
            pl.BlockSpec((1, d), lambda b, i: (0, 0)),
            pl.BlockSpec((1, 1, 1, d), lambda b, i: (b, seg(i + off), 0, 0)),
            pl.BlockSpec((1, 1, 1, d), lambda b, i: (b, seg(i + off), 0, 0)),
            pl.BlockSpec((1, 1, 1, d), lambda b, i: (b, seg(i + off), 0, 0)),
            pl.BlockSpec((d, 2 * f), lambda b, i: (0, 0)),
            pl.BlockSpec((3, 2 * f), lambda b, i: (0, 0)),
            pl.BlockSpec((1, 2 * f), lambda b, i: (0, 0)),
            pl.BlockSpec((f, d), lambda b, i: (0, 0)),
            pl.BlockSpec((1, d), lambda b, i: (0, 0)),
        ],
        out_specs=pl.BlockSpec((1, TM, d), lambda b, i: (b, i, 0)),
        out_shape=jax.ShapeDtypeStruct((bsz, lt - off * TM, d), F32),
        scratch_shapes=[
            pltpu.VMEM((TM + FFN_HALO, d), BF16),
            pltpu.VMEM((TM + 2 * FFN_PAD, 2 * f), F32),
        ],
        compiler_params=_cparams("parallel", "parallel"),
        name="conv_ffn",
    )(hs, hs, hs, g.reshape(1, d), sh, sc, gate, il(w_up.astype(BF16)), il(conv_w), il(conv_b.reshape(1, 2 * f)),
      w_down.astype(BF16), fg)


def _rope_tables(lc, l):
    half = MLA_ROPE // 4
    inv = 1.0 / (ROPE_THETA ** (jnp.arange(half, dtype=F32) / half))
    t = jnp.arange(l)
    ang_r = (t // GRID_W).astype(F32)[:, None] * inv[None, :]
    ang_c = (t % GRID_W).astype(F32)[:, None] * inv[None, :]
    cos = jnp.concatenate([jnp.cos(ang_r), jnp.cos(ang_r), jnp.cos(ang_c), jnp.cos(ang_c)], axis=-1)
    sin = jnp.concatenate([-jnp.sin(ang_r), jnp.sin(ang_r), -jnp.sin(ang_c), jnp.sin(ang_c)], axis=-1)
    cos_t = jnp.ones((lc + l, LANES), F32).at[lc:, MLA_NOPE:MLA_NOPE + MLA_ROPE].set(cos)
    sin_t = jnp.zeros((lc + l, LANES), F32).at[lc:, MLA_NOPE:MLA_NOPE + MLA_ROPE].set(sin)
    return cos_t, sin_t


def _mla_proj_kernel(x_ref, g_ref, sh_ref, sc_ref, win_ref, qg_ref, wq_ref, kvg_ref, wkn_ref, wv_ref,
                     cos_ref, sin_ref, q_ref, k_ref, v_ref):
    u = _norm_mod(x_ref[0], g_ref[...], sh_ref[0, 0], sc_ref[0, 0]).astype(BF16)
    lat = _dot(u, win_ref[...])
    q_lat = lat[:, :MLA_Q_RANK]
    kv_lat = lat[:, MLA_Q_RANK:MLA_Q_RANK + MLA_KV_RANK]
    kr = lat[:, MLA_Q_RANK + MLA_KV_RANK:]

    def rms(t, gg):
        return (t * lax.rsqrt(jnp.mean(t * t, axis=-1, keepdims=True) + EPS) * gg).astype(BF16)

    qn = rms(q_lat, qg_ref[...])
    kvn = rms(kv_lat, kvg_ref[...])
    cos = cos_ref[...]
    sin = sin_ref[...]
    lane = lax.broadcasted_iota(jnp.int32, (TM, LANES), 1)
    first_half = (lane % (MLA_ROPE // 2)) < (MLA_ROPE // 4)

    def rope(t):
        partner = jnp.where(first_half, pltpu.roll(t, LANES - MLA_ROPE // 4, 1), pltpu.roll(t, MLA_ROPE // 4, 1))
        return t * cos + partner * sin

    kr_rot = rope(kr)
    v_ref[0] = _dot(kvn, wv_ref[...]).astype(BF16)
    q_all = _dot(qn, wq_ref[...]) * ((MLA_QK ** -0.5) * LOG2E)
    kn_all = _dot(kvn, wkn_ref[...])
    for h in range(MLA_HEADS):
        sl = slice(h * LANES, (h + 1) * LANES)
        q_ref[0, :, sl] = rope(q_all[:, sl]).astype(BF16)
        k_ref[0, :, sl] = (kn_all[:, sl] + kr_rot).astype(BF16)


def _mla_attn_kernel(q_ref, k_ref, v_ref, o_ref, *, ncb, lc):
    i = pl.program_id(2)

    def attend(lk):
        q = q_ref[0]
        lane = lax.broadcasted_iota(jnp.int32, (q.shape[0], LANES), 1)
        for pair in range(q.shape[1] // (2 * LANES)):
            v = v_ref[0, 0:lk, pair * LANES:(pair + 1) * LANES]
            outs = []
            for e in range(2):
                sl = slice((2 * pair + e) * LANES, (2 * pair + e + 1) * LANES)
                s = _dot_nt(q[:, sl], k_ref[0, 0:lk, sl])
                m = jnp.max(s, axis=-1, keepdims=True)
                p = jnp.exp2(s - m)
                l = jnp.sum(p, axis=-1, keepdims=True)
                outs.append(_dot(p.astype(BF16), v) / l)
            o_ref[0, :, pair * LANES:(pair + 1) * LANES] = jnp.where(lane < MLA_V, outs[0], outs[1]).astype(o_ref.dtype)

    @pl.when(i < ncb)
    def _():
        attend(lc)

    @pl.when(i >= ncb)
    def _():
        attend(k_ref.shape[1])


def _mla_layer(hs, g, sh, sc, gate, w_in, q_norm_g, w_q_up, kv_norm_g, w_kv_up, w_out, *, ncb, lc):
    bsz, lt, d = hs.shape
    l = lt - lc
    nh = MLA_HEADS
    seg = _seg(ncb)
    w_in_p = jnp.zeros((d, 768), F32)
    w_in_p = w_in_p.at[:, :MLA_Q_RANK + MLA_KV_RANK].set(w_in[:, :MLA_Q_RANK + MLA_KV_RANK])
    w_in_p = w_in_p.at[:, 640 + MLA_NOPE:640 + MLA_QK].set(w_in[:, MLA_Q_RANK + MLA_KV_RANK:])
    wq = w_q_up.reshape(MLA_Q_RANK, nh, MLA_QK)
    wq = jnp.pad(wq, ((0, 0), (0, 0), (0, LANES - MLA_QK))).reshape(MLA_Q_RANK, nh * LANES)
    wkv = w_kv_up.reshape(MLA_KV_RANK, nh, MLA_NOPE + MLA_V)
    wkn = jnp.pad(wkv[:, :, :MLA_NOPE], ((0, 0), (0, 0), (0, LANES - MLA_NOPE))).reshape(MLA_KV_RANK, nh * LANES)
    wv = wkv[:, :, MLA_NOPE:].reshape(MLA_KV_RANK, nh * MLA_V)
    cos_t, sin_t = _rope_tables(lc, l)

    q, k, v = pl.pallas_call(
        _mla_proj_kernel,
        grid=(bsz, lt // TM),
        in_specs=[
            pl.BlockSpec((1, TM, d), lambda b, i: (b, i, 0)),
            pl.BlockSpec((1, d), lambda b, i: (0, 0)),
            pl.BlockSpec((1, 1, 1, d), lambda b, i: (b, seg(i), 0, 0)),
            pl.BlockSpec((1, 1, 1, d), lambda b, i: (b, seg(i), 0, 0)),
            pl.BlockSpec((d, 768), lambda b, i: (0, 0)),
            pl.BlockSpec((1, MLA_Q_RANK), lambda b, i: (0, 0)),
            pl.BlockSpec((MLA_Q_RANK, nh * LANES), lambda b, i: (0, 0)),
            pl.BlockSpec((1, MLA_KV_RANK), lambda b, i: (0, 0)),
            pl.BlockSpec((MLA_KV_RANK, nh * LANES), lambda b, i: (0, 0)),
            pl.BlockSpec((MLA_KV_RANK, nh * MLA_V), lambda b, i: (0, 0)),
            pl.BlockSpec((TM, LANES), lambda b, i: (i, 0)),
            pl.BlockSpec((TM, LANES), lambda b, i: (i, 0)),
        ],
        out_specs=[
            pl.BlockSpec((1, TM, nh * LANES), lambda b, i: (b, i, 0)),
            pl.BlockSpec((1, TM, nh * LANES), lambda b, i: (b, i, 0)),
            pl.BlockSpec((1, TM, nh * MLA_V), lambda b, i: (b, i, 0)),
        ],
        out_shape=[
            jax.ShapeDtypeStruct((bsz, lt, nh * LANES), BF16),
            jax.ShapeDtypeStruct((bsz, lt, nh * LANES), BF16),
            jax.ShapeDtypeStruct((bsz, lt, nh * MLA_V), BF16),
        ],
        compiler_params=_cparams("parallel", "parallel"),
        name="mla_proj",
    )(hs, g.reshape(1, d), sh, sc, w_in_p.astype(BF16), q_norm_g.reshape(1, -1), wq.astype(BF16),
      kv_norm_g.reshape(1, -1), wkn.astype(BF16), wv.astype(BF16), cos_t, sin_t)

    o = pl.pallas_call(
        functools.partial(_mla_attn_kernel, ncb=ncb, lc=lc),
        grid=(bsz, nh // MLA_HEADS_PER_STEP, lt // TM),
        in_specs=[
            pl.BlockSpec((1, TM, MLA_HEADS_PER_STEP * LANES), lambda b, p, i: (b, i, p)),
            pl.BlockSpec((1, lt, MLA_HEADS_PER_STEP * LANES), lambda b, p, i: (b, 0, p)),
            pl.BlockSpec((1, lt, MLA_HEADS_PER_STEP * MLA_V), lambda b, p, i: (b, 0, p)),
        ],
        out_specs=pl.BlockSpec((1, TM, MLA_HEADS_PER_STEP * MLA_V), lambda b, p, i: (b, i, p)),
        out_shape=jax.ShapeDtypeStruct((bsz, lt, nh * MLA_V), BF16),
        compiler_params=_cparams("parallel", "parallel", "parallel"),
        name="mla_attn",
    )(q, k, v)
    return _matmul_gate_residual(o, w_out.astype(BF16), hs, gate, ncb=ncb)


def _s5_matrices(lam_re, lam_im, log_dt, b_re, b_im, c_re, c_im):
    t_len = S5_T
    n_grp = lam_re.shape[1]
    ch = S5_GROUP_CH
    n_st = S5_STATE
    dt = jnp.exp(log_dt.astype(F32))[..., None]
    ld_re = lam_re.astype(F32) * dt
    ld_im = lam_im.astype(F32) * dt
    tau = jnp.arange(t_len + 1, dtype=F32)[:, None, None, None]
    mag = jnp.exp(tau * ld_re[None])
    pw_re = mag * jnp.cos(tau * ld_im[None])
    pw_im = mag * jnp.sin(tau * ld_im[None])
    lb_re, lb_im = pw_re[1] - 1.0, pw_im[1]
    den = lam_re * lam_re + lam_im * lam_im
    f_re = (lb_re * lam_re + lb_im * lam_im) / den
    f_im = (lb_im * lam_re - lb_re * lam_im) / den
    bb_re = f_re[..., None] * b_re - f_im[..., None] * b_im
    bb_im = f_re[..., None] * b_im + f_im[..., None] * b_re
    cp_re = c_re[None] * pw_re[:t_len, :, :, None, :] - c_im[None] * pw_im[:t_len, :, :, None, :]
    cp_im = c_re[None] * pw_im[:t_len, :, :, None, :] + c_im[None] * pw_re[:t_len, :, :, None, :]
    taps = (jnp.einsum("tdgon,dgni->tdgoi", cp_re, bb_re, precision=HI)
            - jnp.einsum("tdgon,dgni->tdgoi", cp_im, bb_im, precision=HI))
    s_idx = np.arange(t_len)[:, None, None]
    t_idx = np.arange(t_len)[None, :, None]
    u_idx = np.arange(t_len)[None, None, :]
    sel_f = jnp.asarray((t_idx - s_idx == u_idx).astype(np.float32))
    sel_b = jnp.asarray((s_idx - t_idx == u_idx).astype(np.float32))
    kf = jnp.einsum("stu,ugoi->stgoi", sel_f, taps[:, 0], precision=HI)
    kb = jnp.einsum("stu,ugoi->stgoi", sel_b, taps[:, 1], precision=HI)
    k_tot = (kf + kb).transpose(2, 0, 4, 1, 3).reshape(n_grp, t_len * ch, t_len * ch)

    def state_in(pw_r, pw_i, brr, bii):
        re = pw_r[..., None] * brr[None] - pw_i[..., None] * bii[None]
        im = pw_r[..., None] * bii[None] + pw_i[..., None] * brr[None]
        return (re.transpose(1, 0, 3, 2).reshape(n_grp, t_len * ch, n_st),
                im.transpose(1, 0, 3, 2).reshape(n_grp, t_len * ch, n_st))

    inf_re, inf_im = state_in(jnp.flip(pw_re[:t_len, 0], 0), jnp.flip(pw_im[:t_len, 0], 0), bb_re[0], bb_im[0])
    inb_re, inb_im = state_in(pw_re[:t_len, 1], pw_im[:t_len, 1], bb_re[1], bb_im[1])

    def state_out(pw_r, pw_i, crr, cii):
        re = crr[None] * pw_r[:, :, None, :] - cii[None] * pw_i[:, :, None, :]
        im = crr[None] * pw_i[:, :, None, :] + cii[None] * pw_r[:, :, None, :]
        return (re.transpose(1, 3, 0, 2).reshape(n_grp, n_st, t_len * ch),
                (-im).transpose(1, 3, 0, 2).reshape(n_grp, n_st, t_len * ch))

    outf_re, outf_im = state_out(pw_re[1:, 0], pw_im[1:, 0], c_re[0], c_im[0])
    outb_re, outb_im = state_out(jnp.flip(pw_re[1:, 1], 0), jnp.flip(pw_im[1:, 1], 0), c_re[1], c_im[1])
    odd = (jnp.arange(n_grp) % 2 == 1)[:, None, None]

    def pad_cols(m):
        z = jnp.zeros_like(m)
        return jnp.where(odd, jnp.concatenate([z, m], -1), jnp.concatenate([m, z], -1))

    def pad_rows(m):
        z = jnp.zeros_like(m)
        return jnp.where(odd, jnp.concatenate([z, m], 1), jnp.concatenate([m, z], 1))

    w_in = jnp.concatenate([pad_cols(inf_re), pad_cols(inf_im), pad_cols(inb_re), pad_cols(inb_im)], -1)
    m_out = jnp.stack([pad_rows(outf_re), pad_rows(outf_im), pad_rows(outb_re), pad_rows(outb_im)], 1)
    decay = jnp.stack([pw_re[t_len, 0], pw_im[t_len, 0], pw_re[t_len, 1], pw_im[t_len, 1]], 0)
    decay = decay.reshape(4, 1, n_grp * n_st)
    return w_in.astype(BF16), k_tot.astype(BF16), m_out.astype(BF16), decay


def _s5_in_kernel(z_ref, w_ref, u_ref):
    for k in range(4):
        sl = slice(k * LANES, (k + 1) * LANES)
        u_ref[k] = _dot(z_ref[0], w_ref[0, :, sl]) + _dot(z_ref[1], w_ref[1, :, sl])


def _s5_scan_kernel(u_ref, a_ref, p_ref, *, jc, jt):
    afr, afi, abr, abi = a_ref[0], a_ref[1], a_ref[2], a_ref[3]
    zero = jnp.zeros(p_ref.shape[2:], F32)

    def body(jj, carry):
        fr, fi, br, bi = carry
        jb = jnp.where(jj < jc, jc - 1 - jj, jt - 1 - jj + jc)
        p_ref[0, jj] = fr
        p_ref[1, jj] = fi
        p_ref[2, jb] = br
        p_ref[3, jb] = bi
        nfr = afr * fr - afi * fi + u_ref[0, jj]
        nfi = afr * fi + afi * fr + u_ref[1, jj]
        nbr = abr * br - abi * bi + u_ref[2, jb]
        nbi = abr * bi + abi * br + u_ref[3, jb]
        return nfr, nfi, nbr, nbi

    lax.fori_loop(0, jt, body, (zero, zero, zero, zero))


def _s5_out_kernel(z_ref, p_ref, k_ref, m_ref, y_ref):
    for e in range(2):
        acc = _dot(z_ref[e], k_ref[e])
        for k in range(4):
            acc += _dot(p_ref[k].astype(BF16), m_ref[e, k])
        y_ref[e] = acc.astype(y_ref.dtype)


def _s5_glu_kernel(y_ref, z_ref, d_ref, w_ref, x_ref, gate_ref, o_ref):
    d = x_ref.shape[2]
    y = y_ref[0].astype(F32) + z_ref[0] * d_ref[...]
    ge = jax.nn.gelu(y).astype(BF16)
    ag = _dot(ge, w_ref[...])
    o_ref[0] = x_ref[0] + gate_ref[0, 0] * (ag[:, :d] * _sigmoid(ag[:, d:]))


def _s5_layer(hs, g, sh, sc, gate, w_in, lam_re, lam_im, log_dt, b_re, b_im, c_re, c_im, d_skip, w_glu,
              *, ncb, lc):
    bsz, lt, d = hs.shape
    width = w_in.shape[1]
    n_grp = width // S5_GROUP_CH
    jt = lt // S5_T
    jc = lc // S5_T
    rows = jt * bsz
    cols = S5_T * S5_GROUP_CH
    seg = _seg(ncb)
    z, z16 = _norm_mod_matmul(hs, g, sh, sc, w_in.astype(BF16), jnp.ones((1, width), F32),
                              ncb=ncb, out_dtype=(F32, BF16), tn=width)
    w_si, k_tot, m_out, decay = _s5_matrices(lam_re, lam_im, log_dt, b_re, b_im, c_re, c_im)
    zg = z16.reshape(bsz, jt, S5_T, n_grp, S5_GROUP_CH).transpose(3, 1, 0, 2, 4)
    zg = zg.reshape(n_grp, rows, cols)
    n_state = n_grp * S5_STATE
    u = pl.pallas_call(
        _s5_in_kernel,
        grid=(n_grp // 2,),
        in_specs=[
            pl.BlockSpec((2, rows, cols), lambda p: (p, 0, 0)),
            pl.BlockSpec((2, cols, 4 * LANES), lambda p: (p, 0, 0)),
        ],
        out_specs=pl.BlockSpec((4, rows, LANES), lambda p: (0, 0, p)),
        out_shape=jax.ShapeDtypeStruct((4, rows, n_state), F32),
        compiler_params=_cparams("parallel"),
        name="s5_state_in",
    )(zg, w_si)
    p_state = pl.pallas_call(
        functools.partial(_s5_scan_kernel, jc=jc, jt=jt),
        grid=(n_state // LANES,),
        in_specs=[
            pl.BlockSpec((4, jt, bsz, LANES), lambda p: (0, 0, 0, p)),
            pl.BlockSpec((4, 1, LANES), lambda p: (0, 0, p)),
        ],
        out_specs=pl.BlockSpec((4, jt, bsz, LANES), lambda p: (0, 0, 0, p)),
        out_shape=jax.ShapeDtypeStruct((4, jt, bsz, n_state), F32),
        compiler_params=_cparams("parallel"),
        name="s5_scan",
    )(u.reshape(4, jt, bsz, n_state), decay)
    yg = pl.pallas_call(
        _s5_out_kernel,
        grid=(n_grp // 2,),
        in_specs=[
            pl.BlockSpec((2, rows, cols), lambda p: (p, 0, 0)),
            pl.BlockSpec((4, rows, LANES), lambda p: (0, 0, p)),
            pl.BlockSpec((2, cols, cols), lambda p: (p, 0, 0)),
            pl.BlockSpec((2, 4, LANES, cols), lambda p: (p, 0, 0, 0)),
        ],
        out_specs=pl.BlockSpec((2, rows, cols), lambda p: (p, 0, 0)),
        out_shape=jax.ShapeDtypeStruct((n_grp, rows, cols), BF16),
        compiler_params=_cparams("parallel"),
        name="s5_chunk_out",
    )(zg, p_state.reshape(4, rows, n_state), k_tot, m_out)
    y = yg.reshape(n_grp, jt, bsz, S5_T, S5_GROUP_CH).transpose(2, 1, 3, 0, 4).reshape(bsz, lt, width)
    return pl.pallas_call(
        _s5_glu_kernel,
        grid=(bsz, lt // TM),
        in_specs=[
            pl.BlockSpec((1, TM, width), lambda b, i: (b, i, 0)),
            pl.BlockSpec((1, TM, width), lambda b, i: (b, i, 0)),
            pl.BlockSpec((1, width), lambda b, i: (0, 0)),
            pl.BlockSpec((width, 2 * d), lambda b, i: (0, 0)),
            pl.BlockSpec((1, TM, d), lambda b, i: (b, i, 0)),
            pl.BlockSpec((1, 1, 1, d), lambda b, i: (b, seg(i), 0, 0)),
        ],
        out_specs=pl.BlockSpec((1, TM, d), lambda b, i: (b, i, 0)),
        out_shape=jax.ShapeDtypeStruct((bsz, lt, d), F32),
        compiler_params=_cparams("parallel", "parallel"),
        name="s5_glu",
    )(y, z, d_skip.reshape(1, width), w_glu.astype(BF16), hs, gate)


def _gla_dir(zq, zf, zv, lb, states, incl, tri, rev):
    dk = HG_HEAD_DIM
    width = zq.shape[1]
    nchunks = zq.shape[0] // HG_CHUNK
    forget = lb + (1.0 - lb) * _sigmoid(zf)
    lf = jnp.log(forget)
    kk = 1.0 - forget
    v_all = zv.astype(BF16)
    hi = lf.astype(BF16)
    r1 = lf - hi.astype(F32)
    mid = r1.astype(BF16)
    lo = (r1 - mid.astype(F32)).astype(BF16)
    parts = _dot(tri, jnp.concatenate([hi, mid, lo], axis=1))
    bcum = parts[:, :width] + parts[:, width:2 * width] + parts[:, 2 * width:]
    btot = jnp.concatenate(
        [jnp.broadcast_to(bcum[c * HG_CHUNK:c * HG_CHUNK + 1] if rev else bcum[(c + 1) * HG_CHUNK - 1:(c + 1) * HG_CHUNK],
                          (HG_CHUNK, width)) for c in range(nchunks)], axis=0)
    q_in_all = (zq * _sigmoid(zq)) * (dk ** -0.5) * jnp.exp(bcum)
    k_in_all = (kk * jnp.exp(-bcum)).astype(BF16)
    k_out_all = kk * jnp.exp(btot - bcum)
    chunk_of_row = lax.broadcasted_iota(jnp.int32, (zq.shape[0], dk), 0) // HG_CHUNK
    zero = jnp.zeros((zq.shape[0], dk), F32)
    outs, new_states = [], []
    for h, st in enumerate(states):
        sl = slice(h * dk, (h + 1) * dk)
        q_in, k_out, v = q_in_all[:, sl], k_out_all[:, sl], v_all[:, sl]
        att = jnp.where(incl, _dot_nt(q_in.astype(BF16), k_in_all[:, sl]), 0.0)
        o_intra = _dot(att.astype(BF16), v)
        k_cat = jnp.concatenate([jnp.where(chunk_of_row == c, k_out, zero) for c in range(nchunks)], axis=1)
        q_cat = jnp.concatenate([jnp.where(chunk_of_row == c, q_in, zero) for c in range(nchunks)], axis=1)
        ds = _dot_tn(v, k_cat.astype(BF16))
        entering = [None] * nchunks
        for cc in range(nchunks):
            c = nchunks - 1 - cc if rev else cc
            entering[c] = st
            st = st * jnp.exp(btot[c * HG_CHUNK:c * HG_CHUNK + 1, sl]) + ds[:, c * dk:(c + 1) * dk]
        scat = jnp.concatenate(entering, axis=1).astype(BF16)
        outs.append(o_intra + _dot_nt(q_cat.astype(BF16), scat))
        new_states.append(st)
    return outs, new_states


def _gla_kernel(zqf_ref, zff_ref, zvf_ref, zqb_ref, zfb_ref, zvb_ref, lb_ref, of_ref, ob_ref, st_scr):
    @pl.when(pl.program_id(2) == 0)
    def _():
        st_scr[...] = jnp.zeros_like(st_scr)

    n = zqf_ref.shape[1]
    row = lax.broadcasted_iota(jnp.int32, (n, n), 0)
    col = lax.broadcasted_iota(jnp.int32, (n, n), 1)
    same = (row // HG_CHUNK) == (col // HG_CHUNK)
    dk = HG_HEAD_DIM
    for direction, (zq_ref, zf_ref, zv_ref, o_ref) in enumerate(
            ((zqf_ref, zff_ref, zvf_ref, of_ref), (zqb_ref, zfb_ref, zvb_ref, ob_ref))):
        rev = direction == 1
        incl = same & ((col >= row) if rev else (col <= row))
        tri = incl.astype(BF16)
        nheads = zq_ref.shape[2] // dk
        outs, states = _gla_dir(zq_ref[0], zf_ref[0], zv_ref[0], lb_ref[0],
                                [st_scr[direction, h] for h in range(nheads)], incl, tri, rev)
        for h in range(nheads):
            o_ref[0, :, h * dk:(h + 1) * dk] = outs[h]
            st_scr[direction, h] = states[h]


def _hg_out_kernel(of_ref, ob_ref, zg_ref, ng_ref, w_ref, x_ref, gate_ref, o_ref):
    o = of_ref[0] + ob_ref[0]
    gsig = zg_ref[0]
    gsig = gsig * _sigmoid(gsig)
    ng = ng_ref[...]
    parts = []
    for h in range(o.shape[1] // HG_HEAD_DIM):
        sl = slice(h * HG_HEAD_DIM, (h + 1) * HG_HEAD_DIM)
        oh = o[:, sl]
        on = oh * lax.rsqrt(jnp.mean(oh * oh, axis=-1, keepdims=True) + EPS)
        parts.append((on * ng[:, sl] * gsig[:, sl]).astype(BF16))
    a = jnp.concatenate(parts, axis=-1)
    o_ref[0] = x_ref[0] + gate_ref[0, 0] * _dot(a, w_ref[...])


def _hgrn2_layer(hs, g, sh, sc, gate, w_in, lower_bound, norm_g, w_out, *, ncb, lc):
    bsz, lt, d = hs.shape
    nh = d // HG_HEAD_DIM
    nb = lt // TM
    seg = _seg(ncb)
    z = _norm_mod_matmul(hs, g, sh, sc, w_in.astype(BF16), jnp.ones((1, 5 * d), F32),
                         ncb=ncb, out_dtype=F32, tn=1024)
    lb = lower_bound.astype(F32).reshape(1, 1, d)
    hps = HG_HEADS_PER_STEP
    ng = nh // hps
    wb = hps * HG_HEAD_DIM

    def rblk(s):
        return jnp.where(s < ncb, ncb - 1 - s, nb - 1 - s + ncb)

    outs = pl.pallas_call(
        _gla_kernel,
        grid=(bsz, ng, nb),
        in_specs=[
            pl.BlockSpec((1, TM, wb), lambda b, h, s: (b, s, h)),
            pl.BlockSpec((1, TM, wb), lambda b, h, s: (b, s, ng + h)),
            pl.BlockSpec((1, TM, wb), lambda b, h, s: (b, s, 3 * ng + h)),
            pl.BlockSpec((1, TM, wb), lambda b, h, s: (b, rblk(s), h)),
            pl.BlockSpec((1, TM, wb), lambda b, h, s: (b, rblk(s), 2 * ng + h)),
            pl.BlockSpec((1, TM, wb), lambda b, h, s: (b, rblk(s), 3 * ng + h)),
            pl.BlockSpec((1, 1, wb), lambda b, h, s: (0, 0, h)),
        ],
        out_specs=[
            pl.BlockSpec((1, TM, wb), lambda b, h, s: (b, s, h)),
            pl.BlockSpec((1, TM, wb), lambda b, h, s: (b, rblk(s), h)),
        ],
        out_shape=[jax.ShapeDtypeStruct((bsz, lt, d), F32), jax.ShapeDtypeStruct((bsz, lt, d), F32)],
        scratch_shapes=[pltpu.VMEM((2, hps, HG_HEAD_DIM, HG_HEAD_DIM), F32)],
        compiler_params=_cparams("parallel", "parallel", "arbitrary"),
        name="hgrn2_gla",
    )(z, z, z, z, z, z, lb)
    return pl.pallas_call(
        _hg_out_kernel,
        grid=(bsz, nb),
        in_specs=[
            pl.BlockSpec((1, TM, d), lambda b, i: (b, i, 0)),
            pl.BlockSpec((1, TM, d), lambda b, i: (b, i, 0)),
            pl.BlockSpec((1, TM, d), lambda b, i: (b, i, 4)),
            pl.BlockSpec((1, d), lambda b, i: (0, 0)),
            pl.BlockSpec((d, d), lambda b, i: (0, 0)),
            pl.BlockSpec((1, TM, d), lambda b, i: (b, i, 0)),
            pl.BlockSpec((1, 1, 1, d), lambda b, i: (b, seg(i), 0, 0)),
        ],
        out_specs=pl.BlockSpec((1, TM, d), lambda b, i: (b, i, 0)),
        out_shape=jax.ShapeDtypeStruct((bsz, lt, d), F32),
        compiler_params=_cparams("parallel", "parallel"),
        name="hgrn2_out",
    )(outs[0], outs[1], z, norm_g.reshape(1, d), w_out.astype(BF16), hs, gate)


def _na_bias_table(rpb):
    nh = rpb.shape[0]
    w = np.arange(GRID_W)[:, None, None]
    kc = np.arange(GRID_W)[None, :, None]
    co = np.arange(2 * NA_KW - 1)[None, None, :]
    c0 = np.clip(w - NA_KW // 2, 0, GRID_W - NA_KW)
    valid = (kc >= c0) & (kc < c0 + NA_KW)
    onehot = jnp.asarray((valid & (kc - w + (NA_KW - 1) == co)).astype(np.float32))
    t = jnp.einsum("hrc,wkc->hrwk", rpb.astype(F32), onehot, precision=HI) * LOG2E
    t = jnp.where(jnp.asarray(valid[None, None, :, :, 0]), t, NEG_BIG)
    neg = jnp.full_like(t[:, 0], NEG_BIG)
    half, blk = NA_KH // 2, NA_BLK_ROWS
    combos = ([(min(j, half), NA_KH - 1 - j) for j in range(blk)]
              + [(half, NA_KH - 1 - half - j) for j in range(blk)]
              + [(half if j <= half else j, -1 - j) for j in range(blk)])
    entries = []
    for vi, delta in combos:
        rows_kr = []
        for kr in range(NA_WIN_ROWS):
            ro = kr + delta
            rows_kr.append(t[:, ro] if NA_KH - 1 - vi <= ro < 2 * NA_KH - 1 - vi else neg)
        entries.append(jnp.stack(rows_kr, axis=2))
    bias = jnp.stack(entries, axis=0).reshape(len(combos), nh // 2, 2 * GRID_W, NA_WIN_ROWS * GRID_W)
    return bias


def _na_kernel(q_ref, k_ref, v_ref, bias_ref, o_ref, *, lc, rows):
    nblk = rows // NA_BLK_ROWS
    nq = NA_BLK_ROWS * GRID_W
    nwin = NA_WIN_ROWS * GRID_W
    o_ref[0, 0:lc, :] = jnp.zeros((lc, LANES), o_ref.dtype)
    kc = k_ref[0, 0:lc, :]
    vc = v_ref[0, 0:lc, :]
    lane = lax.broadcasted_iota(jnp.int32, (nq, LANES), 1)
    lo_half = lane < NA_HEAD_DIM

    def one_block(i):
        wr0 = jnp.clip(i * NA_BLK_ROWS - NA_KH // 2, 0, rows - NA_WIN_ROWS)
        btype = jnp.where(i == 0, 0, jnp.where(i == nblk - 1, 2, 1))
        qoff = pl.multiple_of(lc + i * nq, GRID_W * NA_KH // 2)
        woff = pl.multiple_of(lc + wr0 * GRID_W, GRID_W * NA_KH // 2)
        q = q_ref[0, pl.ds(qoff, nq), :]
        kw = k_ref[0, pl.ds(woff, nwin), :]
        vw = v_ref[0, pl.ds(woff, nwin), :]
        zq = jnp.zeros_like(q)
        outs = []
        for e in range(2):
            qe = jnp.where(lo_half, q, zq) if e == 0 else jnp.where(lo_half, zq, q)
            s_loc = _dot_nt(qe, kw)
            s_loc = jnp.concatenate(
                [s_loc[j * GRID_W:(j + 1) * GRID_W]
                 + bias_ref[btype * NA_BLK_ROWS + j, 0, e * GRID_W:(e + 1) * GRID_W, :]
                 for j in range(NA_BLK_ROWS)], axis=0)
            s_ctx = _dot_nt(qe, kc)
            m = jnp.maximum(jnp.max(s_loc, axis=-1, keepdims=True), jnp.max(s_ctx, axis=-1, keepdims=True))
            p_loc = jnp.exp2(s_loc - m)
            p_ctx = jnp.exp2(s_ctx - m)
            l = jnp.sum(p_loc, axis=-1, keepdims=True) + jnp.sum(p_ctx, axis=-1, keepdims=True)
            outs.append((_dot(p_loc.astype(BF16), vw) + _dot(p_ctx.astype(BF16), vc)) / l)
        o_ref[0, pl.ds(qoff, nq), :] = jnp.where(lo_half, outs[0], outs[1]).astype(o_ref.dtype)

    def body(ii, carry):
        for j in range(NA_BLK_UNROLL):
            one_block(ii * NA_BLK_UNROLL + j)
        return carry

    lax.fori_loop(0, nblk // NA_BLK_UNROLL, body, 0)


def _natten_layer(hs, g, sh, sc, gate, w_qkv, rpb, w_out, *, ncb, lc):
    bsz, lt, d = hs.shape
    rows = (lt - lc) // GRID_W
    assert rows >= NA_WIN_ROWS and rows % (NA_BLK_ROWS * NA_BLK_UNROLL) == 0
    npair = NA_HEADS // 2
    col_scale = jnp.concatenate([jnp.full((1, d), (NA_HEAD_DIM ** -0.5) * LOG2E, F32),
                                 jnp.ones((1, 2 * d), F32)], axis=-1)
    qkv = _norm_mod_matmul(hs, g, sh, sc, w_qkv.astype(BF16), col_scale, ncb=ncb, out_dtype=BF16, tn=1024)
    bias = _na_bias_table(rpb)
    o = pl.pallas_call(
        functools.partial(_na_kernel, lc=lc, rows=rows),
        grid=(npair, bsz),
        in_specs=[
            pl.BlockSpec((1, lt, LANES), lambda p, b: (b, 0, p)),
            pl.BlockSpec((1, lt, LANES), lambda p, b: (b, 0, npair + p)),
            pl.BlockSpec((1, lt, LANES), lambda p, b: (b, 0, 2 * npair + p)),
            pl.BlockSpec((3 * NA_BLK_ROWS, 1, 2 * GRID_W, NA_WIN_ROWS * GRID_W), lambda p, b: (0, p, 0, 0),
                         pipeline_mode=pl.Buffered(1)),
        ],
        out_specs=pl.BlockSpec((1, lt, LANES), lambda p, b: (b, 0, p)),
        out_shape=jax.ShapeDtypeStruct((bsz, lt, d), BF16),
        compiler_params=_cparams("parallel", "parallel"),
        name="natten",
    )(qkv, qkv, qkv, bias)
    return _matmul_gate_residual(o, w_out.astype(BF16), hs, gate, ncb=ncb)


def _layer_mods(mod_rows, bsz, d):
    mx = mod_rows[:bsz].reshape(bsz, 6, d)
    mc = jnp.broadcast_to(mod_rows[bsz].reshape(1, 6, d), (bsz, 6, d))
    m = jnp.stack([mc, mx], axis=1)
    return [m[:, :, k, None, :] for k in range(6)]


def kernel(x, c, ctx, c_ctx, ada_w, ada_b, norm1_g, norm2_g, mla_w_in, mla_q_norm_g, mla_w_q_up, mla_kv_norm_g, mla_w_kv_up, mla_w_out, s5_w_in, s5_lambda_re, s5_lambda_im, s5_log_dt, s5_b_re, s5_b_im, s5_c_re, s5_c_im, s5_d, s5_w_glu, hg_w_in, hg_lower_bound, hg_norm_g, hg_w_out, na_w_qkv, na_rpb, na_w_out, ffn_w_up, ffn_conv_w, ffn_conv_b, ffn_w_down, final_g):
    bsz, l, d = x.shape
    lc = ctx.shape[1]
    depth = ada_w.shape[0]
    assert lc % TM == 0 and l % TM == 0 and l % GRID_W == 0 and bsz + 1 <= 16
    ncb = lc // TM
    hs = jnp.concatenate([ctx, x], axis=1)
    cond_rows = jnp.zeros((16, d), F32).at[:bsz].set(c).at[bsz].set(c_ctx)
    mod_all = _ada_mod(cond_rows, ada_w, ada_b)
    lb_cum = jnp.cumsum(jax.nn.softmax(hg_lower_bound.astype(F32), axis=0), axis=0)
    lower_bounds = lb_cum - lb_cum[0]
    for i in range(depth):
        kind, j = i % 4, i // 4
        sh1, sc1, g1, sh2, sc2, g2 = _layer_mods(mod_all[i], bsz, d)
        if kind == 0:
            hs = _mla_layer(hs, norm1_g[i], sh1, sc1, g1, mla_w_in[j], mla_q_norm_g[j], mla_w_q_up[j],
                            mla_kv_norm_g[j], mla_w_kv_up[j], mla_w_out[j], ncb=ncb, lc=lc)
        elif kind == 1:
            hs = _s5_layer(hs, norm1_g[i], sh1, sc1, g1, s5_w_in[j], s5_lambda_re[j], s5_lambda_im[j],
                           s5_log_dt[j], s5_b_re[j], s5_b_im[j], s5_c_re[j], s5_c_im[j], s5_d[j], s5_w_glu[j],
                           ncb=ncb, lc=lc)
        elif kind == 2:
            hs = _hgrn2_layer(hs, norm1_g[i], sh1, sc1, g1, hg_w_in[j], lower_bounds[i], hg_norm_g[j],
                              hg_w_out[j], ncb=ncb, lc=lc)
        else:
            hs = _natten_layer(hs, norm1_g[i], sh1, sc1, g1, na_w_qkv[j], na_rpb[j], na_w_out[j],
                               ncb=ncb, lc=lc)
        hs = _conv_ffn(hs, norm2_g[i], sh2, sc2, g2, ffn_w_up[i], ffn_conv_w[i], ffn_conv_b[i],
                       ffn_w_down[i], final_g if i == depth - 1 else None, ncb=ncb)
    return hs
```

```python
import functools
import math

import jax
import jax.numpy as jnp
import numpy as np
from jax import lax
from jax.experimental import pallas as pl
from jax.experimental.pallas import tpu as pltpu

F32 = jnp.float32
BF16 = jnp.bfloat16
HI = lax.Precision.HIGHEST

EPS = 1e-6
GRID_W = 64
ROPE_THETA = 10000.0
LOG2E = math.log2(math.e)

LANES = 128
VMEM_LIMIT_BYTES = 56 * 1024 * 1024
TM = 256

MLA_HEADS = 16
MLA_Q_RANK = 384
MLA_KV_RANK = 256
MLA_NOPE = 64
MLA_ROPE = 32
MLA_V = 64
MLA_QK = MLA_NOPE + MLA_ROPE
MLA_HEADS_PER_STEP = 4

S5_GROUP_CH = 16
S5_STATE = 64
S5_T = 16
S5_QGRP = LANES // S5_GROUP_CH

HG_HEAD_DIM = 128
HG_CHUNK = 64
HG_HEADS_PER_STEP = 4

NA_HEADS = 16
NA_HEAD_DIM = 64
NA_KH = 8
NA_KW = 16
NEG_BIG = -1e30
NA_BLK_ROWS = 8
NA_WIN_ROWS = 16
NA_TABLE_PAIRS = 4 * NA_KH - 2
NA_BLK_UNROLL = 2


def _cparams(*sem):
    return pltpu.CompilerParams(dimension_semantics=sem, vmem_limit_bytes=VMEM_LIMIT_BYTES)


def _seg(ncb):
    return lambda i: jnp.where(i >= ncb, 1, 0)


def _norm_mod(x, g, sh, sc):
    ms = jnp.mean(x * x, axis=-1, keepdims=True)
    return (x * lax.rsqrt(ms + EPS) * g) * (1.0 + sc) + sh


def _sigmoid(x):
    return 1.0 / (1.0 + jnp.exp(-x))


def _dot(a, b):
    return jnp.dot(a, b, preferred_element_type=F32)


def _dot_nt(a, b):
    return lax.dot_general(a, b, (((1,), (1,)), ((), ())), preferred_element_type=F32)


def _dot_tn(a, b):
    return lax.dot_general(a, b, (((0,), (0,)), ((), ())), preferred_element_type=F32)


def _ada_kernel(cond_ref, w_ref, b_ref, o_ref):
    cond = cond_ref[...]
    a = (cond * _sigmoid(cond)).astype(BF16)
    o_ref[0] = _dot(a, w_ref[0].astype(BF16)) + b_ref[0]


def _ada_mod(cond_rows, ada_w, ada_b):
    depth, d, n = ada_w.shape
    rows = cond_rows.shape[0]
    tn = 1536
    return pl.pallas_call(
        _ada_kernel,
        grid=(depth, n // tn),
        in_specs=[
            pl.BlockSpec((rows, d), lambda i, j: (0, 0)),
            pl.BlockSpec((1, d, tn), lambda i, j: (i, 0, j)),
            pl.BlockSpec((1, 1, tn), lambda i, j: (i, 0, j)),
        ],
        out_specs=pl.BlockSpec((1, rows, tn), lambda i, j: (i, 0, j)),
        out_shape=jax.ShapeDtypeStruct((depth, rows, n), F32),
        compiler_params=_cparams("parallel", "parallel"),
        name="ada_mod",
    )(cond_rows, ada_w, ada_b.reshape(depth, 1, n))


def _nmm_kernel(x_ref, g_ref, sh_ref, sc_ref, w_ref, cs_ref, *o_refs, tn):
    u = _norm_mod(x_ref[0], g_ref[...], sh_ref[0, 0], sc_ref[0, 0]).astype(BF16)
    n = w_ref.shape[1]
    for j in range(n // tn):
        sl = slice(j * tn, (j + 1) * tn)
        r = _dot(u, w_ref[:, sl]) * cs_ref[:, sl]
        for o_ref in o_refs:
            o_ref[0, :, sl] = r.astype(o_ref.dtype)


def _norm_mod_matmul(hs, g, sh, sc, w, col_scale, *, ncb, out_dtype, tn):
    bsz, lt, d = hs.shape
    n = w.shape[1]
    seg = _seg(ncb)
    multi = isinstance(out_dtype, tuple)
    dtypes = out_dtype if multi else (out_dtype,)
    outs = pl.pallas_call(
        functools.partial(_nmm_kernel, tn=tn),
        grid=(bsz, lt // TM),
        in_specs=[
            pl.BlockSpec((1, TM, d), lambda b, i: (b, i, 0)),
            pl.BlockSpec((1, d), lambda b, i: (0, 0)),
            pl.BlockSpec((1, 1, 1, d), lambda b, i: (b, seg(i), 0, 0)),
            pl.BlockSpec((1, 1, 1, d), lambda b, i: (b, seg(i), 0, 0)),
            pl.BlockSpec((d, n), lambda b, i: (0, 0)),
            pl.BlockSpec((1, n), lambda b, i: (0, 0)),
        ],
        out_specs=[pl.BlockSpec((1, TM, n), lambda b, i: (b, i, 0)) for _ in dtypes],
        out_shape=[jax.ShapeDtypeStruct((bsz, lt, n), dt) for dt in dtypes],
        compiler_params=_cparams("parallel", "parallel"),
        name="norm_mod_matmul",
    )(hs, g.reshape(1, d), sh, sc, w, col_scale)
    return tuple(outs) if multi else outs[0]


def _mgr_kernel(a_ref, w_ref, x_ref, gate_ref, o_ref):
    o_ref[0] = x_ref[0] + gate_ref[0, 0] * _dot(a_ref[0], w_ref[...])


def _matmul_gate_residual(a, w, hs, gate, *, ncb):
    bsz, lt, d = hs.shape
    k = a.shape[2]
    seg = _seg(ncb)
    return pl.pallas_call(
        _mgr_kernel,
        grid=(bsz, lt // TM),
        in_specs=[
            pl.BlockSpec((1, TM, k), lambda b, i: (b, i, 0)),
            pl.BlockSpec((k, d), lambda b, i: (0, 0)),
            pl.BlockSpec((1, TM, d), lambda b, i: (b, i, 0)),
            pl.BlockSpec((1, 1, 1, d), lambda b, i: (b, seg(i), 0, 0)),
        ],
        out_specs=pl.BlockSpec((1, TM, d), lambda b, i: (b, i, 0)),
        out_shape=jax.ShapeDtypeStruct((bsz, lt, d), F32),
        compiler_params=_cparams("parallel", "parallel"),
        name="matmul_gate_residual",
    )(a, w, hs, gate)


FFN_HALO = 16
FFN_PAD = 8


def _ffn_kernel(x_ref, xp_ref, xn_ref, g_ref, sh_ref, sc_ref, gate_ref, wup_ref, cw_ref, cb_ref,
                wdn_ref, fg_ref, o_ref, u_scr, h_scr, *, ncb, nb, off, final):
    i = pl.program_id(1) + off
    g = g_ref[...]
    sh = sh_ref[0, 0]
    sc = sc_ref[0, 0]
    f = wdn_ref.shape[0]
    keep_prev = jnp.where((i == 0) | (i == ncb), 0.0, 1.0)
    keep_next = jnp.where((i == ncb - 1) | (i == nb - 1), 0.0, 1.0)
    u_prev = _norm_mod(xp_ref[0], g, sh, sc)[FFN_HALO - 1:FFN_HALO] * keep_prev
    u_next = _norm_mod(xn_ref[0], g, sh, sc)[0:1] * keep_next
    row = lax.broadcasted_iota(jnp.int32, (FFN_HALO, u_prev.shape[1]), 0)
    extra = jnp.where(row == 0, u_prev, jnp.where(row == 1, u_next, 0.0))
    u_scr[0:TM, :] = _norm_mod(x_ref[0], g, sh, sc).astype(BF16)
    u_scr[TM:, :] = extra.astype(BF16)
    h = _dot(u_scr[...], wup_ref[...])
    h_scr[FFN_PAD:FFN_PAD + TM, :] = h[0:TM]
    h_scr[FFN_PAD - 1:FFN_PAD, :] = h[TM:TM + 1]
    h_scr[FFN_PAD + TM:FFN_PAD + TM + 1, :] = h[TM + 1:TM + 2]
    w = cw_ref[...]
    hc = (h_scr[FFN_PAD - 1:FFN_PAD - 1 + TM, :] * w[0:1] + h_scr[FFN_PAD:FFN_PAD + TM, :] * w[1:2]
          + h_scr[FFN_PAD + 1:FFN_PAD + 1 + TM, :] * w[2:3] + cb_ref[...])
    gg = hc[:, f:]
    act = (hc[:, :f] * (gg * _sigmoid(gg))).astype(BF16)
    y = x_ref[0] + gate_ref[0, 0] * _dot(act, wdn_ref[...])
    if final:
        y = y * lax.rsqrt(jnp.mean(y * y, axis=-1, keepdims=True) + EPS) * fg_ref[...]
    o_ref[0] = y


def _conv_ffn(hs, g, sh, sc, gate, w_up, conv_w, conv_b, w_down, final_g=None, *, ncb):
    bsz, lt, d = hs.shape
    f = w_down.shape[0]
    nb = lt // TM
    final = final_g is not None
    off = ncb if final else 0
    hb = TM // FFN_HALO
    nhalo = lt // FFN_HALO
    seg = _seg(ncb)
    fg = (final_g if final else jnp.ones((d,), F32)).reshape(1, d)
    return pl.pallas_call(
        functools.partial(_ffn_kernel, ncb=ncb, nb=nb, off=off, final=final),
        grid=(bsz, nb - off),
        in_specs=[
            pl.BlockSpec((1, TM, d), lambda b, i: (b, i + off, 0)),
            pl.BlockSpec((1, FFN_HALO, d), lambda b, i: (b, jnp.maximum((i + off) * hb - 1, 0), 0)),
            pl.BlockSpec((1, FFN_HALO, d), lambda b, i: (b, jnp.minimum((i + off + 1) * hb, nhalo - 1), 0)),
            pl.BlockSpec((1, d), lambda b, i: (0, 0)),
            pl.BlockSpec((1, 1, 1, d), lambda b, i: (b, seg(i + off), 0, 0)),
            pl.BlockSpec((1, 1, 1, d), lambda b, i: (b, seg(i + off), 0, 0)),
            pl.BlockSpec((1, 1, 1, d), lambda b, i: (b, seg(i + off), 0, 0)),
            pl.BlockSpec((d, 2 * f), lambda b, i: (0, 0)),
            pl.BlockSpec((3, 2 * f), lambda b, i: (0, 0)),
            pl.BlockSpec((1, 2 * f), lambda b, i: (0, 0)),
            pl.BlockSpec((f, d), lambda b, i: (0, 0)),
            pl.BlockSpec((1, d), lambda b, i: (0, 0)),
        ],
        out_specs=pl.BlockSpec((1, TM, d), lambda b, i: (b, i, 0)),
        out_shape=jax.ShapeDtypeStruct((bsz, lt - off * TM, d), F32),
        scratch_shapes=[
            pltpu.VMEM((TM + FFN_HALO, d), BF16),
            pltpu.VMEM((TM + 2 * FFN_PAD, 2 * f), F32),
        ],
        compiler_params=_cparams("parallel", "parallel"),
        name="conv_ffn",
    )(hs, hs, hs, g.reshape(1, d), sh, sc, gate, w_up.astype(BF16), conv_w, conv_b.reshape(1, 2 * f),
      w_down.astype(BF16), fg)


def _rope_tables(lc, l):
    half = MLA_ROPE // 4
    inv = 1.0 / (ROPE_THETA ** (jnp.arange(half, dtype=F32) / half))
    t = jnp.arange(l)
    ang_r = (t // GRID_W).astype(F32)[:, None] * inv[None, :]
    ang_c = (t % GRID_W).astype(F32)[:, None] * inv[None, :]
    cos = jnp.concatenate([jnp.cos(ang_r), jnp.cos(ang_r), jnp.cos(ang_c), jnp.cos(ang_c)], axis=-1)
    sin = jnp.concatenate([-jnp.sin(ang_r), jnp.sin(ang_r), -jnp.sin(ang_c), jnp.sin(ang_c)], axis=-1)
    cos_t = jnp.ones((lc + l, LANES), F32).at[lc:, MLA_NOPE:MLA_NOPE + MLA_ROPE].set(cos)
    sin_t = jnp.zeros((lc + l, LANES), F32).at[lc:, MLA_NOPE:MLA_NOPE + MLA_ROPE].set(sin)
    return cos_t, sin_t


def _mla_proj_kernel(x_ref, g_ref, sh_ref, sc_ref, win_ref, qg_ref, wq_ref, kvg_ref, wkn_ref, wv_ref,
                     cos_ref, sin_ref, q_ref, k_ref, v_ref):
    u = _norm_mod(x_ref[0], g_ref[...], sh_ref[0, 0], sc_ref[0, 0]).astype(BF16)
    lat = _dot(u, win_ref[...])
    q_lat = lat[:, :MLA_Q_RANK]
    kv_lat = lat[:, MLA_Q_RANK:MLA_Q_RANK + MLA_KV_RANK]
    kr = lat[:, MLA_Q_RANK + MLA_KV_RANK:]

    def rms(t, gg):
        return (t * lax.rsqrt(jnp.mean(t * t, axis=-1, keepdims=True) + EPS) * gg).astype(BF16)

    qn = rms(q_lat, qg_ref[...])
    kvn = rms(kv_lat, kvg_ref[...])
    cos = cos_ref[...]
    sin = sin_ref[...]
    lane = lax.broadcasted_iota(jnp.int32, (TM, LANES), 1)
    first_half = (lane % (MLA_ROPE // 2)) < (MLA_ROPE // 4)

    def rope(t):
        partner = jnp.where(first_half, pltpu.roll(t, LANES - MLA_ROPE // 4, 1), pltpu.roll(t, MLA_ROPE // 4, 1))
        return t * cos + partner * sin

    kr_rot = rope(kr)
    v_ref[0] = _dot(kvn, wv_ref[...]).astype(BF16)
    q_all = _dot(qn, wq_ref[...]) * ((MLA_QK ** -0.5) * LOG2E)
    kn_all = _dot(kvn, wkn_ref[...])
    for h in range(MLA_HEADS):
        sl = slice(h * LANES, (h + 1) * LANES)
        q_ref[0, :, sl] = rope(q_all[:, sl]).astype(BF16)
        k_ref[0, :, sl] = (kn_all[:, sl] + kr_rot).astype(BF16)


def _mla_attn_kernel(q_ref, k_ref, v_ref, o_ref, *, ncb, lc):
    i = pl.program_id(2)

    def attend(lk):
        q = q_ref[0]
        lane = lax.broadcasted_iota(jnp.int32, (q.shape[0], LANES), 1)
        for pair in range(q.shape[1] // (2 * LANES)):
            v = v_ref[0, 0:lk, pair * LANES:(pair + 1) * LANES]
            outs = []
            for e in range(2):
                sl = slice((2 * pair + e) * LANES, (2 * pair + e + 1) * LANES)
                s = _dot_nt(q[:, sl], k_ref[0, 0:lk, sl])
                m = jnp.max(s, axis=-1, keepdims=True)
                p = jnp.exp2(s - m)
                l = jnp.sum(p, axis=-1, keepdims=True)
                outs.append(_dot(p.astype(BF16), v) / l)
            o_ref[0, :, pair * LANES:(pair + 1) * LANES] = jnp.where(lane < MLA_V, outs[0], outs[1]).astype(o_ref.dtype)

    @pl.when(i < ncb)
    def _():
        attend(lc)

    @pl.when(i >= ncb)
    def _():
        attend(k_ref.shape[1])


def _mla_layer(hs, g, sh, sc, gate, w_in, q_norm_g, w_q_up, kv_norm_g, w_kv_up, w_out, *, ncb, lc):
    bsz, lt, d = hs.shape
    l = lt - lc
    nh = MLA_HEADS
    seg = _seg(ncb)
    w_in_p = jnp.zeros((d, 768), F32)
    w_in_p = w_in_p.at[:, :MLA_Q_RANK + MLA_KV_RANK].set(w_in[:, :MLA_Q_RANK + MLA_KV_RANK])
    w_in_p = w_in_p.at[:, 640 + MLA_NOPE:640 + MLA_QK].set(w_in[:, MLA_Q_RANK + MLA_KV_RANK:])
    wq = w_q_up.reshape(MLA_Q_RANK, nh, MLA_QK)
    wq = jnp.pad(wq, ((0, 0), (0, 0), (0, LANES - MLA_QK))).reshape(MLA_Q_RANK, nh * LANES)
    wkv = w_kv_up.reshape(MLA_KV_RANK, nh, MLA_NOPE + MLA_V)
    wkn = jnp.pad(wkv[:, :, :MLA_NOPE], ((0, 0), (0, 0), (0, LANES - MLA_NOPE))).reshape(MLA_KV_RANK, nh * LANES)
    wv = wkv[:, :, MLA_NOPE:].reshape(MLA_KV_RANK, nh * MLA_V)
    cos_t, sin_t = _rope_tables(lc, l)

    q, k, v = pl.pallas_call(
        _mla_proj_kernel,
        grid=(bsz, lt // TM),
        in_specs=[
            pl.BlockSpec((1, TM, d), lambda b, i: (b, i, 0)),
            pl.BlockSpec((1, d), lambda b, i: (0, 0)),
            pl.BlockSpec((1, 1, 1, d), lambda b, i: (b, seg(i), 0, 0)),
            pl.BlockSpec((1, 1, 1, d), lambda b, i: (b, seg(i), 0, 0)),
            pl.BlockSpec((d, 768), lambda b, i: (0, 0)),
            pl.BlockSpec((1, MLA_Q_RANK), lambda b, i: (0, 0)),
            pl.BlockSpec((MLA_Q_RANK, nh * LANES), lambda b, i: (0, 0)),
            pl.BlockSpec((1, MLA_KV_RANK), lambda b, i: (0, 0)),
            pl.BlockSpec((MLA_KV_RANK, nh * LANES), lambda b, i: (0, 0)),
            pl.BlockSpec((MLA_KV_RANK, nh * MLA_V), lambda b, i: (0, 0)),
            pl.BlockSpec((TM, LANES), lambda b, i: (i, 0)),
            pl.BlockSpec((TM, LANES), lambda b, i: (i, 0)),
        ],
        out_specs=[
            pl.BlockSpec((1, TM, nh * LANES), lambda b, i: (b, i, 0)),
            pl.BlockSpec((1, TM, nh * LANES), lambda b, i: (b, i, 0)),
            pl.BlockSpec((1, TM, nh * MLA_V), lambda b, i: (b, i, 0)),
        ],
        out_shape=[
            jax.ShapeDtypeStruct((bsz, lt, nh * LANES), BF16),
            jax.ShapeDtypeStruct((bsz, lt, nh * LANES), BF16),
            jax.ShapeDtypeStruct((bsz, lt, nh * MLA_V), BF16),
        ],
        compiler_params=_cparams("parallel", "parallel"),
        name="mla_proj",
    )(hs, g.reshape(1, d), sh, sc, w_in_p.astype(BF16), q_norm_g.reshape(1, -1), wq.astype(BF16),
      kv_norm_g.reshape(1, -1), wkn.astype(BF16), wv.astype(BF16), cos_t, sin_t)

    o = pl.pallas_call(
        functools.partial(_mla_attn_kernel, ncb=ncb, lc=lc),
        grid=(bsz, nh // MLA_HEADS_PER_STEP, lt // TM),
        in_specs=[
            pl.BlockSpec((1, TM, MLA_HEADS_PER_STEP * LANES), lambda b, p, i: (b, i, p)),
            pl.BlockSpec((1, lt, MLA_HEADS_PER_STEP * LANES), lambda b, p, i: (b, 0, p)),
            pl.BlockSpec((1, lt, MLA_HEADS_PER_STEP * MLA_V), lambda b, p, i: (b, 0, p)),
        ],
        out_specs=pl.BlockSpec((1, TM, MLA_HEADS_PER_STEP * MLA_V), lambda b, p, i: (b, i, p)),
        out_shape=jax.ShapeDtypeStruct((bsz, lt, nh * MLA_V), BF16),
        compiler_params=_cparams("parallel", "parallel", "parallel"),
        name="mla_attn",
    )(q, k, v)
    return _matmul_gate_residual(o, w_out.astype(BF16), hs, gate, ncb=ncb)


def _s5_matrices(lam_re, lam_im, log_dt, b_re, b_im, c_re, c_im):
    t_len = S5_T
    n_grp = lam_re.shape[1]
    ch = S5_GROUP_CH
    n_st = S5_STATE
    dt = jnp.exp(log_dt.astype(F32))[..., None]
    ld_re = lam_re.astype(F32) * dt
    ld_im = lam_im.astype(F32) * dt
    tau = jnp.arange(t_len + 1, dtype=F32)[:, None, None, None]
    mag = jnp.exp(tau * ld_re[None])
    pw_re = mag * jnp.cos(tau * ld_im[None])
    pw_im = mag * jnp.sin(tau * ld_im[None])
    tau_r = jnp.arange(t_len, -1, -1, dtype=F32)[:, None, None, None]
    mag_r = jnp.exp(tau_r * ld_re[None])
    pr_re = mag_r * jnp.cos(tau_r * ld_im[None])
    pr_im = mag_r * jnp.sin(tau_r * ld_im[None])
    lb_re, lb_im = pw_re[1] - 1.0, pw_im[1]
    den = lam_re * lam_re + lam_im * lam_im
    f_re = (lb_re * lam_re + lb_im * lam_im) / den
    f_im = (lb_im * lam_re - lb_re * lam_im) / den
    bb_re = f_re[..., None] * b_re - f_im[..., None] * b_im
    bb_im = f_re[..., None] * b_im + f_im[..., None] * b_re
    cp_re = c_re[None] * pw_re[:t_len, :, :, None, :] - c_im[None] * pw_im[:t_len, :, :, None, :]
    cp_im = c_re[None] * pw_im[:t_len, :, :, None, :] + c_im[None] * pw_re[:t_len, :, :, None, :]
    taps = (jnp.einsum("tdgon,dgni->tdgoi", cp_re, bb_re, precision=HI)
            - jnp.einsum("tdgon,dgni->tdgoi", cp_im, bb_im, precision=HI))
    s_idx = np.arange(t_len)[:, None, None]
    t_idx = np.arange(t_len)[None, :, None]
    u_idx = np.arange(t_len)[None, None, :]
    sel_f = jnp.asarray((t_idx - s_idx == u_idx).astype(np.float32))
    sel_b = jnp.asarray((s_idx - t_idx == u_idx).astype(np.float32))
    kf = jnp.einsum("stu,ugoi->stgoi", sel_f, taps[:, 0], precision=HI)
    kb = jnp.einsum("stu,ugoi->stgoi", sel_b, taps[:, 1], precision=HI)
    k_tot = (kf + kb).transpose(2, 0, 4, 1, 3).reshape(n_grp, t_len * ch, t_len * ch)

    def state_in(pw_r, pw_i, brr, bii):
        re = pw_r[..., None] * brr[None] - pw_i[..., None] * bii[None]
        im = pw_r[..., None] * bii[None] + pw_i[..., None] * brr[None]
        return (re.transpose(1, 0, 3, 2).reshape(n_grp, t_len * ch, n_st),
                im.transpose(1, 0, 3, 2).reshape(n_grp, t_len * ch, n_st))

    inf_re, inf_im = state_in(pr_re[1:, 0], pr_im[1:, 0], bb_re[0], bb_im[0])
    inb_re, inb_im = state_in(pw_re[:t_len, 1], pw_im[:t_len, 1], bb_re[1], bb_im[1])

    def state_out(pw_r, pw_i, crr, cii):
        re = crr[None] * pw_r[:, :, None, :] - cii[None] * pw_i[:, :, None, :]
        im = crr[None] * pw_i[:, :, None, :] + cii[None] * pw_r[:, :, None, :]
        return (re.transpose(1, 3, 0, 2).reshape(n_grp, n_st, t_len * ch),
                (-im).transpose(1, 3, 0, 2).reshape(n_grp, n_st, t_len * ch))

    outf_re, outf_im = state_out(pw_re[1:, 0], pw_im[1:, 0], c_re[0], c_im[0])
    outb_re, outb_im = state_out(pr_re[:t_len, 1], pr_im[:t_len, 1], c_re[1], c_im[1])
    nq = n_grp // S5_QGRP
    eye = jnp.eye(S5_QGRP, dtype=BF16)
    kt = k_tot.astype(BF16).reshape(nq, S5_QGRP, t_len, ch, t_len, ch)
    m_chunk = jnp.einsum("qgtcuo,gh->qtgcuho", kt, eye).reshape(nq, S5_QGRP * t_len * ch, S5_QGRP * t_len * ch)
    w_in = jnp.stack([inf_re, inf_im, inb_re, inb_im], 0).astype(BF16)
    w_in = w_in.reshape(4, nq, S5_QGRP, t_len, ch, n_st)
    w_in = jnp.einsum("kqgtcn,gh->qtgckhn", w_in, eye).reshape(nq, S5_QGRP * t_len * ch, 4 * S5_QGRP * n_st)
    m_out = jnp.stack([outf_re, outf_im, outb_re, outb_im], 0).astype(BF16)
    m_out = m_out.reshape(4, nq, S5_QGRP, n_st, t_len, ch)
    m_out = jnp.einsum("kqgnuo,gh->qkgnuho", m_out, eye).reshape(nq, 4, S5_QGRP * n_st, S5_QGRP * t_len * ch)
    decay = jnp.stack([pw_re[t_len, 0], pw_im[t_len, 0], pw_re[t_len, 1], pw_im[t_len, 1]], 0)
    decay = decay.reshape(4, nq, S5_QGRP * n_st).transpose(1, 0, 2).reshape(nq, 1, 4 * S5_QGRP * n_st)
    return w_in, m_chunk, m_out, decay


def _s5_proj_kernel(x_ref, g_ref, sh_ref, sc_ref, w_ref, z_ref, zq_ref):
    u = _norm_mod(x_ref[0], g_ref[...], sh_ref[0, 0], sc_ref[0, 0]).astype(BF16)
    z = _dot(u, w_ref[...])
    z_ref[0] = z
    for q in range(zq_ref.shape[1]):
        zq_ref[0, q] = z[:, q * LANES:(q + 1) * LANES].astype(zq_ref.dtype)


def _s5_in_kernel(z_ref, w_ref, u_ref):
    u_ref[0] = _dot(z_ref[0, 0], w_ref[0])


def _s5_scan_kernel(ufr_ref, ufi_ref, ubr_ref, ubi_ref, afr_ref, afi_ref, abr_ref, abi_ref,
                    pfr_ref, pfi_ref, pbr_ref, pbi_ref, *, jc, jt):
    afr, afi, abr, abi = afr_ref[0], afi_ref[0], abr_ref[0], abi_ref[0]
    zero = jnp.zeros(pfr_ref.shape[2:], F32)

    def body(jj, carry):
        fr, fi, br, bi = carry
        jb = jnp.where(jj < jc, jc - 1 - jj, jt - 1 - jj + jc)
        pfr_ref[0, jj] = fr
        pfi_ref[0, jj] = fi
        pbr_ref[0, jb] = br
        pbi_ref[0, jb] = bi
        nfr = afr * fr - afi * fi + ufr_ref[0, jj]
        nfi = afr * fi + afi * fr + ufi_ref[0, jj]
        nbr = abr * br - abi * bi + ubr_ref[0, jb]
        nbi = abr * bi + abi * br + ubi_ref[0, jb]
        return nfr, nfi, nbr, nbi

    lax.fori_loop(0, jt, body, (zero, zero, zero, zero))


def _s5_out_kernel(z_ref, pfr_ref, pfi_ref, pbr_ref, pbi_ref, k_ref, m_ref, y_ref):
    acc = _dot(z_ref[0, 0], k_ref[0])
    for k, p_ref in enumerate((pfr_ref, pfi_ref, pbr_ref, pbi_ref)):
        acc += _dot(p_ref[0].astype(BF16), m_ref[0, k])
    y_ref[0, 0] = acc.astype(y_ref.dtype)


def _s5_glu_kernel(y_ref, z_ref, d_ref, w_ref, x_ref, gate_ref, o_ref):
    d = x_ref.shape[2]
    y_mm = jnp.concatenate([y_ref[0, q] for q in range(y_ref.shape[1])], axis=1).astype(F32)
    y = y_mm + z_ref[0] * d_ref[...]
    ge = jax.nn.gelu(y).astype(BF16)
    ag = _dot(ge, w_ref[...])
    o_ref[0] = x_ref[0] + gate_ref[0, 0] * (ag[:, :d] * _sigmoid(ag[:, d:]))


def _s5_layer(hs, g, sh, sc, gate, w_in, lam_re, lam_im, log_dt, b_re, b_im, c_re, c_im, d_skip, w_glu,
              *, ncb, lc):
    bsz, lt, d = hs.shape
    width = w_in.shape[1]
    nq = width // LANES
    jt = lt // S5_T
    jc = lc // S5_T
    ccols = S5_T * LANES
    scols = S5_QGRP * S5_STATE
    nst = scols // LANES
    seg = _seg(ncb)
    z, zq = pl.pallas_call(
        _s5_proj_kernel,
        grid=(bsz, lt // TM),
        in_specs=[
            pl.BlockSpec((1, TM, d), lambda b, i: (b, i, 0)),
            pl.BlockSpec((1, d), lambda b, i: (0, 0)),
            pl.BlockSpec((1, 1, 1, d), lambda b, i: (b, seg(i), 0, 0)),
            pl.BlockSpec((1, 1, 1, d), lambda b, i: (b, seg(i), 0, 0)),
            pl.BlockSpec((d, width), lambda b, i: (0, 0)),
        ],
        out_specs=[
            pl.BlockSpec((1, TM, width), lambda b, i: (b, i, 0)),
            pl.BlockSpec((1, nq, TM, LANES), lambda b, i: (b, 0, i, 0)),
        ],
        out_shape=[
            jax.ShapeDtypeStruct((bsz, lt, width), F32),
            jax.ShapeDtypeStruct((bsz, nq, lt, LANES), BF16),
        ],
        compiler_params=_cparams("parallel", "parallel"),
        name="s5_proj",
    )(hs, g.reshape(1, d), sh, sc, w_in.astype(BF16))
    zq = zq.reshape(bsz, nq, jt, ccols)
    w_si, m_chunk, m_out, decay = _s5_matrices(lam_re, lam_im, log_dt, b_re, b_im, c_re, c_im)
    u = pl.pallas_call(
        _s5_in_kernel,
        grid=(nq, bsz),
        in_specs=[
            pl.BlockSpec((1, 1, jt, ccols), lambda q, b: (b, q, 0, 0)),
            pl.BlockSpec((1, ccols, 4 * scols), lambda q, b: (q, 0, 0)),
        ],
        out_specs=pl.BlockSpec((1, jt, 4 * scols), lambda q, b: (q, 0, b)),
        out_shape=jax.ShapeDtypeStruct((nq, jt, bsz * 4 * scols), F32),
        compiler_params=_cparams("parallel", "parallel"),
        name="s5_state_in",
    )(zq, w_si)
    u = u.reshape(nq, jt, bsz, 4 * scols)

    def kind_spec(k):
        return pl.BlockSpec((1, jt, bsz, LANES), lambda q, t: (q, 0, 0, k * nst + t))

    def decay_spec(k):
        return pl.BlockSpec((1, 1, LANES), lambda q, t: (q, 0, k * nst + t))

    p_state = pl.pallas_call(
        functools.partial(_s5_scan_kernel, jc=jc, jt=jt),
        grid=(nq, nst),
        in_specs=[kind_spec(k) for k in range(4)] + [decay_spec(k) for k in range(4)],
        out_specs=[pl.BlockSpec((1, jt, bsz, LANES), lambda q, t: (q, 0, 0, t)) for _ in range(4)],
        out_shape=[jax.ShapeDtypeStruct((nq, jt, bsz, scols), F32) for _ in range(4)],
        compiler_params=_cparams("parallel", "parallel"),
        name="s5_scan",
    )(u, u, u, u, decay, decay, decay, decay)
    p_state = [p.reshape(nq, jt, bsz * scols) for p in p_state]
    yq = pl.pallas_call(
        _s5_out_kernel,
        grid=(nq, bsz),
        in_specs=[pl.BlockSpec((1, 1, jt, ccols), lambda q, b: (b, q, 0, 0))]
        + [pl.BlockSpec((1, jt, scols), lambda q, b: (q, 0, b)) for _ in range(4)]
        + [pl.BlockSpec((1, ccols, ccols), lambda q, b: (q, 0, 0)),
           pl.BlockSpec((1, 4, scols, ccols), lambda q, b: (q, 0, 0, 0))],
        out_specs=pl.BlockSpec((1, 1, jt, ccols), lambda q, b: (b, q, 0, 0)),
        out_shape=jax.ShapeDtypeStruct((bsz, nq, jt, ccols), BF16),
        compiler_params=_cparams("parallel", "parallel"),
        name="s5_chunk_out",
    )(zq, *p_state, m_chunk, m_out)
    y = yq.reshape(bsz, nq, lt, LANES)
    return pl.pallas_call(
        _s5_glu_kernel,
        grid=(bsz, lt // TM),
        in_specs=[
            pl.BlockSpec((1, nq, TM, LANES), lambda b, i: (b, 0, i, 0)),
            pl.BlockSpec((1, TM, width), lambda b, i: (b, i, 0)),
            pl.BlockSpec((1, width), lambda b, i: (0, 0)),
            pl.BlockSpec((width, 2 * d), lambda b, i: (0, 0)),
            pl.BlockSpec((1, TM, d), lambda b, i: (b, i, 0)),
            pl.BlockSpec((1, 1, 1, d), lambda b, i: (b, seg(i), 0, 0)),
        ],
        out_specs=pl.BlockSpec((1, TM, d), lambda b, i: (b, i, 0)),
        out_shape=jax.ShapeDtypeStruct((bsz, lt, d), F32),
        compiler_params=_cparams("parallel", "parallel"),
        name="s5_glu",
    )(y, z, d_skip.reshape(1, width), w_glu.astype(BF16), hs, gate)


def _gla_dir(zq, zf, zv, lb, states, incl, tri, rev):
    dk = HG_HEAD_DIM
    width = zq.shape[1]
    nchunks = zq.shape[0] // HG_CHUNK
    forget = lb + (1.0 - lb) * _sigmoid(zf)
    lf = jnp.log(forget)
    kk = 1.0 - forget
    v_all = zv.astype(BF16)
    hi = lf.astype(BF16)
    r1 = lf - hi.astype(F32)
    mid = r1.astype(BF16)
    lo = (r1 - mid.astype(F32)).astype(BF16)
    parts = _dot(tri, jnp.concatenate([hi, mid, lo], axis=1))
    bcum = parts[:, :width] + parts[:, width:2 * width] + parts[:, 2 * width:]
    btot = jnp.concatenate(
        [jnp.broadcast_to(bcum[c * HG_CHUNK:c * HG_CHUNK + 1] if rev else bcum[(c + 1) * HG_CHUNK - 1:(c + 1) * HG_CHUNK],
                          (HG_CHUNK, width)) for c in range(nchunks)], axis=0)
    q_in_all = (zq * _sigmoid(zq)) * (dk ** -0.5) * jnp.exp(bcum)
    k_in_all = (kk * jnp.exp(-bcum)).astype(BF16)
    k_out_all = kk * jnp.exp(btot - bcum)
    chunk_of_row = lax.broadcasted_iota(jnp.int32, (zq.shape[0], dk), 0) // HG_CHUNK
    zero = jnp.zeros((zq.shape[0], dk), F32)
    outs, new_states = [], []
    for h, st in enumerate(states):
        sl = slice(h * dk, (h + 1) * dk)
        q_in, k_out, v = q_in_all[:, sl], k_out_all[:, sl], v_all[:, sl]
        att = jnp.where(incl, _dot_nt(q_in.astype(BF16), k_in_all[:, sl]), 0.0)
        o_intra = _dot(att.astype(BF16), v)
        k_cat = jnp.concatenate([jnp.where(chunk_of_row == c, k_out, zero) for c in range(nchunks)], axis=1)
        q_cat = jnp.concatenate([jnp.where(chunk_of_row == c, q_in, zero) for c in range(nchunks)], axis=1)
        ds = _dot_tn(v, k_cat.astype(BF16))
        entering = [None] * nchunks
        for cc in range(nchunks):
            c = nchunks - 1 - cc if rev else cc
            entering[c] = st
            st = st * jnp.exp(btot[c * HG_CHUNK:c * HG_CHUNK + 1, sl]) + ds[:, c * dk:(c + 1) * dk]
        scat = jnp.concatenate(entering, axis=1).astype(BF16)
        outs.append(o_intra + _dot_nt(q_cat.astype(BF16), scat))
        new_states.append(st)
    return outs, new_states


def _gla_kernel(zqf_ref, zff_ref, zvf_ref, zqb_ref, zfb_ref, zvb_ref, lb_ref, of_ref, ob_ref, st_scr):
    @pl.when(pl.program_id(2) == 0)
    def _():
        st_scr[...] = jnp.zeros_like(st_scr)

    n = zqf_ref.shape[1]
    row = lax.broadcasted_iota(jnp.int32, (n, n), 0)
    col = lax.broadcasted_iota(jnp.int32, (n, n), 1)
    same = (row // HG_CHUNK) == (col // HG_CHUNK)
    dk = HG_HEAD_DIM
    for direction, (zq_ref, zf_ref, zv_ref, o_ref) in enumerate(
            ((zqf_ref, zff_ref, zvf_ref, of_ref), (zqb_ref, zfb_ref, zvb_ref, ob_ref))):
        rev = direction == 1
        incl = same & ((col >= row) if rev else (col <= row))
        tri = incl.astype(BF16)
        nheads = zq_ref.shape[2] // dk
        outs, states = _gla_dir(zq_ref[0], zf_ref[0], zv_ref[0], lb_ref[0],
                                [st_scr[direction, h] for h in range(nheads)], incl, tri, rev)
        for h in range(nheads):
            o_ref[0, :, h * dk:(h + 1) * dk] = outs[h]
            st_scr[direction, h] = states[h]


def _hg_out_kernel(of_ref, ob_ref, zg_ref, ng_ref, w_ref, x_ref, gate_ref, o_ref):
    o = of_ref[0] + ob_ref[0]
    gsig = zg_ref[0]
    gsig = gsig * _sigmoid(gsig)
    ng = ng_ref[...]
    parts = []
    for h in range(o.shape[1] // HG_HEAD_DIM):
        sl = slice(h * HG_HEAD_DIM, (h + 1) * HG_HEAD_DIM)
        oh = o[:, sl]
        on = oh * lax.rsqrt(jnp.mean(oh * oh, axis=-1, keepdims=True) + EPS)
        parts.append((on * ng[:, sl] * gsig[:, sl]).astype(BF16))
    a = jnp.concatenate(parts, axis=-1)
    o_ref[0] = x_ref[0] + gate_ref[0, 0] * _dot(a, w_ref[...])


def _hgrn2_layer(hs, g, sh, sc, gate, w_in, lower_bound, norm_g, w_out, *, ncb, lc):
    bsz, lt, d = hs.shape
    nh = d // HG_HEAD_DIM
    nb = lt // TM
    seg = _seg(ncb)
    z = _norm_mod_matmul(hs, g, sh, sc, w_in.astype(BF16), jnp.ones((1, 5 * d), F32),
                         ncb=ncb, out_dtype=F32, tn=1024)
    lb = lower_bound.astype(F32).reshape(1, 1, d)
    hps = HG_HEADS_PER_STEP
    ng = nh // hps
    wb = hps * HG_HEAD_DIM

    def rblk(s):
        return jnp.where(s < ncb, ncb - 1 - s, nb - 1 - s + ncb)

    outs = pl.pallas_call(
        _gla_kernel,
        grid=(bsz, ng, nb),
        in_specs=[
            pl.BlockSpec((1, TM, wb), lambda b, h, s: (b, s, h)),
            pl.BlockSpec((1, TM, wb), lambda b, h, s: (b, s, ng + h)),
            pl.BlockSpec((1, TM, wb), lambda b, h, s: (b, s, 3 * ng + h)),
            pl.BlockSpec((1, TM, wb), lambda b, h, s: (b, rblk(s), h)),
            pl.BlockSpec((1, TM, wb), lambda b, h, s: (b, rblk(s), 2 * ng + h)),
            pl.BlockSpec((1, TM, wb), lambda b, h, s: (b, rblk(s), 3 * ng + h)),
            pl.BlockSpec((1, 1, wb), lambda b, h, s: (0, 0, h)),
        ],
        out_specs=[
            pl.BlockSpec((1, TM, wb), lambda b, h, s: (b, s, h)),
            pl.BlockSpec((1, TM, wb), lambda b, h, s: (b, rblk(s), h)),
        ],
        out_shape=[jax.ShapeDtypeStruct((bsz, lt, d), F32), jax.ShapeDtypeStruct((bsz, lt, d), F32)],
        scratch_shapes=[pltpu.VMEM((2, hps, HG_HEAD_DIM, HG_HEAD_DIM), F32)],
        compiler_params=_cparams("parallel", "parallel", "arbitrary"),
        name="hgrn2_gla",
    )(z, z, z, z, z, z, lb)
    return pl.pallas_call(
        _hg_out_kernel,
        grid=(bsz, nb),
        in_specs=[
            pl.BlockSpec((1, TM, d), lambda b, i: (b, i, 0)),
            pl.BlockSpec((1, TM, d), lambda b, i: (b, i, 0)),
            pl.BlockSpec((1, TM, d), lambda b, i: (b, i, 4)),
            pl.BlockSpec((1, d), lambda b, i: (0, 0)),
            pl.BlockSpec((d, d), lambda b, i: (0, 0)),
            pl.BlockSpec((1, TM, d), lambda b, i: (b, i, 0)),
            pl.BlockSpec((1, 1, 1, d), lambda b, i: (b, seg(i), 0, 0)),
        ],
        out_specs=pl.BlockSpec((1, TM, d), lambda b, i: (b, i, 0)),
        out_shape=jax.ShapeDtypeStruct((bsz, lt, d), F32),
        compiler_params=_cparams("parallel", "parallel"),
        name="hgrn2_out",
    )(outs[0], outs[1], z, norm_g.reshape(1, d), w_out.astype(BF16), hs, gate)


def _na_bias_table(rpb):
    w = np.arange(GRID_W)[:, None, None]
    kc = np.arange(GRID_W)[None, :, None]
    co = np.arange(2 * NA_KW - 1)[None, None, :]
    c0 = np.clip(w - NA_KW // 2, 0, GRID_W - NA_KW)
    valid = (kc >= c0) & (kc < c0 + NA_KW)
    onehot = jnp.asarray((valid & (kc - w + (NA_KW - 1) == co)).astype(np.float32))
    t = jnp.einsum("hrc,wkc->hrwk", rpb.astype(F32), onehot, precision=HI) * LOG2E
    t = jnp.where(jnp.asarray(valid[None, None, :, :, 0]), t, NEG_BIG)
    ext = jnp.pad(t, ((0, 0), (NA_KH, NA_KH), (0, 0), (0, 0)), constant_values=NEG_BIG)
    return jnp.concatenate([ext[:, :-1], ext[:, 1:]], axis=-1)


def _na_kernel(q_ref, k_ref, v_ref, bias_ref, o_ref, *, lc, rows):
    nblk = rows // NA_BLK_ROWS
    nq = NA_BLK_ROWS * GRID_W
    nwin = NA_WIN_ROWS * GRID_W
    o_ref[0, 0:lc, :] = jnp.zeros((lc, LANES), o_ref.dtype)
    kc = k_ref[0, 0:lc, :]
    vc = v_ref[0, 0:lc, :]
    lane = lax.broadcasted_iota(jnp.int32, (nq, LANES), 1)
    lo_half = lane < NA_HEAD_DIM
    key_lane = lax.broadcasted_iota(jnp.int32, (GRID_W, nwin), 1)
    half = NA_KH // 2

    def one_block(i):
        wr0 = jnp.clip(i * NA_BLK_ROWS - half, 0, rows - NA_WIN_ROWS)
        first, last = i == 0, i == nblk - 1
        delta0 = jnp.where(first, NA_KH - 1, jnp.where(last, -1, NA_KH - 1 - half))
        qoff = pl.multiple_of(lc + i * nq, GRID_W * NA_KH // 2)
        woff = pl.multiple_of(lc + wr0 * GRID_W, GRID_W * NA_KH // 2)
        q = q_ref[0, pl.ds(qoff, nq), :]
        kw = k_ref[0, pl.ds(woff, nwin), :]
        vw = v_ref[0, pl.ds(woff, nwin), :]
        zq = jnp.zeros_like(q)
        outs = []
        for e in range(2):
            qe = jnp.where(lo_half, q, zq) if e == 0 else jnp.where(lo_half, zq, q)
            s_all = _dot_nt(qe, kw)
            slabs = []
            for j in range(NA_BLK_ROWS):
                a = jnp.where(first, max(j - half, 0), jnp.where(last, min(j + half, NA_KH), j))
                valid = (key_lane >= a * GRID_W) & (key_lane < (a + NA_KH) * GRID_W)
                bias = jnp.concatenate([bias_ref[e, delta0 - j + NA_KH + 2 * m] for m in range(NA_WIN_ROWS // 2)],
                                       axis=1)
                slabs.append(jnp.where(valid, s_all[j * GRID_W:(j + 1) * GRID_W] + bias, NEG_BIG))
            s_loc = jnp.concatenate(slabs, axis=0)
            s_ctx = _dot_nt(qe, kc)
            m = jnp.maximum(jnp.max(s_loc, axis=-1, keepdims=True), jnp.max(s_ctx, axis=-1, keepdims=True))
            p_loc = jnp.exp2(s_loc - m)
            p_ctx = jnp.exp2(s_ctx - m)
            l = jnp.sum(p_loc, axis=-1, keepdims=True) + jnp.sum(p_ctx, axis=-1, keepdims=True)
            outs.append((_dot(p_loc.astype(BF16), vw) + _dot(p_ctx.astype(BF16), vc)) / l)
        o_ref[0, pl.ds(qoff, nq), :] = jnp.where(lo_half, outs[0], outs[1]).astype(o_ref.dtype)

    def body(ii, carry):
        for j in range(NA_BLK_UNROLL):
            one_block(ii * NA_BLK_UNROLL + j)
        return carry

    lax.fori_loop(0, nblk // NA_BLK_UNROLL, body, 0)


def _natten_layer(hs, g, sh, sc, gate, w_qkv, rpb, w_out, *, ncb, lc):
    bsz, lt, d = hs.shape
    rows = (lt - lc) // GRID_W
    assert rows >= NA_WIN_ROWS and rows % (NA_BLK_ROWS * NA_BLK_UNROLL) == 0
    npair = NA_HEADS // 2
    col_scale = jnp.concatenate([jnp.full((1, d), (NA_HEAD_DIM ** -0.5) * LOG2E, F32),
                                 jnp.ones((1, 2 * d), F32)], axis=-1)
    qkv = _norm_mod_matmul(hs, g, sh, sc, w_qkv.astype(BF16), col_scale, ncb=ncb, out_dtype=BF16, tn=1024)
    bias = _na_bias_table(rpb)
    o = pl.pallas_call(
        functools.partial(_na_kernel, lc=lc, rows=rows),
        grid=(npair, bsz),
        in_specs=[
            pl.BlockSpec((1, lt, LANES), lambda p, b: (b, 0, p)),
            pl.BlockSpec((1, lt, LANES), lambda p, b: (b, 0, npair + p)),
            pl.BlockSpec((1, lt, LANES), lambda p, b: (b, 0, 2 * npair + p)),
            pl.BlockSpec((2, NA_TABLE_PAIRS, GRID_W, 2 * GRID_W), lambda p, b: (p, 0, 0, 0)),
        ],
        out_specs=pl.BlockSpec((1, lt, LANES), lambda p, b: (b, 0, p)),
        out_shape=jax.ShapeDtypeStruct((bsz, lt, d), BF16),
        compiler_params=_cparams("parallel", "parallel"),
        name="natten",
    )(qkv, qkv, qkv, bias)
    return _matmul_gate_residual(o, w_out.astype(BF16), hs, gate, ncb=ncb)


def _layer_mods(mod_rows, bsz, d):
    mx = mod_rows[:bsz].reshape(bsz, 6, d)
    mc = jnp.broadcast_to(mod_rows[bsz].reshape(1, 6, d), (bsz, 6, d))
    m = jnp.stack([mc, mx], axis=1)
    return [m[:, :, k, None, :] for k in range(6)]


def kernel(x, c, ctx, c_ctx, ada_w, ada_b, norm1_g, norm2_g, mla_w_in, mla_q_norm_g, mla_w_q_up, mla_kv_norm_g, mla_w_kv_up, mla_w_out, s5_w_in, s5_lambda_re, s5_lambda_im, s5_log_dt, s5_b_re, s5_b_im, s5_c_re, s5_c_im, s5_d, s5_w_glu, hg_w_in, hg_lower_bound, hg_norm_g, hg_w_out, na_w_qkv, na_rpb, na_w_out, ffn_w_up, ffn_conv_w, ffn_conv_b, ffn_w_down, final_g):
    bsz, l, d = x.shape
    lc = ctx.shape[1]
    depth = ada_w.shape[0]
    assert lc % TM == 0 and l % TM == 0 and l % GRID_W == 0 and bsz + 1 <= 16
    ncb = lc // TM
    hs = jnp.concatenate([ctx, x], axis=1)
    cond_rows = jnp.zeros((16, d), F32).at[:bsz].set(c).at[bsz].set(c_ctx)
    mod_all = _ada_mod(cond_rows, ada_w, ada_b)
    lb_cum = jnp.cumsum(jax.nn.softmax(hg_lower_bound.astype(F32), axis=0), axis=0)
    lower_bounds = lb_cum - lb_cum[0]
    for i in range(depth):
        kind, j = i % 4, i // 4
        sh1, sc1, g1, sh2, sc2, g2 = _layer_mods(mod_all[i], bsz, d)
        if kind == 0:
            hs = _mla_layer(hs, norm1_g[i], sh1, sc1, g1, mla_w_in[j], mla_q_norm_g[j], mla_w_q_up[j],
                            mla_kv_norm_g[j], mla_w_kv_up[j], mla_w_out[j], ncb=ncb, lc=lc)
        elif kind == 1:
            hs = _s5_layer(hs, norm1_g[i], sh1, sc1, g1, s5_w_in[j], s5_lambda_re[j], s5_lambda_im[j],
                           s5_log_dt[j], s5_b_re[j], s5_b_im[j], s5_c_re[j], s5_c_im[j], s5_d[j], s5_w_glu[j],
                           ncb=ncb, lc=lc)
        elif kind == 2:
            hs = _hgrn2_layer(hs, norm1_g[i], sh1, sc1, g1, hg_w_in[j], lower_bounds[i], hg_norm_g[j],
                              hg_w_out[j], ncb=ncb, lc=lc)
        else:
            hs = _natten_layer(hs, norm1_g[i], sh1, sc1, g1, na_w_qkv[j], na_rpb[j], na_w_out[j],
                               ncb=ncb, lc=lc)
        hs = _conv_ffn(hs, norm2_g[i], sh2, sc2, g2, ffn_w_up[i], ffn_conv_w[i], ffn_conv_b[i],
                       ffn_w_down[i], final_g if i == depth - 1 else None, ncb=ncb)
    return hs
```

```python
import functools
import math

import jax
import jax.numpy as jnp
import numpy as np
from jax import lax
from jax.experimental import pallas as pl
from jax.experimental.pallas import tpu as pltpu

F32 = jnp.float32
BF16 = jnp.bfloat16
HI = lax.Precision.HIGHEST

EPS = 1e-6
GRID_W = 64
ROPE_THETA = 10000.0
LOG2E = math.log2(math.e)

LANES = 128
VMEM_LIMIT_BYTES = 56 * 1024 * 1024
TM = 256

MLA_HEADS = 16
MLA_Q_RANK = 384
MLA_KV_RANK = 256
MLA_NOPE = 64
MLA_ROPE = 32
MLA_V = 64
MLA_QK = MLA_NOPE + MLA_ROPE
MLA_HEADS_PER_STEP = 4

S5_GROUP_CH = 16
S5_STATE = 64
S5_T = 16
S5_SCAN_UNROLL = 4
S5_QGRP = LANES // S5_GROUP_CH

HG_HEAD_DIM = 128
HG_CHUNK = 64
HG_HEADS_PER_STEP = 4

NA_HEADS = 16
NA_HEAD_DIM = 64
NA_KH = 8
NA_KW = 16
NEG_BIG = -1e30
NA_BLK_ROWS = 8
NA_WIN_ROWS = 16
NA_TABLE_PAIRS = 4 * NA_KH - 2
NA_BLK_UNROLL = 2


def _cparams(*sem):
    return pltpu.CompilerParams(dimension_semantics=sem, vmem_limit_bytes=VMEM_LIMIT_BYTES)


def _seg(ncb):
    return lambda i: jnp.where(i >= ncb, 1, 0)


def _norm_mod(x, g, sh, sc):
    ms = jnp.mean(x * x, axis=-1, keepdims=True)
    return (x * lax.rsqrt(ms + EPS) * g) * (1.0 + sc) + sh


def _sigmoid(x):
    return 1.0 / (1.0 + jnp.exp(-x))


def _dot(a, b):
    return jnp.dot(a, b, preferred_element_type=F32)


def _dot_nt(a, b):
    return lax.dot_general(a, b, (((1,), (1,)), ((), ())), preferred_element_type=F32)


def _dot_tn(a, b):
    return lax.dot_general(a, b, (((0,), (0,)), ((), ())), preferred_element_type=F32)


def _ada_kernel(cond_ref, w_ref, b_ref, o_ref):
    cond = cond_ref[...]
    a = (cond * _sigmoid(cond)).astype(BF16)
    o_ref[0] = _dot(a, w_ref[0].astype(BF16)) + b_ref[0]


def _ada_mod(cond_rows, ada_w, ada_b):
    depth, d, n = ada_w.shape
    rows = cond_rows.shape[0]
    tn = 1536
    return pl.pallas_call(
        _ada_kernel,
        grid=(depth, n // tn),
        in_specs=[
            pl.BlockSpec((rows, d), lambda i, j: (0, 0)),
            pl.BlockSpec((1, d, tn), lambda i, j: (i, 0, j)),
            pl.BlockSpec((1, 1, tn), lambda i, j: (i, 0, j)),
        ],
        out_specs=pl.BlockSpec((1, rows, tn), lambda i, j: (i, 0, j)),
        out_shape=jax.ShapeDtypeStruct((depth, rows, n), F32),
        compiler_params=_cparams("parallel", "parallel"),
        name="ada_mod",
    )(cond_rows, ada_w, ada_b.reshape(depth, 1, n))


def _nmm_kernel(x_ref, g_ref, sh_ref, sc_ref, w_ref, cs_ref, *o_refs, tn):
    u = _norm_mod(x_ref[0], g_ref[...], sh_ref[0, 0], sc_ref[0, 0]).astype(BF16)
    n = w_ref.shape[1]
    for j in range(n // tn):
        sl = slice(j * tn, (j + 1) * tn)
        r = _dot(u, w_ref[:, sl]) * cs_ref[:, sl]
        for o_ref in o_refs:
            o_ref[0, :, sl] = r.astype(o_ref.dtype)


def _norm_mod_matmul(hs, g, sh, sc, w, col_scale, *, ncb, out_dtype, tn):
    bsz, lt, d = hs.shape
    n = w.shape[1]
    seg = _seg(ncb)
    multi = isinstance(out_dtype, tuple)
    dtypes = out_dtype if multi else (out_dtype,)
    outs = pl.pallas_call(
        functools.partial(_nmm_kernel, tn=tn),
        grid=(bsz, lt // TM),
        in_specs=[
            pl.BlockSpec((1, TM, d), lambda b, i: (b, i, 0)),
            pl.BlockSpec((1, d), lambda b, i: (0, 0)),
            pl.BlockSpec((1, 1, 1, d), lambda b, i: (b, seg(i), 0, 0)),
            pl.BlockSpec((1, 1, 1, d), lambda b, i: (b, seg(i), 0, 0)),
            pl.BlockSpec((d, n), lambda b, i: (0, 0)),
            pl.BlockSpec((1, n), lambda b, i: (0, 0)),
        ],
        out_specs=[pl.BlockSpec((1, TM, n), lambda b, i: (b, i, 0)) for _ in dtypes],
        out_shape=[jax.ShapeDtypeStruct((bsz, lt, n), dt) for dt in dtypes],
        compiler_params=_cparams("parallel", "parallel"),
        name="norm_mod_matmul",
    )(hs, g.reshape(1, d), sh, sc, w, col_scale)
    return tuple(outs) if multi else outs[0]


def _mgr_kernel(a_ref, w_ref, x_ref, gate_ref, o_ref):
    o_ref[0] = x_ref[0] + gate_ref[0, 0] * _dot(a_ref[0], w_ref[...])


def _matmul_gate_residual(a, w, hs, gate, *, ncb):
    bsz, lt, d = hs.shape
    k = a.shape[2]
    seg = _seg(ncb)
    return pl.pallas_call(
        _mgr_kernel,
        grid=(bsz, lt // TM),
        in_specs=[
            pl.BlockSpec((1, TM, k), lambda b, i: (b, i, 0)),
            pl.BlockSpec((k, d), lambda b, i: (0, 0)),
            pl.BlockSpec((1, TM, d), lambda b, i: (b, i, 0)),
            pl.BlockSpec((1, 1, 1, d), lambda b, i: (b, seg(i), 0, 0)),
        ],
        out_specs=pl.BlockSpec((1, TM, d), lambda b, i: (b, i, 0)),
        out_shape=jax.ShapeDtypeStruct((bsz, lt, d), F32),
        compiler_params=_cparams("parallel", "parallel"),
        name="matmul_gate_residual",
    )(a, w, hs, gate)


FFN_HALO = 16
FFN_PAD = 8


def _ffn_kernel(x_ref, xp_ref, xn_ref, g_ref, sh_ref, sc_ref, gate_ref, wup_ref, cw_ref, cb_ref,
                wdn_ref, fg_ref, o_ref, u_scr, h_scr, *, ncb, nb, off, final):
    i = pl.program_id(1) + off
    g = g_ref[...]
    sh = sh_ref[0, 0]
    sc = sc_ref[0, 0]
    f = wdn_ref.shape[0]
    keep_prev = jnp.where((i == 0) | (i == ncb), 0.0, 1.0)
    keep_next = jnp.where((i == ncb - 1) | (i == nb - 1), 0.0, 1.0)
    u_prev = _norm_mod(xp_ref[0], g, sh, sc)[FFN_HALO - 1:FFN_HALO] * keep_prev
    u_next = _norm_mod(xn_ref[0], g, sh, sc)[0:1] * keep_next
    row = lax.broadcasted_iota(jnp.int32, (FFN_HALO, u_prev.shape[1]), 0)
    extra = jnp.where(row == 0, u_prev, jnp.where(row == 1, u_next, 0.0))
    u_scr[0:TM, :] = _norm_mod(x_ref[0], g, sh, sc).astype(BF16)
    u_scr[TM:, :] = extra.astype(BF16)
    h = _dot(u_scr[...], wup_ref[...])
    h_scr[FFN_PAD:FFN_PAD + TM, :] = h[0:TM]
    h_scr[FFN_PAD - 1:FFN_PAD, :] = h[TM:TM + 1]
    h_scr[FFN_PAD + TM:FFN_PAD + TM + 1, :] = h[TM + 1:TM + 2]
    w = cw_ref[...]
    hc = (h_scr[FFN_PAD - 1:FFN_PAD - 1 + TM, :] * w[0:1] + h_scr[FFN_PAD:FFN_PAD + TM, :] * w[1:2]
          + h_scr[FFN_PAD + 1:FFN_PAD + 1 + TM, :] * w[2:3] + cb_ref[...])
    gg = hc[:, f:]
    act = (hc[:, :f] * (gg * _sigmoid(gg))).astype(BF16)
    y = x_ref[0] + gate_ref[0, 0] * _dot(act, wdn_ref[...])
    if final:
        y = y * lax.rsqrt(jnp.mean(y * y, axis=-1, keepdims=True) + EPS) * fg_ref[...]
    o_ref[0] = y


def _conv_ffn(hs, g, sh, sc, gate, w_up, conv_w, conv_b, w_down, final_g=None, *, ncb):
    bsz, lt, d = hs.shape
    f = w_down.shape[0]
    nb = lt // TM
    final = final_g is not None
    off = ncb if final else 0
    hb = TM // FFN_HALO
    nhalo = lt // FFN_HALO
    seg = _seg(ncb)
    fg = (final_g if final else jnp.ones((d,), F32)).reshape(1, d)
    return pl.pallas_call(
        functools.partial(_ffn_kernel, ncb=ncb, nb=nb, off=off, final=final),
        grid=(bsz, nb - off),
        in_specs=[
            pl.BlockSpec((1, TM, d), lambda b, i: (b, i + off, 0)),
            pl.BlockSpec((1, FFN_HALO, d), lambda b, i: (b, jnp.maximum((i + off) * hb - 1, 0), 0)),
            pl.BlockSpec((1, FFN_HALO, d), lambda b, i: (b, jnp.minimum((i + off + 1) * hb, nhalo - 1), 0)),
            pl.BlockSpec((1, d), lambda b, i: (0, 0)),
            pl.BlockSpec((1, 1, 1, d), lambda b, i: (b, seg(i + off), 0, 0)),
            pl.BlockSpec((1, 1, 1, d), lambda b, i: (b, seg(i + off), 0, 0)),
            pl.BlockSpec((1, 1, 1, d), lambda b, i: (b, seg(i + off), 0, 0)),
            pl.BlockSpec((d, 2 * f), lambda b, i: (0, 0)),
            pl.BlockSpec((3, 2 * f), lambda b, i: (0, 0)),
            pl.BlockSpec((1, 2 * f), lambda b, i: (0, 0)),
            pl.BlockSpec((f, d), lambda b, i: (0, 0)),
            pl.BlockSpec((1, d), lambda b, i: (0, 0)),
        ],
        out_specs=pl.BlockSpec((1, TM, d), lambda b, i: (b, i, 0)),
        out_shape=jax.ShapeDtypeStruct((bsz, lt - off * TM, d), F32),
        scratch_shapes=[
            pltpu.VMEM((TM + FFN_HALO, d), BF16),
            pltpu.VMEM((TM + 2 * FFN_PAD, 2 * f), F32),
        ],
        compiler_params=_cparams("parallel", "parallel"),
        name="conv_ffn",
    )(hs, hs, hs, g.reshape(1, d), sh, sc, gate, w_up.astype(BF16), conv_w, conv_b.reshape(1, 2 * f),
      w_down.astype(BF16), fg)


def _rope_tables(lc, l):
    half = MLA_ROPE // 4
    inv = 1.0 / (ROPE_THETA ** (jnp.arange(half, dtype=F32) / half))
    t = jnp.arange(l)
    ang_r = (t // GRID_W).astype(F32)[:, None] * inv[None, :]
    ang_c = (t % GRID_W).astype(F32)[:, None] * inv[None, :]
    cos = jnp.concatenate([jnp.cos(ang_r), jnp.cos(ang_r), jnp.cos(ang_c), jnp.cos(ang_c)], axis=-1)
    sin = jnp.concatenate([-jnp.sin(ang_r), jnp.sin(ang_r), -jnp.sin(ang_c), jnp.sin(ang_c)], axis=-1)
    cos_t = jnp.ones((lc + l, LANES), F32).at[lc:, MLA_NOPE:MLA_NOPE + MLA_ROPE].set(cos)
    sin_t = jnp.zeros((lc + l, LANES), F32).at[lc:, MLA_NOPE:MLA_NOPE + MLA_ROPE].set(sin)
    return cos_t, sin_t


def _mla_proj_kernel(x_ref, g_ref, sh_ref, sc_ref, win_ref, qg_ref, wq_ref, kvg_ref, wkn_ref, wv_ref,
                     cos_ref, sin_ref, q_ref, k_ref, v_ref):
    u = _norm_mod(x_ref[0], g_ref[...], sh_ref[0, 0], sc_ref[0, 0]).astype(BF16)
    lat = _dot(u, win_ref[...])
    q_lat = lat[:, :MLA_Q_RANK]
    kv_lat = lat[:, MLA_Q_RANK:MLA_Q_RANK + MLA_KV_RANK]
    kr = lat[:, MLA_Q_RANK + MLA_KV_RANK:]

    def rms(t, gg):
        return (t * lax.rsqrt(jnp.mean(t * t, axis=-1, keepdims=True) + EPS) * gg).astype(BF16)

    qn = rms(q_lat, qg_ref[...])
    kvn = rms(kv_lat, kvg_ref[...])
    cos = cos_ref[...]
    sin = sin_ref[...]
    lane = lax.broadcasted_iota(jnp.int32, (TM, LANES), 1)
    first_half = (lane % (MLA_ROPE // 2)) < (MLA_ROPE // 4)

    def rope(t):
        partner = jnp.where(first_half, pltpu.roll(t, LANES - MLA_ROPE // 4, 1), pltpu.roll(t, MLA_ROPE // 4, 1))
        return t * cos + partner * sin

    kr_rot = rope(kr)
    v_ref[0] = _dot(kvn, wv_ref[...]).astype(BF16)
    q_all = _dot(qn, wq_ref[...]) * ((MLA_QK ** -0.5) * LOG2E)
    kn_all = _dot(kvn, wkn_ref[...])
    for h in range(MLA_HEADS):
        sl = slice(h * LANES, (h + 1) * LANES)
        q_ref[0, :, sl] = rope(q_all[:, sl]).astype(BF16)
        k_ref[0, :, sl] = (kn_all[:, sl] + kr_rot).astype(BF16)


def _mla_attn_kernel(q_ref, k_ref, v_ref, o_ref, *, ncb, lc):
    i = pl.program_id(2)

    def attend(lk):
        q = q_ref[0]
        lane = lax.broadcasted_iota(jnp.int32, (q.shape[0], LANES), 1)
        for pair in range(q.shape[1] // (2 * LANES)):
            v = v_ref[0, 0:lk, pair * LANES:(pair + 1) * LANES]
            outs = []
            for e in range(2):
                sl = slice((2 * pair + e) * LANES, (2 * pair + e + 1) * LANES)
                s = _dot_nt(q[:, sl], k_ref[0, 0:lk, sl])
                m = jnp.max(s, axis=-1, keepdims=True)
                p = jnp.exp2(s - m)
                l = jnp.sum(p, axis=-1, keepdims=True)
                outs.append(_dot(p.astype(BF16), v) / l)
            o_ref[0, :, pair * LANES:(pair + 1) * LANES] = jnp.where(lane < MLA_V, outs[0], outs[1]).astype(o_ref.dtype)

    @pl.when(i < ncb)
    def _():
        attend(lc)

    @pl.when(i >= ncb)
    def _():
        attend(k_ref.shape[1])


def _mla_layer(hs, g, sh, sc, gate, w_in, q_norm_g, w_q_up, kv_norm_g, w_kv_up, w_out, *, ncb, lc):
    bsz, lt, d = hs.shape
    l = lt - lc
    nh = MLA_HEADS
    seg = _seg(ncb)
    w_in_p = jnp.zeros((d, 768), F32)
    w_in_p = w_in_p.at[:, :MLA_Q_RANK + MLA_KV_RANK].set(w_in[:, :MLA_Q_RANK + MLA_KV_RANK])
    w_in_p = w_in_p.at[:, 640 + MLA_NOPE:640 + MLA_QK].set(w_in[:, MLA_Q_RANK + MLA_KV_RANK:])
    wq = w_q_up.reshape(MLA_Q_RANK, nh, MLA_QK)
    wq = jnp.pad(wq, ((0, 0), (0, 0), (0, LANES - MLA_QK))).reshape(MLA_Q_RANK, nh * LANES)
    wkv = w_kv_up.reshape(MLA_KV_RANK, nh, MLA_NOPE + MLA_V)
    wkn = jnp.pad(wkv[:, :, :MLA_NOPE], ((0, 0), (0, 0), (0, LANES - MLA_NOPE))).reshape(MLA_KV_RANK, nh * LANES)
    wv = wkv[:, :, MLA_NOPE:].reshape(MLA_KV_RANK, nh * MLA_V)
    cos_t, sin_t = _rope_tables(lc, l)

    q, k, v = pl.pallas_call(
        _mla_proj_kernel,
        grid=(bsz, lt // TM),
        in_specs=[
            pl.BlockSpec((1, TM, d), lambda b, i: (b, i, 0)),
            pl.BlockSpec((1, d), lambda b, i: (0, 0)),
            pl.BlockSpec((1, 1, 1, d), lambda b, i: (b, seg(i), 0, 0)),
            pl.BlockSpec((1, 1, 1, d), lambda b, i: (b, seg(i), 0, 0)),
            pl.BlockSpec((d, 768), lambda b, i: (0, 0)),
            pl.BlockSpec((1, MLA_Q_RANK), lambda b, i: (0, 0)),
            pl.BlockSpec((MLA_Q_RANK, nh * LANES), lambda b, i: (0, 0)),
            pl.BlockSpec((1, MLA_KV_RANK), lambda b, i: (0, 0)),
            pl.BlockSpec((MLA_KV_RANK, nh * LANES), lambda b, i: (0, 0)),
            pl.BlockSpec((MLA_KV_RANK, nh * MLA_V), lambda b, i: (0, 0)),
            pl.BlockSpec((TM, LANES), lambda b, i: (i, 0)),
            pl.BlockSpec((TM, LANES), lambda b, i: (i, 0)),
        ],
        out_specs=[
            pl.BlockSpec((1, TM, nh * LANES), lambda b, i: (b, i, 0)),
            pl.BlockSpec((1, TM, nh * LANES), lambda b, i: (b, i, 0)),
            pl.BlockSpec((1, TM, nh * MLA_V), lambda b, i: (b, i, 0)),
        ],
        out_shape=[
            jax.ShapeDtypeStruct((bsz, lt, nh * LANES), BF16),
            jax.ShapeDtypeStruct((bsz, lt, nh * LANES), BF16),
            jax.ShapeDtypeStruct((bsz, lt, nh * MLA_V), BF16),
        ],
        compiler_params=_cparams("parallel", "parallel"),
        name="mla_proj",
    )(hs, g.reshape(1, d), sh, sc, w_in_p.astype(BF16), q_norm_g.reshape(1, -1), wq.astype(BF16),
      kv_norm_g.reshape(1, -1), wkn.astype(BF16), wv.astype(BF16), cos_t, sin_t)

    o = pl.pallas_call(
        functools.partial(_mla_attn_kernel, ncb=ncb, lc=lc),
        grid=(bsz, nh // MLA_HEADS_PER_STEP, lt // TM),
        in_specs=[
            pl.BlockSpec((1, TM, MLA_HEADS_PER_STEP * LANES), lambda b, p, i: (b, i, p)),
            pl.BlockSpec((1, lt, MLA_HEADS_PER_STEP * LANES), lambda b, p, i: (b, 0, p)),
            pl.BlockSpec((1, lt, MLA_HEADS_PER_STEP * MLA_V), lambda b, p, i: (b, 0, p)),
        ],
        out_specs=pl.BlockSpec((1, TM, MLA_HEADS_PER_STEP * MLA_V), lambda b, p, i: (b, i, p)),
        out_shape=jax.ShapeDtypeStruct((bsz, lt, nh * MLA_V), BF16),
        compiler_params=_cparams("parallel", "parallel", "parallel"),
        name="mla_attn",
    )(q, k, v)
    return _matmul_gate_residual(o, w_out.astype(BF16), hs, gate, ncb=ncb)


def _s5_matrices(lam_re, lam_im, log_dt, b_re, b_im, c_re, c_im):
    t_len = S5_T
    n_grp = lam_re.shape[1]
    ch = S5_GROUP_CH
    n_st = S5_STATE
    dt = jnp.exp(log_dt.astype(F32))[..., None]
    ld_re = lam_re.astype(F32) * dt
    ld_im = lam_im.astype(F32) * dt
    tau = jnp.arange(t_len + 1, dtype=F32)[:, None, None, None]
    mag = jnp.exp(tau * ld_re[None])
    pw_re = mag * jnp.cos(tau * ld_im[None])
    pw_im = mag * jnp.sin(tau * ld_im[None])
    tau_r = jnp.arange(t_len, -1, -1, dtype=F32)[:, None, None, None]
    mag_r = jnp.exp(tau_r * ld_re[None])
    pr_re = mag_r * jnp.cos(tau_r * ld_im[None])
    pr_im = mag_r * jnp.sin(tau_r * ld_im[None])
    lb_re, lb_im = pw_re[1] - 1.0, pw_im[1]
    den = lam_re * lam_re + lam_im * lam_im
    f_re = (lb_re * lam_re + lb_im * lam_im) / den
    f_im = (lb_im * lam_re - lb_re * lam_im) / den
    bb_re = f_re[..., None] * b_re - f_im[..., None] * b_im
    bb_im = f_re[..., None] * b_im + f_im[..., None] * b_re
    cp_re = c_re[None] * pw_re[:t_len, :, :, None, :] - c_im[None] * pw_im[:t_len, :, :, None, :]
    cp_im = c_re[None] * pw_im[:t_len, :, :, None, :] + c_im[None] * pw_re[:t_len, :, :, None, :]
    taps = (jnp.einsum("tdgon,dgni->tdgoi", cp_re, bb_re, precision=HI)
            - jnp.einsum("tdgon,dgni->tdgoi", cp_im, bb_im, precision=HI))
    s_idx = np.arange(t_len)[:, None, None]
    t_idx = np.arange(t_len)[None, :, None]
    u_idx = np.arange(t_len)[None, None, :]
    sel_f = jnp.asarray((t_idx - s_idx == u_idx).astype(np.float32))
    sel_b = jnp.asarray((s_idx - t_idx == u_idx).astype(np.float32))
    kf = jnp.einsum("stu,ugoi->stgoi", sel_f, taps[:, 0], precision=HI)
    kb = jnp.einsum("stu,ugoi->stgoi", sel_b, taps[:, 1], precision=HI)
    k_tot = (kf + kb).transpose(2, 0, 4, 1, 3).reshape(n_grp, t_len * ch, t_len * ch)

    def state_in(pw_r, pw_i, brr, bii):
        re = pw_r[..., None] * brr[None] - pw_i[..., None] * bii[None]
        im = pw_r[..., None] * bii[None] + pw_i[..., None] * brr[None]
        return (re.transpose(1, 0, 3, 2).reshape(n_grp, t_len * ch, n_st),
                im.transpose(1, 0, 3, 2).reshape(n_grp, t_len * ch, n_st))

    inf_re, inf_im = state_in(pr_re[1:, 0], pr_im[1:, 0], bb_re[0], bb_im[0])
    inb_re, inb_im = state_in(pw_re[:t_len, 1], pw_im[:t_len, 1], bb_re[1], bb_im[1])

    def state_out(pw_r, pw_i, crr, cii):
        re = crr[None] * pw_r[:, :, None, :] - cii[None] * pw_i[:, :, None, :]
        im = crr[None] * pw_i[:, :, None, :] + cii[None] * pw_r[:, :, None, :]
        return (re.transpose(1, 3, 0, 2).reshape(n_grp, n_st, t_len * ch),
                (-im).transpose(1, 3, 0, 2).reshape(n_grp, n_st, t_len * ch))

    outf_re, outf_im = state_out(pw_re[1:, 0], pw_im[1:, 0], c_re[0], c_im[0])
    outb_re, outb_im = state_out(pr_re[:t_len, 1], pr_im[:t_len, 1], c_re[1], c_im[1])
    nq = n_grp // S5_QGRP
    eye = jnp.eye(S5_QGRP, dtype=BF16)
    kt = k_tot.astype(BF16).reshape(nq, S5_QGRP, t_len, ch, t_len, ch)
    m_chunk = jnp.einsum("qgtcuo,gh->qtgcuho", kt, eye).reshape(nq, S5_QGRP * t_len * ch, S5_QGRP * t_len * ch)
    w_in = jnp.stack([inf_re, inf_im, inb_re, inb_im], 0).astype(BF16)
    w_in = w_in.reshape(4, nq, S5_QGRP, t_len, ch, n_st)
    w_in = jnp.einsum("kqgtcn,gh->qtgckhn", w_in, eye).reshape(nq, S5_QGRP * t_len * ch, 4 * S5_QGRP * n_st)
    m_out = jnp.stack([outf_re, outf_im, outb_re, outb_im], 0).astype(BF16)
    m_out = m_out.reshape(4, nq, S5_QGRP, n_st, t_len, ch)
    m_out = jnp.einsum("kqgnuo,gh->qkgnuho", m_out, eye).reshape(nq, 4, S5_QGRP * n_st, S5_QGRP * t_len * ch)
    decay = jnp.stack([pw_re[t_len, 0], pw_im[t_len, 0], pw_re[t_len, 1], pw_im[t_len, 1]], 0)
    decay = decay.reshape(4, nq, S5_QGRP * n_st).transpose(1, 0, 2).reshape(nq, 1, 4 * S5_QGRP * n_st)
    return w_in, m_chunk, m_out, decay


def _s5_chunk_rows(z_ref):
    jt = z_ref.shape[1] // S5_T
    return jnp.concatenate([z_ref[0, pl.ds(t, jt, stride=S5_T), :].astype(BF16) for t in range(S5_T)], axis=1)


def _s5_core_kernel(z_ref, win_ref, mch_ref, mout_ref, a_ref, y_ref, u_scr, p_scr, *, jc, jt):
    sc = a_ref.shape[2] // 4
    zc = _s5_chunk_rows(z_ref)
    u_scr[...] = _dot(zc, win_ref[0])
    a = a_ref[0]
    afr, afi, abr, abi = (a[:, k * sc:(k + 1) * sc] for k in range(4))
    zero = jnp.zeros((1, sc), F32)

    def body(jj, carry):
        fr, fi, br, bi = carry
        jb = jnp.where(jj < jc, jc - 1 - jj, jt - 1 - jj + jc)
        p_scr[pl.ds(jj, 1), 0:sc] = fr
        p_scr[pl.ds(jj, 1), sc:2 * sc] = fi
        p_scr[pl.ds(jb, 1), 2 * sc:3 * sc] = br
        p_scr[pl.ds(jb, 1), 3 * sc:4 * sc] = bi
        nfr = afr * fr - afi * fi + u_scr[pl.ds(jj, 1), 0:sc]
        nfi = afr * fi + afi * fr + u_scr[pl.ds(jj, 1), sc:2 * sc]
        nbr = abr * br - abi * bi + u_scr[pl.ds(jb, 1), 2 * sc:3 * sc]
        nbi = abr * bi + abi * br + u_scr[pl.ds(jb, 1), 3 * sc:4 * sc]
        return nfr, nfi, nbr, nbi

    lax.fori_loop(0, jt, body, (zero, zero, zero, zero), unroll=S5_SCAN_UNROLL)
    acc = _dot(zc, mch_ref[0]) + _dot(p_scr[...].astype(BF16), mout_ref[0])
    for t in range(S5_T):
        y_ref[0, pl.ds(t, jt, stride=S5_T), :] = acc[:, t * LANES:(t + 1) * LANES]


def _s5_glu_kernel(y_ref, z_ref, d_ref, w_ref, x_ref, gate_ref, o_ref):
    d = x_ref.shape[2]
    y = y_ref[0] + z_ref[0] * d_ref[...]
    ge = jax.nn.gelu(y).astype(BF16)
    ag = _dot(ge, w_ref[...])
    o_ref[0] = x_ref[0] + gate_ref[0, 0] * (ag[:, :d] * _sigmoid(ag[:, d:]))


def _s5_layer(hs, g, sh, sc, gate, w_in, lam_re, lam_im, log_dt, b_re, b_im, c_re, c_im, d_skip, w_glu,
              *, ncb, lc):
    bsz, lt, d = hs.shape
    width = w_in.shape[1]
    nq = width // LANES
    jt = lt // S5_T
    jc = lc // S5_T
    ccols = S5_T * LANES
    scols = 4 * S5_QGRP * S5_STATE
    seg = _seg(ncb)
    z = _norm_mod_matmul(hs, g, sh, sc, w_in.astype(BF16), jnp.ones((1, width), F32),
                         ncb=ncb, out_dtype=F32, tn=width)
    w_si, m_chunk, m_out, decay = _s5_matrices(lam_re, lam_im, log_dt, b_re, b_im, c_re, c_im)

    def resident(shape):
        return pl.BlockSpec((1,) + shape, lambda q, b: (q, 0, 0), pipeline_mode=pl.Buffered(1))

    y = pl.pallas_call(
        functools.partial(_s5_core_kernel, jc=jc, jt=jt),
        grid=(nq, bsz),
        in_specs=[
            pl.BlockSpec((1, lt, LANES), lambda q, b: (b, 0, q)),
            resident((ccols, scols)),
            resident((ccols, ccols)),
            resident((scols, ccols)),
            pl.BlockSpec((1, 1, scols), lambda q, b: (q, 0, 0)),
        ],
        out_specs=pl.BlockSpec((1, lt, LANES), lambda q, b: (b, 0, q)),
        out_shape=jax.ShapeDtypeStruct((bsz, lt, width), F32),
        scratch_shapes=[pltpu.VMEM((jt, scols), F32), pltpu.VMEM((jt, scols), F32)],
        compiler_params=_cparams("parallel", "parallel"),
        name="s5_core",
    )(z, w_si, m_chunk, m_out.reshape(nq, scols, ccols), decay)
    return pl.pallas_call(
        _s5_glu_kernel,
        grid=(bsz, lt // TM),
        in_specs=[
            pl.BlockSpec((1, TM, width), lambda b, i: (b, i, 0)),
            pl.BlockSpec((1, TM, width), lambda b, i: (b, i, 0)),
            pl.BlockSpec((1, width), lambda b, i: (0, 0)),
            pl.BlockSpec((width, 2 * d), lambda b, i: (0, 0)),
            pl.BlockSpec((1, TM, d), lambda b, i: (b, i, 0)),
            pl.BlockSpec((1, 1, 1, d), lambda b, i: (b, seg(i), 0, 0)),
        ],
        out_specs=pl.BlockSpec((1, TM, d), lambda b, i: (b, i, 0)),
        out_shape=jax.ShapeDtypeStruct((bsz, lt, d), F32),
        compiler_params=_cparams("parallel", "parallel"),
        name="s5_glu",
    )(y, z, d_skip.reshape(1, width), w_glu.astype(BF16), hs, gate)


def _gla_dir(zq, zf, zv, lb, states, incl, tri, rev):
    dk = HG_HEAD_DIM
    width = zq.shape[1]
    nchunks = zq.shape[0] // HG_CHUNK
    forget = lb + (1.0 - lb) * _sigmoid(zf)
    lf = jnp.log(forget)
    kk = 1.0 - forget
    v_all = zv.astype(BF16)
    hi = lf.astype(BF16)
    r1 = lf - hi.astype(F32)
    mid = r1.astype(BF16)
    lo = (r1 - mid.astype(F32)).astype(BF16)
    parts = _dot(tri, jnp.concatenate([hi, mid, lo], axis=1))
    bcum = parts[:, :width] + parts[:, width:2 * width] + parts[:, 2 * width:]
    btot = jnp.concatenate(
        [jnp.broadcast_to(bcum[c * HG_CHUNK:c * HG_CHUNK + 1] if rev else bcum[(c + 1) * HG_CHUNK - 1:(c + 1) * HG_CHUNK],
                          (HG_CHUNK, width)) for c in range(nchunks)], axis=0)
    q_in_all = (zq * _sigmoid(zq)) * (dk ** -0.5) * jnp.exp(bcum)
    k_in_all = (kk * jnp.exp(-bcum)).astype(BF16)
    k_out_all = kk * jnp.exp(btot - bcum)
    chunk_of_row = lax.broadcasted_iota(jnp.int32, (zq.shape[0], dk), 0) // HG_CHUNK
    zero = jnp.zeros((zq.shape[0], dk), F32)
    outs, new_states = [], []
    for h, st in enumerate(states):
        sl = slice(h * dk, (h + 1) * dk)
        q_in, k_out, v = q_in_all[:, sl], k_out_all[:, sl], v_all[:, sl]
        att = jnp.where(incl, _dot_nt(q_in.astype(BF16), k_in_all[:, sl]), 0.0)
        o_intra = _dot(att.astype(BF16), v)
        k_cat = jnp.concatenate([jnp.where(chunk_of_row == c, k_out, zero) for c in range(nchunks)], axis=1)
        q_cat = jnp.concatenate([jnp.where(chunk_of_row == c, q_in, zero) for c in range(nchunks)], axis=1)
        ds = _dot_tn(v, k_cat.astype(BF16))
        entering = [None] * nchunks
        for cc in range(nchunks):
            c = nchunks - 1 - cc if rev else cc
            entering[c] = st
            st = st * jnp.exp(btot[c * HG_CHUNK:c * HG_CHUNK + 1, sl]) + ds[:, c * dk:(c + 1) * dk]
        scat = jnp.concatenate(entering, axis=1).astype(BF16)
        outs.append(o_intra + _dot_nt(q_cat.astype(BF16), scat))
        new_states.append(st)
    return outs, new_states


def _gla_kernel(zqf_ref, zff_ref, zvf_ref, zqb_ref, zfb_ref, zvb_ref, lb_ref, of_ref, ob_ref, st_scr):
    @pl.when(pl.program_id(2) == 0)
    def _():
        st_scr[...] = jnp.zeros_like(st_scr)

    n = zqf_ref.shape[1]
    row = lax.broadcasted_iota(jnp.int32, (n, n), 0)
    col = lax.broadcasted_iota(jnp.int32, (n, n), 1)
    same = (row // HG_CHUNK) == (col // HG_CHUNK)
    dk = HG_HEAD_DIM
    for direction, (zq_ref, zf_ref, zv_ref, o_ref) in enumerate(
            ((zqf_ref, zff_ref, zvf_ref, of_ref), (zqb_ref, zfb_ref, zvb_ref, ob_ref))):
        rev = direction == 1
        incl = same & ((col >= row) if rev else (col <= row))
        tri = incl.astype(BF16)
        nheads = zq_ref.shape[2] // dk
        outs, states = _gla_dir(zq_ref[0], zf_ref[0], zv_ref[0], lb_ref[0],
                                [st_scr[direction, h] for h in range(nheads)], incl, tri, rev)
        for h in range(nheads):
            o_ref[0, :, h * dk:(h + 1) * dk] = outs[h]
            st_scr[direction, h] = states[h]


def _hg_out_kernel(of_ref, ob_ref, zg_ref, ng_ref, w_ref, x_ref, gate_ref, o_ref):
    o = of_ref[0] + ob_ref[0]
    gsig = zg_ref[0]
    gsig = gsig * _sigmoid(gsig)
    ng = ng_ref[...]
    parts = []
    for h in range(o.shape[1] // HG_HEAD_DIM):
        sl = slice(h * HG_HEAD_DIM, (h + 1) * HG_HEAD_DIM)
        oh = o[:, sl]
        on = oh * lax.rsqrt(jnp.mean(oh * oh, axis=-1, keepdims=True) + EPS)
        parts.append((on * ng[:, sl] * gsig[:, sl]).astype(BF16))
    a = jnp.concatenate(parts, axis=-1)
    o_ref[0] = x_ref[0] + gate_ref[0, 0] * _dot(a, w_ref[...])


def _hgrn2_layer(hs, g, sh, sc, gate, w_in, lower_bound, norm_g, w_out, *, ncb, lc):
    bsz, lt, d = hs.shape
    nh = d // HG_HEAD_DIM
    nb = lt // TM
    seg = _seg(ncb)
    z = _norm_mod_matmul(hs, g, sh, sc, w_in.astype(BF16), jnp.ones((1, 5 * d), F32),
                         ncb=ncb, out_dtype=F32, tn=1024)
    lb = lower_bound.astype(F32).reshape(1, 1, d)
    hps = HG_HEADS_PER_STEP
    ng = nh // hps
    wb = hps * HG_HEAD_DIM

    def rblk(s):
        return jnp.where(s < ncb, ncb - 1 - s, nb - 1 - s + ncb)

    outs = pl.pallas_call(
        _gla_kernel,
        grid=(bsz, ng, nb),
        in_specs=[
            pl.BlockSpec((1, TM, wb), lambda b, h, s: (b, s, h)),
            pl.BlockSpec((1, TM, wb), lambda b, h, s: (b, s, ng + h)),
            pl.BlockSpec((1, TM, wb), lambda b, h, s: (b, s, 3 * ng + h)),
            pl.BlockSpec((1, TM, wb), lambda b, h, s: (b, rblk(s), h)),
            pl.BlockSpec((1, TM, wb), lambda b, h, s: (b, rblk(s), 2 * ng + h)),
            pl.BlockSpec((1, TM, wb), lambda b, h, s: (b, rblk(s), 3 * ng + h)),
            pl.BlockSpec((1, 1, wb), lambda b, h, s: (0, 0, h)),
        ],
        out_specs=[
            pl.BlockSpec((1, TM, wb), lambda b, h, s: (b, s, h)),
            pl.BlockSpec((1, TM, wb), lambda b, h, s: (b, rblk(s), h)),
        ],
        out_shape=[jax.ShapeDtypeStruct((bsz, lt, d), F32), jax.ShapeDtypeStruct((bsz, lt, d), F32)],
        scratch_shapes=[pltpu.VMEM((2, hps, HG_HEAD_DIM, HG_HEAD_DIM), F32)],
        compiler_params=_cparams("parallel", "parallel", "arbitrary"),
        name="hgrn2_gla",
    )(z, z, z, z, z, z, lb)
    return pl.pallas_call(
        _hg_out_kernel,
        grid=(bsz, nb),
        in_specs=[
            pl.BlockSpec((1, TM, d), lambda b, i: (b, i, 0)),
            pl.BlockSpec((1, TM, d), lambda b, i: (b, i, 0)),
            pl.BlockSpec((1, TM, d), lambda b, i: (b, i, 4)),
            pl.BlockSpec((1, d), lambda b, i: (0, 0)),
            pl.BlockSpec((d, d), lambda b, i: (0, 0)),
            pl.BlockSpec((1, TM, d), lambda b, i: (b, i, 0)),
            pl.BlockSpec((1, 1, 1, d), lambda b, i: (b, seg(i), 0, 0)),
        ],
        out_specs=pl.BlockSpec((1, TM, d), lambda b, i: (b, i, 0)),
        out_shape=jax.ShapeDtypeStruct((bsz, lt, d), F32),
        compiler_params=_cparams("parallel", "parallel"),
        name="hgrn2_out",
    )(outs[0], outs[1], z, norm_g.reshape(1, d), w_out.astype(BF16), hs, gate)


def _na_bias_table(rpb):
    w = np.arange(GRID_W)[:, None, None]
    kc = np.arange(GRID_W)[None, :, None]
    co = np.arange(2 * NA_KW - 1)[None, None, :]
    c0 = np.clip(w - NA_KW // 2, 0, GRID_W - NA_KW)
    valid = (kc >= c0) & (kc < c0 + NA_KW)
    onehot = jnp.asarray((valid & (kc - w + (NA_KW - 1) == co)).astype(np.float32))
    t = jnp.einsum("hrc,wkc->hrwk", rpb.astype(F32), onehot, precision=HI) * LOG2E
    t = jnp.where(jnp.asarray(valid[None, None, :, :, 0]), t, NEG_BIG)
    ext = jnp.pad(t, ((0, 0), (NA_KH, NA_KH), (0, 0), (0, 0)), constant_values=NEG_BIG)
    return jnp.concatenate([ext[:, :-1], ext[:, 1:]], axis=-1)


def _na_kernel(q_ref, k_ref, v_ref, bias_ref, o_ref, *, lc, rows):
    nblk = rows // NA_BLK_ROWS
    nq = NA_BLK_ROWS * GRID_W
    nwin = NA_WIN_ROWS * GRID_W
    o_ref[0, 0:lc, :] = jnp.zeros((lc, LANES), o_ref.dtype)
    kc = k_ref[0, 0:lc, :]
    vc = v_ref[0, 0:lc, :]
    lane = lax.broadcasted_iota(jnp.int32, (nq, LANES), 1)
    lo_half = lane < NA_HEAD_DIM
    key_lane = lax.broadcasted_iota(jnp.int32, (GRID_W, nwin), 1)
    half = NA_KH // 2

    def one_block(i):
        wr0 = jnp.clip(i * NA_BLK_ROWS - half, 0, rows - NA_WIN_ROWS)
        first, last = i == 0, i == nblk - 1
        delta0 = jnp.where(first, NA_KH - 1, jnp.where(last, -1, NA_KH - 1 - half))
        qoff = pl.multiple_of(lc + i * nq, GRID_W * NA_KH // 2)
        woff = pl.multiple_of(lc + wr0 * GRID_W, GRID_W * NA_KH // 2)
        q = q_ref[0, pl.ds(qoff, nq), :]
        kw = k_ref[0, pl.ds(woff, nwin), :]
        vw = v_ref[0, pl.ds(woff, nwin), :]
        zq = jnp.zeros_like(q)
        outs = []
        for e in range(2):
            qe = jnp.where(lo_half, q, zq) if e == 0 else jnp.where(lo_half, zq, q)
            s_all = _dot_nt(qe, kw)
            slabs = []
            for j in range(NA_BLK_ROWS):
                a = jnp.where(first, max(j - half, 0), jnp.where(last, min(j + half, NA_KH), j))
                valid = (key_lane >= a * GRID_W) & (key_lane < (a + NA_KH) * GRID_W)
                bias = jnp.concatenate([bias_ref[e, delta0 - j + NA_KH + 2 * m] for m in range(NA_WIN_ROWS // 2)],
                                       axis=1)
                slabs.append(jnp.where(valid, s_all[j * GRID_W:(j + 1) * GRID_W] + bias, NEG_BIG))
            s_loc = jnp.concatenate(slabs, axis=0)
            s_ctx = _dot_nt(qe, kc)
            m = jnp.maximum(jnp.max(s_loc, axis=-1, keepdims=True), jnp.max(s_ctx, axis=-1, keepdims=True))
            p_loc = jnp.exp2(s_loc - m)
            p_ctx = jnp.exp2(s_ctx - m)
            l = jnp.sum(p_loc, axis=-1, keepdims=True) + jnp.sum(p_ctx, axis=-1, keepdims=True)
            outs.append((_dot(p_loc.astype(BF16), vw) + _dot(p_ctx.astype(BF16), vc)) / l)
        o_ref[0, pl.ds(qoff, nq), :] = jnp.where(lo_half, outs[0], outs[1]).astype(o_ref.dtype)

    def body(ii, carry):
        for j in range(NA_BLK_UNROLL):
            one_block(ii * NA_BLK_UNROLL + j)
        return carry

    lax.fori_loop(0, nblk // NA_BLK_UNROLL, body, 0)


def _natten_layer(hs, g, sh, sc, gate, w_qkv, rpb, w_out, *, ncb, lc):
    bsz, lt, d = hs.shape
    rows = (lt - lc) // GRID_W
    assert rows >= NA_WIN_ROWS and rows % (NA_BLK_ROWS * NA_BLK_UNROLL) == 0
    npair = NA_HEADS // 2
    col_scale = jnp.concatenate([jnp.full((1, d), (NA_HEAD_DIM ** -0.5) * LOG2E, F32),
                                 jnp.ones((1, 2 * d), F32)], axis=-1)
    qkv = _norm_mod_matmul(hs, g, sh, sc, w_qkv.astype(BF16), col_scale, ncb=ncb, out_dtype=BF16, tn=1024)
    bias = _na_bias_table(rpb)
    o = pl.pallas_call(
        functools.partial(_na_kernel, lc=lc, rows=rows),
        grid=(npair, bsz),
        in_specs=[
            pl.BlockSpec((1, lt, LANES), lambda p, b: (b, 0, p)),
            pl.BlockSpec((1, lt, LANES), lambda p, b: (b, 0, npair + p)),
            pl.BlockSpec((1, lt, LANES), lambda p, b: (b, 0, 2 * npair + p)),
            pl.BlockSpec((2, NA_TABLE_PAIRS, GRID_W, 2 * GRID_W), lambda p, b: (p, 0, 0, 0)),
        ],
        out_specs=pl.BlockSpec((1, lt, LANES), lambda p, b: (b, 0, p)),
        out_shape=jax.ShapeDtypeStruct((bsz, lt, d), BF16),
        compiler_params=_cparams("parallel", "parallel"),
        name="natten",
    )(qkv, qkv, qkv, bias)
    return _matmul_gate_residual(o, w_out.astype(BF16), hs, gate, ncb=ncb)


def _layer_mods(mod_rows, bsz, d):
    mx = mod_rows[:bsz].reshape(bsz, 6, d)
    mc = jnp.broadcast_to(mod_rows[bsz].reshape(1, 6, d), (bsz, 6, d))
    m = jnp.stack([mc, mx], axis=1)
    return [m[:, :, k, None, :] for k in range(6)]


def kernel(x, c, ctx, c_ctx, ada_w, ada_b, norm1_g, norm2_g, mla_w_in, mla_q_norm_g, mla_w_q_up, mla_kv_norm_g, mla_w_kv_up, mla_w_out, s5_w_in, s5_lambda_re, s5_lambda_im, s5_log_dt, s5_b_re, s5_b_im, s5_c_re, s5_c_im, s5_d, s5_w_glu, hg_w_in, hg_lower_bound, hg_norm_g, hg_w_out, na_w_qkv, na_rpb, na_w_out, ffn_w_up, ffn_conv_w, ffn_conv_b, ffn_w_down, final_g):
    bsz, l, d = x.shape
    lc = ctx.shape[1]
    depth = ada_w.shape[0]
    assert lc % TM == 0 and l % TM == 0 and l % GRID_W == 0 and bsz + 1 <= 16
    ncb = lc // TM
    hs = jnp.concatenate([ctx, x], axis=1)
    cond_rows = jnp.zeros((16, d), F32).at[:bsz].set(c).at[bsz].set(c_ctx)
    mod_all = _ada_mod(cond_rows, ada_w, ada_b)
    lb_cum = jnp.cumsum(jax.nn.softmax(hg_lower_bound.astype(F32), axis=0), axis=0)
    lower_bounds = lb_cum - lb_cum[0]
    for i in range(depth):
        kind, j = i % 4, i // 4
        sh1, sc1, g1, sh2, sc2, g2 = _layer_mods(mod_all[i], bsz, d)
        if kind == 0:
            hs = _mla_layer(hs, norm1_g[i], sh1, sc1, g1, mla_w_in[j], mla_q_norm_g[j], mla_w_q_up[j],
                            mla_kv_norm_g[j], mla_w_kv_up[j], mla_w_out[j], ncb=ncb, lc=lc)
        elif kind == 1:
            hs = _s5_layer(hs, norm1_g[i], sh1, sc1, g1, s5_w_in[j], s5_lambda_re[j], s5_lambda_im[j],
                           s5_log_dt[j], s5_b_re[j], s5_b_im[j], s5_c_re[j], s5_c_im[j], s5_d[j], s5_w_glu[j],
                           ncb=ncb, lc=lc)
        elif kind == 2:
            hs = _hgrn2_layer(hs, norm1_g[i], sh1, sc1, g1, hg_w_in[j], lower_bounds[i], hg_norm_g[j],
                              hg_w_out[j], ncb=ncb, lc=lc)
        else:
            hs = _natten_layer(hs, norm1_g[i], sh1, sc1, g1, na_w_qkv[j], na_rpb[j], na_w_out[j],
                               ncb=ncb, lc=lc)
        hs = _conv_ffn(hs, norm2_g[i], sh2, sc2, g2, ffn_w_up[i], ffn_conv_w[i], ffn_conv_b[i],
                       ffn_w_down[i], final_g if i == depth - 1 else None, ncb=ncb)
    return hs
```

```python
import functools
import math

import jax
import jax.numpy as jnp
import numpy as np
from jax import lax
from jax.experimental import pallas as pl
from jax.experimental.pallas import tpu as pltpu

F32 = jnp.float32
BF16 = jnp.bfloat16
HI = lax.Precision.HIGHEST

EPS = 1e-6
GRID_W = 64
ROPE_THETA = 10000.0
LOG2E = math.log2(math.e)

LANES = 128
VMEM_LIMIT_BYTES = 56 * 1024 * 1024
TM = 256

MLA_HEADS = 16
MLA_Q_RANK = 384
MLA_KV_RANK = 256
MLA_NOPE = 64
MLA_ROPE = 32
MLA_V = 64
MLA_QK = MLA_NOPE + MLA_ROPE
MLA_HEADS_PER_STEP = 4

S5_GROUP_CH = 16
S5_STATE = 64
S5_T = 16
S5_SCAN_UNROLL = 4
S5_QGRP = LANES // S5_GROUP_CH

HG_HEAD_DIM = 128
HG_CHUNK = 64
HG_HEADS_PER_STEP = 4

NA_HEADS = 16
NA_HEAD_DIM = 64
NA_KH = 8
NA_KW = 16
NEG_BIG = -1e30
NA_BLK_ROWS = 8
NA_WIN_ROWS = 16
NA_TABLE_PAIRS = 4 * NA_KH - 2
NA_BLK_UNROLL = 2


def _cparams(*sem):
    return pltpu.CompilerParams(dimension_semantics=sem, vmem_limit_bytes=VMEM_LIMIT_BYTES)


def _seg(ncb):
    return lambda i: jnp.where(i >= ncb, 1, 0)


def _norm_mod(x, g, sh, sc):
    ms = jnp.mean(x * x, axis=-1, keepdims=True)
    return (x * lax.rsqrt(ms + EPS) * g) * (1.0 + sc) + sh


def _sigmoid(x):
    return 1.0 / (1.0 + jnp.exp(-x))


def _dot(a, b):
    return jnp.dot(a, b, preferred_element_type=F32)


def _dot_nt(a, b):
    return lax.dot_general(a, b, (((1,), (1,)), ((), ())), preferred_element_type=F32)


def _dot_tn(a, b):
    return lax.dot_general(a, b, (((0,), (0,)), ((), ())), preferred_element_type=F32)


def _ada_kernel(cond_ref, w_ref, b_ref, o_ref):
    cond = cond_ref[...]
    a = (cond * _sigmoid(cond)).astype(BF16)
    o_ref[0] = _dot(a, w_ref[0].astype(BF16)) + b_ref[0]


def _ada_mod(cond_rows, ada_w, ada_b):
    depth, d, n = ada_w.shape
    rows = cond_rows.shape[0]
    tn = 1536
    return pl.pallas_call(
        _ada_kernel,
        grid=(depth, n // tn),
        in_specs=[
            pl.BlockSpec((rows, d), lambda i, j: (0, 0)),
            pl.BlockSpec((1, d, tn), lambda i, j: (i, 0, j)),
            pl.BlockSpec((1, 1, tn), lambda i, j: (i, 0, j)),
        ],
        out_specs=pl.BlockSpec((1, rows, tn), lambda i, j: (i, 0, j)),
        out_shape=jax.ShapeDtypeStruct((depth, rows, n), F32),
        compiler_params=_cparams("parallel", "parallel"),
        name="ada_mod",
    )(cond_rows, ada_w, ada_b.reshape(depth, 1, n))


def _nmm_kernel(x_ref, g_ref, sh_ref, sc_ref, w_ref, cs_ref, *o_refs, tn):
    u = _norm_mod(x_ref[0], g_ref[...], sh_ref[0, 0], sc_ref[0, 0]).astype(BF16)
    start = 0
    for o_ref in o_refs:
        for j in range(o_ref.shape[2] // tn):
            sl = slice(start + j * tn, start + (j + 1) * tn)
            o_ref[0, :, j * tn:(j + 1) * tn] = (_dot(u, w_ref[:, sl]) * cs_ref[:, sl]).astype(o_ref.dtype)
        start += o_ref.shape[2]


def _norm_mod_matmul(hs, g, sh, sc, w, col_scale, *, ncb, out_dtype, tn):
    bsz, lt, d = hs.shape
    n = w.shape[1]
    seg = _seg(ncb)
    multi = isinstance(out_dtype, tuple)
    parts = out_dtype if multi else ((out_dtype, n),)
    assert sum(c for _, c in parts) == n
    outs = pl.pallas_call(
        functools.partial(_nmm_kernel, tn=tn),
        grid=(bsz, lt // TM),
        in_specs=[
            pl.BlockSpec((1, TM, d), lambda b, i: (b, i, 0)),
            pl.BlockSpec((1, d), lambda b, i: (0, 0)),
            pl.BlockSpec((1, 1, 1, d), lambda b, i: (b, seg(i), 0, 0)),
            pl.BlockSpec((1, 1, 1, d), lambda b, i: (b, seg(i), 0, 0)),
            pl.BlockSpec((d, n), lambda b, i: (0, 0)),
            pl.BlockSpec((1, n), lambda b, i: (0, 0)),
        ],
        out_specs=[pl.BlockSpec((1, TM, c), lambda b, i: (b, i, 0)) for _, c in parts],
        out_shape=[jax.ShapeDtypeStruct((bsz, lt, c), dt) for dt, c in parts],
        compiler_params=_cparams("parallel", "parallel"),
        name="norm_mod_matmul",
    )(hs, g.reshape(1, d), sh, sc, w, col_scale)
    return tuple(outs) if multi else outs[0]


def _mgr_kernel(a_ref, w_ref, x_ref, gate_ref, o_ref):
    o_ref[0] = x_ref[0] + gate_ref[0, 0] * _dot(a_ref[0], w_ref[...])


def _matmul_gate_residual(a, w, hs, gate, *, ncb):
    bsz, lt, d = hs.shape
    k = a.shape[2]
    seg = _seg(ncb)
    return pl.pallas_call(
        _mgr_kernel,
        grid=(bsz, lt // TM),
        in_specs=[
            pl.BlockSpec((1, TM, k), lambda b, i: (b, i, 0)),
            pl.BlockSpec((k, d), lambda b, i: (0, 0)),
            pl.BlockSpec((1, TM, d), lambda b, i: (b, i, 0)),
            pl.BlockSpec((1, 1, 1, d), lambda b, i: (b, seg(i), 0, 0)),
        ],
        out_specs=pl.BlockSpec((1, TM, d), lambda b, i: (b, i, 0)),
        out_shape=jax.ShapeDtypeStruct((bsz, lt, d), F32),
        compiler_params=_cparams("parallel", "parallel"),
        name="matmul_gate_residual",
    )(a, w, hs, gate)


FFN_HALO = 16
FFN_PAD = 8


def _ffn_kernel(x_ref, xp_ref, xn_ref, g_ref, sh_ref, sc_ref, gate_ref, wup_ref, cw_ref, cb_ref,
                wdn_ref, fg_ref, o_ref, u_scr, h_scr, *, ncb, nb, off, final):
    i = pl.program_id(1) + off
    g = g_ref[...]
    sh = sh_ref[0, 0]
    sc = sc_ref[0, 0]
    f = wdn_ref.shape[0]
    keep_prev = jnp.where((i == 0) | (i == ncb), 0.0, 1.0)
    keep_next = jnp.where((i == ncb - 1) | (i == nb - 1), 0.0, 1.0)
    u_prev = _norm_mod(xp_ref[0], g, sh, sc)[FFN_HALO - 1:FFN_HALO] * keep_prev
    u_next = _norm_mod(xn_ref[0], g, sh, sc)[0:1] * keep_next
    row = lax.broadcasted_iota(jnp.int32, (FFN_HALO, u_prev.shape[1]), 0)
    extra = jnp.where(row == 0, u_prev, jnp.where(row == 1, u_next, 0.0))
    u_scr[0:TM, :] = _norm_mod(x_ref[0], g, sh, sc).astype(BF16)
    u_scr[TM:, :] = extra.astype(BF16)
    h = _dot(u_scr[...], wup_ref[...])
    h_scr[FFN_PAD:FFN_PAD + TM, :] = h[0:TM]
    h_scr[FFN_PAD - 1:FFN_PAD, :] = h[TM:TM + 1]
    h_scr[FFN_PAD + TM:FFN_PAD + TM + 1, :] = h[TM + 1:TM + 2]
    w = cw_ref[...]
    hc = (h_scr[FFN_PAD - 1:FFN_PAD - 1 + TM, :] * w[0:1] + h_scr[FFN_PAD:FFN_PAD + TM, :] * w[1:2]
          + h_scr[FFN_PAD + 1:FFN_PAD + 1 + TM, :] * w[2:3] + cb_ref[...])
    gg = hc[:, f:]
    act = (hc[:, :f] * (gg * _sigmoid(gg))).astype(BF16)
    y = x_ref[0] + gate_ref[0, 0] * _dot(act, wdn_ref[...])
    if final:
        y = y * lax.rsqrt(jnp.mean(y * y, axis=-1, keepdims=True) + EPS) * fg_ref[...]
    o_ref[0] = y


def _conv_ffn(hs, g, sh, sc, gate, w_up, conv_w, conv_b, w_down, final_g=None, *, ncb):
    bsz, lt, d = hs.shape
    f = w_down.shape[0]
    nb = lt // TM
    final = final_g is not None
    off = ncb if final else 0
    hb = TM // FFN_HALO
    nhalo = lt // FFN_HALO
    seg = _seg(ncb)
    fg = (final_g if final else jnp.ones((d,), F32)).reshape(1, d)
    return pl.pallas_call(
        functools.partial(_ffn_kernel, ncb=ncb, nb=nb, off=off, final=final),
        grid=(bsz, nb - off),
        in_specs=[
            pl.BlockSpec((1, TM, d), lambda b, i: (b, i + off, 0)),
            pl.BlockSpec((1, FFN_HALO, d), lambda b, i: (b, jnp.maximum((i + off) * hb - 1, 0), 0)),
            pl.BlockSpec((1, FFN_HALO, d), lambda b, i: (b, jnp.minimum((i + off + 1) * hb, nhalo - 1), 0)),
            pl.BlockSpec((1, d), lambda b, i: (0, 0)),
            pl.BlockSpec((1, 1, 1, d), lambda b, i: (b, seg(i + off), 0, 0)),
            pl.BlockSpec((1, 1, 1, d), lambda b, i: (b, seg(i + off), 0, 0)),
            pl.BlockSpec((1, 1, 1, d), lambda b, i: (b, seg(i + off), 0, 0)),
            pl.BlockSpec((d, 2 * f), lambda b, i: (0, 0)),
            pl.BlockSpec((3, 2 * f), lambda b, i: (0, 0)),
            pl.BlockSpec((1, 2 * f), lambda b, i: (0, 0)),
            pl.BlockSpec((f, d), lambda b, i: (0, 0)),
            pl.BlockSpec((1, d), lambda b, i: (0, 0)),
        ],
        out_specs=pl.BlockSpec((1, TM, d), lambda b, i: (b, i, 0)),
        out_shape=jax.ShapeDtypeStruct((bsz, lt - off * TM, d), F32),
        scratch_shapes=[
            pltpu.VMEM((TM + FFN_HALO, d), BF16),
            pltpu.VMEM((TM + 2 * FFN_PAD, 2 * f), F32),
        ],
        compiler_params=_cparams("parallel", "parallel"),
        name="conv_ffn",
    )(hs, hs, hs, g.reshape(1, d), sh, sc, gate, w_up.astype(BF16), conv_w, conv_b.reshape(1, 2 * f),
      w_down.astype(BF16), fg)


def _rope_tables(lc, l):
    half = MLA_ROPE // 4
    inv = 1.0 / (ROPE_THETA ** (jnp.arange(half, dtype=F32) / half))
    t = jnp.arange(l)
    ang_r = (t // GRID_W).astype(F32)[:, None] * inv[None, :]
    ang_c = (t % GRID_W).astype(F32)[:, None] * inv[None, :]
    cos = jnp.concatenate([jnp.cos(ang_r), jnp.cos(ang_r), jnp.cos(ang_c), jnp.cos(ang_c)], axis=-1)
    sin = jnp.concatenate([-jnp.sin(ang_r), jnp.sin(ang_r), -jnp.sin(ang_c), jnp.sin(ang_c)], axis=-1)
    cos_t = jnp.ones((lc + l, LANES), F32).at[lc:, MLA_NOPE:MLA_NOPE + MLA_ROPE].set(cos)
    sin_t = jnp.zeros((lc + l, LANES), F32).at[lc:, MLA_NOPE:MLA_NOPE + MLA_ROPE].set(sin)
    return cos_t, sin_t


def _mla_proj_kernel(x_ref, g_ref, sh_ref, sc_ref, win_ref, qg_ref, wq_ref, kvg_ref, wkn_ref, wv_ref,
                     cos_ref, sin_ref, q_ref, k_ref, v_ref):
    u = _norm_mod(x_ref[0], g_ref[...], sh_ref[0, 0], sc_ref[0, 0]).astype(BF16)
    lat = _dot(u, win_ref[...])
    q_lat = lat[:, :MLA_Q_RANK]
    kv_lat = lat[:, MLA_Q_RANK:MLA_Q_RANK + MLA_KV_RANK]
    kr = lat[:, MLA_Q_RANK + MLA_KV_RANK:]

    def rms(t, gg):
        return (t * lax.rsqrt(jnp.mean(t * t, axis=-1, keepdims=True) + EPS) * gg).astype(BF16)

    qn = rms(q_lat, qg_ref[...])
    kvn = rms(kv_lat, kvg_ref[...])
    cos = cos_ref[...]
    sin = sin_ref[...]
    lane = lax.broadcasted_iota(jnp.int32, (TM, LANES), 1)
    first_half = (lane % (MLA_ROPE // 2)) < (MLA_ROPE // 4)

    def rope(t):
        partner = jnp.where(first_half, pltpu.roll(t, LANES - MLA_ROPE // 4, 1), pltpu.roll(t, MLA_ROPE // 4, 1))
        return t * cos + partner * sin

    kr_rot = rope(kr)
    v_ref[0] = _dot(kvn, wv_ref[...]).astype(BF16)
    q_all = _dot(qn, wq_ref[...]) * ((MLA_QK ** -0.5) * LOG2E)
    kn_all = _dot(kvn, wkn_ref[...])
    for h in range(MLA_HEADS):
        sl = slice(h * LANES, (h + 1) * LANES)
        q_ref[0, :, sl] = rope(q_all[:, sl]).astype(BF16)
        k_ref[0, :, sl] = (kn_all[:, sl] + kr_rot).astype(BF16)


def _mla_attn_kernel(q_ref, k_ref, v_ref, o_ref, *, ncb, lc):
    i = pl.program_id(2)

    def attend(lk):
        q = q_ref[0]
        lane = lax.broadcasted_iota(jnp.int32, (q.shape[0], LANES), 1)
        for pair in range(q.shape[1] // (2 * LANES)):
            v = v_ref[0, 0:lk, pair * LANES:(pair + 1) * LANES]
            outs = []
            for e in range(2):
                sl = slice((2 * pair + e) * LANES, (2 * pair + e + 1) * LANES)
                s = _dot_nt(q[:, sl], k_ref[0, 0:lk, sl])
                m = jnp.max(s, axis=-1, keepdims=True)
                p = jnp.exp2(s - m)
                l = jnp.sum(p, axis=-1, keepdims=True)
                outs.append(_dot(p.astype(BF16), v) / l)
            o_ref[0, :, pair * LANES:(pair + 1) * LANES] = jnp.where(lane < MLA_V, outs[0], outs[1]).astype(o_ref.dtype)

    @pl.when(i < ncb)
    def _():
        attend(lc)

    @pl.when(i >= ncb)
    def _():
        attend(k_ref.shape[1])


def _mla_layer(hs, g, sh, sc, gate, w_in, q_norm_g, w_q_up, kv_norm_g, w_kv_up, w_out, *, ncb, lc):
    bsz, lt, d = hs.shape
    l = lt - lc
    nh = MLA_HEADS
    seg = _seg(ncb)
    w_in_p = jnp.zeros((d, 768), F32)
    w_in_p = w_in_p.at[:, :MLA_Q_RANK + MLA_KV_RANK].set(w_in[:, :MLA_Q_RANK + MLA_KV_RANK])
    w_in_p = w_in_p.at[:, 640 + MLA_NOPE:640 + MLA_QK].set(w_in[:, MLA_Q_RANK + MLA_KV_RANK:])
    wq = w_q_up.reshape(MLA_Q_RANK, nh, MLA_QK)
    wq = jnp.pad(wq, ((0, 0), (0, 0), (0, LANES - MLA_QK))).reshape(MLA_Q_RANK, nh * LANES)
    wkv = w_kv_up.reshape(MLA_KV_RANK, nh, MLA_NOPE + MLA_V)
    wkn = jnp.pad(wkv[:, :, :MLA_NOPE], ((0, 0), (0, 0), (0, LANES - MLA_NOPE))).reshape(MLA_KV_RANK, nh * LANES)
    wv = wkv[:, :, MLA_NOPE:].reshape(MLA_KV_RANK, nh * MLA_V)
    cos_t, sin_t = _rope_tables(lc, l)

    q, k, v = pl.pallas_call(
        _mla_proj_kernel,
        grid=(bsz, lt // TM),
        in_specs=[
            pl.BlockSpec((1, TM, d), lambda b, i: (b, i, 0)),
            pl.BlockSpec((1, d), lambda b, i: (0, 0)),
            pl.BlockSpec((1, 1, 1, d), lambda b, i: (b, seg(i), 0, 0)),
            pl.BlockSpec((1, 1, 1, d), lambda b, i: (b, seg(i), 0, 0)),
            pl.BlockSpec((d, 768), lambda b, i: (0, 0)),
            pl.BlockSpec((1, MLA_Q_RANK), lambda b, i: (0, 0)),
            pl.BlockSpec((MLA_Q_RANK, nh * LANES), lambda b, i: (0, 0)),
            pl.BlockSpec((1, MLA_KV_RANK), lambda b, i: (0, 0)),
            pl.BlockSpec((MLA_KV_RANK, nh * LANES), lambda b, i: (0, 0)),
            pl.BlockSpec((MLA_KV_RANK, nh * MLA_V), lambda b, i: (0, 0)),
            pl.BlockSpec((TM, LANES), lambda b, i: (i, 0)),
            pl.BlockSpec((TM, LANES), lambda b, i: (i, 0)),
        ],
        out_specs=[
            pl.BlockSpec((1, TM, nh * LANES), lambda b, i: (b, i, 0)),
            pl.BlockSpec((1, TM, nh * LANES), lambda b, i: (b, i, 0)),
            pl.BlockSpec((1, TM, nh * MLA_V), lambda b, i: (b, i, 0)),
        ],
        out_shape=[
            jax.ShapeDtypeStruct((bsz, lt, nh * LANES), BF16),
            jax.ShapeDtypeStruct((bsz, lt, nh * LANES), BF16),
            jax.ShapeDtypeStruct((bsz, lt, nh * MLA_V), BF16),
        ],
        compiler_params=_cparams("parallel", "parallel"),
        name="mla_proj",
    )(hs, g.reshape(1, d), sh, sc, w_in_p.astype(BF16), q_norm_g.reshape(1, -1), wq.astype(BF16),
      kv_norm_g.reshape(1, -1), wkn.astype(BF16), wv.astype(BF16), cos_t, sin_t)

    o = pl.pallas_call(
        functools.partial(_mla_attn_kernel, ncb=ncb, lc=lc),
        grid=(bsz, nh // MLA_HEADS_PER_STEP, lt // TM),
        in_specs=[
            pl.BlockSpec((1, TM, MLA_HEADS_PER_STEP * LANES), lambda b, p, i: (b, i, p)),
            pl.BlockSpec((1, lt, MLA_HEADS_PER_STEP * LANES), lambda b, p, i: (b, 0, p)),
            pl.BlockSpec((1, lt, MLA_HEADS_PER_STEP * MLA_V), lambda b, p, i: (b, 0, p)),
        ],
        out_specs=pl.BlockSpec((1, TM, MLA_HEADS_PER_STEP * MLA_V), lambda b, p, i: (b, i, p)),
        out_shape=jax.ShapeDtypeStruct((bsz, lt, nh * MLA_V), BF16),
        compiler_params=_cparams("parallel", "parallel", "parallel"),
        name="mla_attn",
    )(q, k, v)
    return _matmul_gate_residual(o, w_out.astype(BF16), hs, gate, ncb=ncb)


def _s5_matrices(lam_re, lam_im, log_dt, b_re, b_im, c_re, c_im):
    t_len = S5_T
    n_grp = lam_re.shape[1]
    ch = S5_GROUP_CH
    n_st = S5_STATE
    dt = jnp.exp(log_dt.astype(F32))[..., None]
    ld_re = lam_re.astype(F32) * dt
    ld_im = lam_im.astype(F32) * dt
    tau = jnp.arange(t_len + 1, dtype=F32)[:, None, None, None]
    mag = jnp.exp(tau * ld_re[None])
    pw_re = mag * jnp.cos(tau * ld_im[None])
    pw_im = mag * jnp.sin(tau * ld_im[None])
    tau_r = jnp.arange(t_len, -1, -1, dtype=F32)[:, None, None, None]
    mag_r = jnp.exp(tau_r * ld_re[None])
    pr_re = mag_r * jnp.cos(tau_r * ld_im[None])
    pr_im = mag_r * jnp.sin(tau_r * ld_im[None])
    lb_re, lb_im = pw_re[1] - 1.0, pw_im[1]
    den = lam_re * lam_re + lam_im * lam_im
    f_re = (lb_re * lam_re + lb_im * lam_im) / den
    f_im = (lb_im * lam_re - lb_re * lam_im) / den
    bb_re = f_re[..., None] * b_re - f_im[..., None] * b_im
    bb_im = f_re[..., None] * b_im + f_im[..., None] * b_re
    cp_re = c_re[None] * pw_re[:t_len, :, :, None, :] - c_im[None] * pw_im[:t_len, :, :, None, :]
    cp_im = c_re[None] * pw_im[:t_len, :, :, None, :] + c_im[None] * pw_re[:t_len, :, :, None, :]
    taps = (jnp.einsum("tdgon,dgni->tdgoi", cp_re, bb_re, precision=HI)
            - jnp.einsum("tdgon,dgni->tdgoi", cp_im, bb_im, precision=HI))
    s_idx = np.arange(t_len)[:, None, None]
    t_idx = np.arange(t_len)[None, :, None]
    u_idx = np.arange(t_len)[None, None, :]
    sel_f = jnp.asarray((t_idx - s_idx == u_idx).astype(np.float32))
    sel_b = jnp.asarray((s_idx - t_idx == u_idx).astype(np.float32))
    kf = jnp.einsum("stu,ugoi->stgoi", sel_f, taps[:, 0], precision=HI)
    kb = jnp.einsum("stu,ugoi->stgoi", sel_b, taps[:, 1], precision=HI)
    k_tot = (kf + kb).transpose(2, 0, 4, 1, 3).reshape(n_grp, t_len * ch, t_len * ch)

    def state_in(pw_r, pw_i, brr, bii):
        re = pw_r[..., None] * brr[None] - pw_i[..., None] * bii[None]
        im = pw_r[..., None] * bii[None] + pw_i[..., None] * brr[None]
        return (re.transpose(1, 0, 3, 2).reshape(n_grp, t_len * ch, n_st),
                im.transpose(1, 0, 3, 2).reshape(n_grp, t_len * ch, n_st))

    inf_re, inf_im = state_in(pr_re[1:, 0], pr_im[1:, 0], bb_re[0], bb_im[0])
    inb_re, inb_im = state_in(pw_re[:t_len, 1], pw_im[:t_len, 1], bb_re[1], bb_im[1])

    def state_out(pw_r, pw_i, crr, cii):
        re = crr[None] * pw_r[:, :, None, :] - cii[None] * pw_i[:, :, None, :]
        im = crr[None] * pw_i[:, :, None, :] + cii[None] * pw_r[:, :, None, :]
        return (re.transpose(1, 3, 0, 2).reshape(n_grp, n_st, t_len * ch),
                (-im).transpose(1, 3, 0, 2).reshape(n_grp, n_st, t_len * ch))

    outf_re, outf_im = state_out(pw_re[1:, 0], pw_im[1:, 0], c_re[0], c_im[0])
    outb_re, outb_im = state_out(pr_re[:t_len, 1], pr_im[:t_len, 1], c_re[1], c_im[1])
    nq = n_grp // S5_QGRP
    npair = n_grp // 2
    cols = t_len * ch
    x = jnp.stack([inf_re, inf_im, inb_re, inb_im], 0).reshape(4, npair, 2, cols, n_st)
    x = x.transpose(1, 2, 3, 0, 4)
    zx = jnp.zeros_like(x[:, 0])
    w_pair = jnp.concatenate([jnp.concatenate([x[:, 0], zx], -1), jnp.concatenate([zx, x[:, 1]], -1)], 1)
    w_pair = w_pair.reshape(nq, S5_QGRP // 2, 2 * cols, 4 * 2 * n_st)
    y = jnp.stack([outf_re, outf_im, outb_re, outb_im], 0).reshape(4, npair, 2, n_st, cols)
    y = y.transpose(1, 0, 2, 3, 4)
    zy = jnp.zeros_like(y[:, :, 0])
    m_pair = jnp.stack([jnp.concatenate([y[:, :, 0], zy], -1), jnp.concatenate([zy, y[:, :, 1]], -1)], 2)
    m_pair = m_pair.reshape(nq, S5_QGRP // 2, 4 * 2 * n_st, 2 * cols)
    decay = jnp.stack([pw_re[t_len, 0], pw_im[t_len, 0], pw_re[t_len, 1], pw_im[t_len, 1]], 0)
    decay = decay.reshape(4, nq, S5_QGRP * n_st).transpose(1, 0, 2).reshape(nq, 1, 4 * S5_QGRP * n_st)
    k_grp = k_tot.reshape(nq, S5_QGRP, cols, cols)
    return w_pair.astype(BF16), k_grp.astype(BF16), m_pair.astype(BF16), decay


def _s5_permutation():
    n = S5_T * LANES
    src = jnp.arange(n)
    t, g, c = src // LANES, (src % LANES) // S5_GROUP_CH, src % S5_GROUP_CH
    dst = g * (S5_T * S5_GROUP_CH) + t * S5_GROUP_CH + c
    return (dst[:, None] == jnp.arange(n)[None, :]).astype(BF16)


def _s5_chunk_rows(z_ref):
    jt = z_ref.shape[1] // S5_T
    return jnp.concatenate([z_ref[0, pl.ds(t, jt, stride=S5_T), :].astype(BF16) for t in range(S5_T)], axis=1)


def _s5_core_kernel(z_ref, perm_ref, win_ref, kg_ref, mout_ref, a_ref, y_ref, u_scr, p_scr, *, jc, jt):
    sc = a_ref.shape[2] // 4
    npair = win_ref.shape[1]
    pc = win_ref.shape[2]
    gc = pc // 2
    zg = _dot(_s5_chunk_rows(z_ref), perm_ref[...]).astype(BF16)
    for pp in range(npair):
        res = _dot(zg[:, pp * pc:(pp + 1) * pc], win_ref[0, pp])
        for k in range(4):
            u_scr[:, k * sc + pp * LANES:k * sc + (pp + 1) * LANES] = res[:, k * LANES:(k + 1) * LANES]
    a = a_ref[0]
    afr, afi, abr, abi = (a[:, k * sc:(k + 1) * sc] for k in range(4))
    zero = jnp.zeros((1, sc), F32)

    def body(jj, carry):
        fr, fi, br, bi = carry
        jb = jnp.where(jj < jc, jc - 1 - jj, jt - 1 - jj + jc)
        p_scr[pl.ds(jj, 1), 0:sc] = fr
        p_scr[pl.ds(jj, 1), sc:2 * sc] = fi
        p_scr[pl.ds(jb, 1), 2 * sc:3 * sc] = br
        p_scr[pl.ds(jb, 1), 3 * sc:4 * sc] = bi
        nfr = afr * fr - afi * fi + u_scr[pl.ds(jj, 1), 0:sc]
        nfi = afr * fi + afi * fr + u_scr[pl.ds(jj, 1), sc:2 * sc]
        nbr = abr * br - abi * bi + u_scr[pl.ds(jb, 1), 2 * sc:3 * sc]
        nbi = abr * bi + abi * br + u_scr[pl.ds(jb, 1), 3 * sc:4 * sc]
        return nfr, nfi, nbr, nbi

    lax.fori_loop(0, jt, body, (zero, zero, zero, zero), unroll=S5_SCAN_UNROLL)
    pieces = []
    for pp in range(npair):
        p_in = jnp.concatenate([p_scr[:, k * sc + pp * LANES:k * sc + (pp + 1) * LANES] for k in range(4)], axis=1)
        ys = _dot(p_in.astype(BF16), mout_ref[0, pp])
        for e in range(2):
            g = 2 * pp + e
            yg = ys[:, e * gc:(e + 1) * gc] + _dot(zg[:, g * gc:(g + 1) * gc], kg_ref[0, g])
            pieces.append(yg.astype(BF16))
    acc = _dot_nt(jnp.concatenate(pieces, axis=1), perm_ref[...])
    for t in range(S5_T):
        y_ref[0, pl.ds(t, jt, stride=S5_T), :] = acc[:, t * LANES:(t + 1) * LANES]


def _s5_glu_kernel(y_ref, z_ref, d_ref, w_ref, x_ref, gate_ref, o_ref):
    d = x_ref.shape[2]
    y = y_ref[0] + z_ref[0] * d_ref[...]
    ge = jax.nn.gelu(y).astype(BF16)
    ag = _dot(ge, w_ref[...])
    o_ref[0] = x_ref[0] + gate_ref[0, 0] * (ag[:, :d] * _sigmoid(ag[:, d:]))


def _s5_layer(hs, g, sh, sc, gate, w_in, lam_re, lam_im, log_dt, b_re, b_im, c_re, c_im, d_skip, w_glu,
              *, ncb, lc):
    bsz, lt, d = hs.shape
    width = w_in.shape[1]
    nq = width // LANES
    jt = lt // S5_T
    jc = lc // S5_T
    ccols = S5_T * LANES
    scols = 4 * S5_QGRP * S5_STATE
    seg = _seg(ncb)
    z = _norm_mod_matmul(hs, g, sh, sc, w_in.astype(BF16), jnp.ones((1, width), F32),
                         ncb=ncb, out_dtype=F32, tn=width)
    w_pair, k_grp, m_pair, decay = _s5_matrices(lam_re, lam_im, log_dt, b_re, b_im, c_re, c_im)

    def per_quarter(arr):
        return pl.BlockSpec((1,) + arr.shape[1:], lambda q, b: (q, 0, 0, 0))

    y = pl.pallas_call(
        functools.partial(_s5_core_kernel, jc=jc, jt=jt),
        grid=(nq, bsz),
        in_specs=[
            pl.BlockSpec((1, lt, LANES), lambda q, b: (b, 0, q)),
            pl.BlockSpec((ccols, ccols), lambda q, b: (0, 0), pipeline_mode=pl.Buffered(1)),
            per_quarter(w_pair),
            per_quarter(k_grp),
            per_quarter(m_pair),
            pl.BlockSpec((1, 1, scols), lambda q, b: (q, 0, 0)),
        ],
        out_specs=pl.BlockSpec((1, lt, LANES), lambda q, b: (b, 0, q)),
        out_shape=jax.ShapeDtypeStruct((bsz, lt, width), F32),
        scratch_shapes=[pltpu.VMEM((jt, scols), F32), pltpu.VMEM((jt, scols), F32)],
        compiler_params=_cparams("parallel", "parallel"),
        name="s5_core",
    )(z, _s5_permutation(), w_pair, k_grp, m_pair, decay)
    return pl.pallas_call(
        _s5_glu_kernel,
        grid=(bsz, lt // TM),
        in_specs=[
            pl.BlockSpec((1, TM, width), lambda b, i: (b, i, 0)),
            pl.BlockSpec((1, TM, width), lambda b, i: (b, i, 0)),
            pl.BlockSpec((1, width), lambda b, i: (0, 0)),
            pl.BlockSpec((width, 2 * d), lambda b, i: (0, 0)),
            pl.BlockSpec((1, TM, d), lambda b, i: (b, i, 0)),
            pl.BlockSpec((1, 1, 1, d), lambda b, i: (b, seg(i), 0, 0)),
        ],
        out_specs=pl.BlockSpec((1, TM, d), lambda b, i: (b, i, 0)),
        out_shape=jax.ShapeDtypeStruct((bsz, lt, d), F32),
        compiler_params=_cparams("parallel", "parallel"),
        name="s5_glu",
    )(y, z, d_skip.reshape(1, width), w_glu.astype(BF16), hs, gate)


def _gla_dir(zq, zf, zv, lb, states, incl, tri, rev):
    dk = HG_HEAD_DIM
    width = zq.shape[1]
    nchunks = zq.shape[0] // HG_CHUNK
    forget = lb + (1.0 - lb) * _sigmoid(zf)
    lf = jnp.log(forget)
    kk = 1.0 - forget
    v_all = zv.astype(BF16)
    hi = lf.astype(BF16)
    r1 = lf - hi.astype(F32)
    mid = r1.astype(BF16)
    lo = (r1 - mid.astype(F32)).astype(BF16)
    parts = _dot(tri, jnp.concatenate([hi, mid, lo], axis=1))
    bcum = parts[:, :width] + parts[:, width:2 * width] + parts[:, 2 * width:]
    btot = jnp.concatenate(
        [jnp.broadcast_to(bcum[c * HG_CHUNK:c * HG_CHUNK + 1] if rev else bcum[(c + 1) * HG_CHUNK - 1:(c + 1) * HG_CHUNK],
                          (HG_CHUNK, width)) for c in range(nchunks)], axis=0)
    q_in_all = (zq * _sigmoid(zq)) * (dk ** -0.5) * jnp.exp(bcum)
    k_in_all = (kk * jnp.exp(-bcum)).astype(BF16)
    k_out_all = kk * jnp.exp(btot - bcum)
    chunk_of_row = lax.broadcasted_iota(jnp.int32, (zq.shape[0], dk), 0) // HG_CHUNK
    zero = jnp.zeros((zq.shape[0], dk), F32)
    outs, new_states = [], []
    for h, st in enumerate(states):
        sl = slice(h * dk, (h + 1) * dk)
        q_in, k_out, v = q_in_all[:, sl], k_out_all[:, sl], v_all[:, sl]
        att = jnp.where(incl, _dot_nt(q_in.astype(BF16), k_in_all[:, sl]), 0.0)
        o_intra = _dot(att.astype(BF16), v)
        k_cat = jnp.concatenate([jnp.where(chunk_of_row == c, k_out, zero) for c in range(nchunks)], axis=1)
        q_cat = jnp.concatenate([jnp.where(chunk_of_row == c, q_in, zero) for c in range(nchunks)], axis=1)
        ds = _dot_tn(v, k_cat.astype(BF16))
        entering = [None] * nchunks
        for cc in range(nchunks):
            c = nchunks - 1 - cc if rev else cc
            entering[c] = st
            st = st * jnp.exp(btot[c * HG_CHUNK:c * HG_CHUNK + 1, sl]) + ds[:, c * dk:(c + 1) * dk]
        scat = jnp.concatenate(entering, axis=1).astype(BF16)
        outs.append(o_intra + _dot_nt(q_cat.astype(BF16), scat))
        new_states.append(st)
    return outs, new_states


def _gla_kernel(zqf_ref, zff_ref, zvf_ref, zqb_ref, zfb_ref, zvb_ref, lb_ref, of_ref, ob_ref, st_scr):
    @pl.when(pl.program_id(2) == 0)
    def _():
        st_scr[...] = jnp.zeros_like(st_scr)

    n = zqf_ref.shape[1]
    row = lax.broadcasted_iota(jnp.int32, (n, n), 0)
    col = lax.broadcasted_iota(jnp.int32, (n, n), 1)
    same = (row // HG_CHUNK) == (col // HG_CHUNK)
    dk = HG_HEAD_DIM
    for direction, (zq_ref, zf_ref, zv_ref, o_ref) in enumerate(
            ((zqf_ref, zff_ref, zvf_ref, of_ref), (zqb_ref, zfb_ref, zvb_ref, ob_ref))):
        rev = direction == 1
        incl = same & ((col >= row) if rev else (col <= row))
        tri = incl.astype(BF16)
        nheads = zq_ref.shape[2] // dk
        outs, states = _gla_dir(zq_ref[0].astype(F32), zf_ref[0], zv_ref[0], lb_ref[0],
                                [st_scr[direction, h] for h in range(nheads)], incl, tri, rev)
        for h in range(nheads):
            o_ref[0, :, h * dk:(h + 1) * dk] = outs[h]
            st_scr[direction, h] = states[h]


def _hg_out_kernel(of_ref, ob_ref, zg_ref, ng_ref, w_ref, x_ref, gate_ref, o_ref):
    o = of_ref[0] + ob_ref[0]
    gsig = zg_ref[0].astype(F32)
    gsig = gsig * _sigmoid(gsig)
    ng = ng_ref[...]
    parts = []
    for h in range(o.shape[1] // HG_HEAD_DIM):
        sl = slice(h * HG_HEAD_DIM, (h + 1) * HG_HEAD_DIM)
        oh = o[:, sl]
        on = oh * lax.rsqrt(jnp.mean(oh * oh, axis=-1, keepdims=True) + EPS)
        parts.append((on * ng[:, sl] * gsig[:, sl]).astype(BF16))
    a = jnp.concatenate(parts, axis=-1)
    o_ref[0] = x_ref[0] + gate_ref[0, 0] * _dot(a, w_ref[...])


def _hgrn2_layer(hs, g, sh, sc, gate, w_in, lower_bound, norm_g, w_out, *, ncb, lc):
    bsz, lt, d = hs.shape
    nh = d // HG_HEAD_DIM
    nb = lt // TM
    seg = _seg(ncb)
    w_r = jnp.concatenate([w_in[:, d:3 * d], w_in[:, :d], w_in[:, 3 * d:]], axis=1).astype(BF16)
    zf, zr = _norm_mod_matmul(hs, g, sh, sc, w_r, jnp.ones((1, 5 * d), F32),
                              ncb=ncb, out_dtype=((F32, 2 * d), (BF16, 3 * d)), tn=1024)
    lb = lower_bound.astype(F32).reshape(1, 1, d)
    hps = HG_HEADS_PER_STEP
    ng = nh // hps
    wb = hps * HG_HEAD_DIM

    def rblk(s):
        return jnp.where(s < ncb, ncb - 1 - s, nb - 1 - s + ncb)

    outs = pl.pallas_call(
        _gla_kernel,
        grid=(bsz, ng, nb),
        in_specs=[
            pl.BlockSpec((1, TM, wb), lambda b, h, s: (b, s, h)),
            pl.BlockSpec((1, TM, wb), lambda b, h, s: (b, s, h)),
            pl.BlockSpec((1, TM, wb), lambda b, h, s: (b, s, ng + h)),
            pl.BlockSpec((1, TM, wb), lambda b, h, s: (b, rblk(s), h)),
            pl.BlockSpec((1, TM, wb), lambda b, h, s: (b, rblk(s), ng + h)),
            pl.BlockSpec((1, TM, wb), lambda b, h, s: (b, rblk(s), ng + h)),
            pl.BlockSpec((1, 1, wb), lambda b, h, s: (0, 0, h)),
        ],
        out_specs=[
            pl.BlockSpec((1, TM, wb), lambda b, h, s: (b, s, h)),
            pl.BlockSpec((1, TM, wb), lambda b, h, s: (b, rblk(s), h)),
        ],
        out_shape=[jax.ShapeDtypeStruct((bsz, lt, d), F32), jax.ShapeDtypeStruct((bsz, lt, d), F32)],
        scratch_shapes=[pltpu.VMEM((2, hps, HG_HEAD_DIM, HG_HEAD_DIM), F32)],
        compiler_params=_cparams("parallel", "parallel", "arbitrary"),
        name="hgrn2_gla",
    )(zr, zf, zr, zr, zf, zr, lb)
    return pl.pallas_call(
        _hg_out_kernel,
        grid=(bsz, nb),
        in_specs=[
            pl.BlockSpec((1, TM, d), lambda b, i: (b, i, 0)),
            pl.BlockSpec((1, TM, d), lambda b, i: (b, i, 0)),
            pl.BlockSpec((1, TM, d), lambda b, i: (b, i, 2)),
            pl.BlockSpec((1, d), lambda b, i: (0, 0)),
            pl.BlockSpec((d, d), lambda b, i: (0, 0)),
            pl.BlockSpec((1, TM, d), lambda b, i: (b, i, 0)),
            pl.BlockSpec((1, 1, 1, d), lambda b, i: (b, seg(i), 0, 0)),
        ],
        out_specs=pl.BlockSpec((1, TM, d), lambda b, i: (b, i, 0)),
        out_shape=jax.ShapeDtypeStruct((bsz, lt, d), F32),
        compiler_params=_cparams("parallel", "parallel"),
        name="hgrn2_out",
    )(outs[0], outs[1], zr, norm_g.reshape(1, d), w_out.astype(BF16), hs, gate)


def _na_bias_table(rpb):
    w = np.arange(GRID_W)[:, None, None]
    kc = np.arange(GRID_W)[None, :, None]
    co = np.arange(2 * NA_KW - 1)[None, None, :]
    c0 = np.clip(w - NA_KW // 2, 0, GRID_W - NA_KW)
    valid = (kc >= c0) & (kc < c0 + NA_KW)
    onehot = jnp.asarray((valid & (kc - w + (NA_KW - 1) == co)).astype(np.float32))
    t = jnp.einsum("hrc,wkc->hrwk", rpb.astype(F32), onehot, precision=HI) * LOG2E
    t = jnp.where(jnp.asarray(valid[None, None, :, :, 0]), t, NEG_BIG)
    ext = jnp.pad(t, ((0, 0), (NA_KH, NA_KH), (0, 0), (0, 0)), constant_values=NEG_BIG)
    return jnp.concatenate([ext[:, :-1], ext[:, 1:]], axis=-1)


def _na_kernel(q_ref, k_ref, v_ref, bias_ref, o_ref, *, lc, rows):
    nblk = rows // NA_BLK_ROWS
    nq = NA_BLK_ROWS * GRID_W
    nwin = NA_WIN_ROWS * GRID_W
    o_ref[0, 0:lc, :] = jnp.zeros((lc, LANES), o_ref.dtype)
    kc = k_ref[0, 0:lc, :]
    vc = v_ref[0, 0:lc, :]
    lane = lax.broadcasted_iota(jnp.int32, (nq, LANES), 1)
    lo_half = lane < NA_HEAD_DIM
    key_lane = lax.broadcasted_iota(jnp.int32, (GRID_W, nwin), 1)
    half = NA_KH // 2

    def one_block(i):
        wr0 = jnp.clip(i * NA_BLK_ROWS - half, 0, rows - NA_WIN_ROWS)
        first, last = i == 0, i == nblk - 1
        delta0 = jnp.where(first, NA_KH - 1, jnp.where(last, -1, NA_KH - 1 - half))
        qoff = pl.multiple_of(lc + i * nq, GRID_W * NA_KH // 2)
        woff = pl.multiple_of(lc + wr0 * GRID_W, GRID_W * NA_KH // 2)
        q = q_ref[0, pl.ds(qoff, nq), :]
        kw = k_ref[0, pl.ds(woff, nwin), :]
        vw = v_ref[0, pl.ds(woff, nwin), :]
        zq = jnp.zeros_like(q)
        outs = []
        for e in range(2):
            qe = jnp.where(lo_half, q, zq) if e == 0 else jnp.where(lo_half, zq, q)
            s_all = _dot_nt(qe, kw)
            slabs = []
            for j in range(NA_BLK_ROWS):
                a = jnp.where(first, max(j - half, 0), jnp.where(last, min(j + half, NA_KH), j))
                valid = (key_lane >= a * GRID_W) & (key_lane < (a + NA_KH) * GRID_W)
                bias = jnp.concatenate([bias_ref[e, delta0 - j + NA_KH + 2 * m] for m in range(NA_WIN_ROWS // 2)],
                                       axis=1)
                slabs.append(jnp.where(valid, s_all[j * GRID_W:(j + 1) * GRID_W] + bias, NEG_BIG))
            s_loc = jnp.concatenate(slabs, axis=0)
            s_ctx = _dot_nt(qe, kc)
            m = jnp.maximum(jnp.max(s_loc, axis=-1, keepdims=True), jnp.max(s_ctx, axis=-1, keepdims=True))
            p_loc = jnp.exp2(s_loc - m)
            p_ctx = jnp.exp2(s_ctx - m)
            l = jnp.sum(p_loc, axis=-1, keepdims=True) + jnp.sum(p_ctx, axis=-1, keepdims=True)
            outs.append((_dot(p_loc.astype(BF16), vw) + _dot(p_ctx.astype(BF16), vc)) / l)
        o_ref[0, pl.ds(qoff, nq), :] = jnp.where(lo_half, outs[0], outs[1]).astype(o_ref.dtype)

    def body(ii, carry):
        for j in range(NA_BLK_UNROLL):
            one_block(ii * NA_BLK_UNROLL + j)
        return carry

    lax.fori_loop(0, nblk // NA_BLK_UNROLL, body, 0)


def _natten_layer(hs, g, sh, sc, gate, w_qkv, rpb, w_out, *, ncb, lc):
    bsz, lt, d = hs.shape
    rows = (lt - lc) // GRID_W
    assert rows >= NA_WIN_ROWS and rows % (NA_BLK_ROWS * NA_BLK_UNROLL) == 0
    npair = NA_HEADS // 2
    col_scale = jnp.concatenate([jnp.full((1, d), (NA_HEAD_DIM ** -0.5) * LOG2E, F32),
                                 jnp.ones((1, 2 * d), F32)], axis=-1)
    qkv = _norm_mod_matmul(hs, g, sh, sc, w_qkv.astype(BF16), col_scale, ncb=ncb, out_dtype=BF16, tn=1024)
    bias = _na_bias_table(rpb)
    o = pl.pallas_call(
        functools.partial(_na_kernel, lc=lc, rows=rows),
        grid=(npair, bsz),
        in_specs=[
            pl.BlockSpec((1, lt, LANES), lambda p, b: (b, 0, p)),
            pl.BlockSpec((1, lt, LANES), lambda p, b: (b, 0, npair + p)),
            pl.BlockSpec((1, lt, LANES), lambda p, b: (b, 0, 2 * npair + p)),
            pl.BlockSpec((2, NA_TABLE_PAIRS, GRID_W, 2 * GRID_W), lambda p, b: (p, 0, 0, 0)),
        ],
        out_specs=pl.BlockSpec((1, lt, LANES), lambda p, b: (b, 0, p)),
        out_shape=jax.ShapeDtypeStruct((bsz, lt, d), BF16),
        compiler_params=_cparams("parallel", "parallel"),
        name="natten",
    )(qkv, qkv, qkv, bias)
    return _matmul_gate_residual(o, w_out.astype(BF16), hs, gate, ncb=ncb)


def _layer_mods(mod_rows, bsz, d):
    mx = mod_rows[:bsz].reshape(bsz, 6, d)
    mc = jnp.broadcast_to(mod_rows[bsz].reshape(1, 6, d), (bsz, 6, d))
    m = jnp.stack([mc, mx], axis=1)
    return [m[:, :, k, None, :] for k in range(6)]


def kernel(x, c, ctx, c_ctx, ada_w, ada_b, norm1_g, norm2_g, mla_w_in, mla_q_norm_g, mla_w_q_up, mla_kv_norm_g, mla_w_kv_up, mla_w_out, s5_w_in, s5_lambda_re, s5_lambda_im, s5_log_dt, s5_b_re, s5_b_im, s5_c_re, s5_c_im, s5_d, s5_w_glu, hg_w_in, hg_lower_bound, hg_norm_g, hg_w_out, na_w_qkv, na_rpb, na_w_out, ffn_w_up, ffn_conv_w, ffn_conv_b, ffn_w_down, final_g):
    bsz, l, d = x.shape
    lc = ctx.shape[1]
    depth = ada_w.shape[0]
    assert lc % TM == 0 and l % TM == 0 and l % GRID_W == 0 and bsz + 1 <= 16
    ncb = lc // TM
    hs = jnp.concatenate([ctx, x], axis=1)
    cond_rows = jnp.zeros((16, d), F32).at[:bsz].set(c).at[bsz].set(c_ctx)
    mod_all = _ada_mod(cond_rows, ada_w, ada_b)
    lb_cum = jnp.cumsum(jax.nn.softmax(hg_lower_bound.astype(F32), axis=0), axis=0)
    lower_bounds = lb_cum - lb_cum[0]
    for i in range(depth):
        kind, j = i % 4, i // 4
        sh1, sc1, g1, sh2, sc2, g2 = _layer_mods(mod_all[i], bsz, d)
        if kind == 0:
            hs = _mla_layer(hs, norm1_g[i], sh1, sc1, g1, mla_w_in[j], mla_q_norm_g[j], mla_w_q_up[j],
                            mla_kv_norm_g[j], mla_w_kv_up[j], mla_w_out[j], ncb=ncb, lc=lc)
        elif kind == 1:
            hs = _s5_layer(hs, norm1_g[i], sh1, sc1, g1, s5_w_in[j], s5_lambda_re[j], s5_lambda_im[j],
                           s5_log_dt[j], s5_b_re[j], s5_b_im[j], s5_c_re[j], s5_c_im[j], s5_d[j], s5_w_glu[j],
                           ncb=ncb, lc=lc)
        elif kind == 2:
            hs = _hgrn2_layer(hs, norm1_g[i], sh1, sc1, g1, hg_w_in[j], lower_bounds[i], hg_norm_g[j],
                              hg_w_out[j], ncb=ncb, lc=lc)
        else:
            hs = _natten_layer(hs, norm1_g[i], sh1, sc1, g1, na_w_qkv[j], na_rpb[j], na_w_out[j],
                               ncb=ncb, lc=lc)
        hs = _conv_ffn(hs, norm2_g[i], sh2, sc2, g2, ffn_w_up[i], ffn_conv_w[i], ffn_conv_b[i],
                       ffn_w_down[i], final_g if i == depth - 1 else None, ncb=ncb)
    return hs
```

```python
import functools
import math

import jax
import jax.numpy as jnp
import numpy as np
from jax import lax
from jax.experimental import pallas as pl
from jax.experimental.pallas import tpu as pltpu

F32 = jnp.float32
BF16 = jnp.bfloat16
HI = lax.Precision.HIGHEST

EPS = 1e-6
GRID_W = 64
ROPE_THETA = 10000.0
LOG2E = math.log2(math.e)

LANES = 128
VMEM_LIMIT_BYTES = 56 * 1024 * 1024
TM = 256

MLA_HEADS = 16
MLA_Q_RANK = 384
MLA_KV_RANK = 256
MLA_NOPE = 64
MLA_ROPE = 32
MLA_V = 64
MLA_QK = MLA_NOPE + MLA_ROPE
MLA_HEADS_PER_STEP = 8

S5_GROUP_CH = 16
S5_STATE = 64
S5_T = 16
S5_SCAN_UNROLL = 4
S5_QGRP = LANES // S5_GROUP_CH

HG_HEAD_DIM = 128
HG_CHUNK = 64
HG_HEADS_PER_STEP = 8

NA_HEADS = 16
NA_HEAD_DIM = 64
NA_KH = 8
NA_KW = 16
NEG_BIG = -1e30
NA_BLK_ROWS = 8
NA_WIN_ROWS = 16
NA_TABLE_PAIRS = 4 * NA_KH - 2
NA_BLK_UNROLL = 2


def _cparams(*sem):
    return pltpu.CompilerParams(dimension_semantics=sem, vmem_limit_bytes=VMEM_LIMIT_BYTES)


def _seg(ncb):
    return lambda i: jnp.where(i >= ncb, 1, 0)


def _norm_mod(x, g, sh, sc):
    ms = jnp.mean(x * x, axis=-1, keepdims=True)
    return (x * lax.rsqrt(ms + EPS) * g) * (1.0 + sc) + sh


def _sigmoid(x):
    return 1.0 / (1.0 + jnp.exp(-x))


def _dot(a, b):
    return jnp.dot(a, b, preferred_element_type=F32)


def _dot_nt(a, b):
    return lax.dot_general(a, b, (((1,), (1,)), ((), ())), preferred_element_type=F32)


def _dot_tn(a, b):
    return lax.dot_general(a, b, (((0,), (0,)), ((), ())), preferred_element_type=F32)


def _ada_kernel(cond_ref, w_ref, b_ref, o_ref):
    cond = cond_ref[...]
    a = (cond * _sigmoid(cond)).astype(BF16)
    o_ref[0] = _dot(a, w_ref[0].astype(BF16)) + b_ref[0]


def _ada_mod(cond_rows, ada_w, ada_b):
    depth, d, n = ada_w.shape
    rows = cond_rows.shape[0]
    tn = 1536
    return pl.pallas_call(
        _ada_kernel,
        grid=(depth, n // tn),
        in_specs=[
            pl.BlockSpec((rows, d), lambda i, j: (0, 0)),
            pl.BlockSpec((1, d, tn), lambda i, j: (i, 0, j)),
            pl.BlockSpec((1, 1, tn), lambda i, j: (i, 0, j)),
        ],
        out_specs=pl.BlockSpec((1, rows, tn), lambda i, j: (i, 0, j)),
        out_shape=jax.ShapeDtypeStruct((depth, rows, n), F32),
        compiler_params=_cparams("parallel", "parallel"),
        name="ada_mod",
    )(cond_rows, ada_w, ada_b.reshape(depth, 1, n))


def _nmm_kernel(x_ref, g_ref, sh_ref, sc_ref, w_ref, cs_ref, *o_refs, tn):
    u = _norm_mod(x_ref[0], g_ref[...], sh_ref[0, 0], sc_ref[0, 0]).astype(BF16)
    start = 0
    for o_ref in o_refs:
        for j in range(o_ref.shape[2] // tn):
            sl = slice(start + j * tn, start + (j + 1) * tn)
            o_ref[0, :, j * tn:(j + 1) * tn] = (_dot(u, w_ref[:, sl]) * cs_ref[:, sl]).astype(o_ref.dtype)
        start += o_ref.shape[2]


def _norm_mod_matmul(hs, g, sh, sc, w, col_scale, *, ncb, out_dtype, tn):
    bsz, lt, d = hs.shape
    n = w.shape[1]
    seg = _seg(ncb)
    multi = isinstance(out_dtype, tuple)
    parts = out_dtype if multi else ((out_dtype, n),)
    assert sum(c for _, c in parts) == n
    outs = pl.pallas_call(
        functools.partial(_nmm_kernel, tn=tn),
        grid=(bsz, lt // TM),
        in_specs=[
            pl.BlockSpec((1, TM, d), lambda b, i: (b, i, 0)),
            pl.BlockSpec((1, d), lambda b, i: (0, 0)),
            pl.BlockSpec((1, 1, 1, d), lambda b, i: (b, seg(i), 0, 0)),
            pl.BlockSpec((1, 1, 1, d), lambda b, i: (b, seg(i), 0, 0)),
            pl.BlockSpec((d, n), lambda b, i: (0, 0)),
            pl.BlockSpec((1, n), lambda b, i: (0, 0)),
        ],
        out_specs=[pl.BlockSpec((1, TM, c), lambda b, i: (b, i, 0)) for _, c in parts],
        out_shape=[jax.ShapeDtypeStruct((bsz, lt, c), dt) for dt, c in parts],
        compiler_params=_cparams("parallel", "parallel"),
        name="norm_mod_matmul",
    )(hs, g.reshape(1, d), sh, sc, w, col_scale)
    return tuple(outs) if multi else outs[0]


def _mgr_kernel(a_ref, w_ref, x_ref, gate_ref, o_ref):
    o_ref[0] = x_ref[0] + gate_ref[0, 0] * _dot(a_ref[0], w_ref[...])


def _matmul_gate_residual(a, w, hs, gate, *, ncb):
    bsz, lt, d = hs.shape
    k = a.shape[2]
    seg = _seg(ncb)
    return pl.pallas_call(
        _mgr_kernel,
        grid=(bsz, lt // TM),
        in_specs=[
            pl.BlockSpec((1, TM, k), lambda b, i: (b, i, 0)),
            pl.BlockSpec((k, d), lambda b, i: (0, 0)),
            pl.BlockSpec((1, TM, d), lambda b, i: (b, i, 0)),
            pl.BlockSpec((1, 1, 1, d), lambda b, i: (b, seg(i), 0, 0)),
        ],
        out_specs=pl.BlockSpec((1, TM, d), lambda b, i: (b, i, 0)),
        out_shape=jax.ShapeDtypeStruct((bsz, lt, d), F32),
        compiler_params=_cparams("parallel", "parallel"),
        name="matmul_gate_residual",
    )(a, w, hs, gate)


FFN_HALO = 16
FFN_PAD = 8


def _ffn_kernel(x_ref, xp_ref, xn_ref, g_ref, sh_ref, sc_ref, gate_ref, wup_ref, cw_ref, cb_ref,
                wdn_ref, fg_ref, o_ref, u_scr, h_scr, *, ncb, nb, off, final):
    i = pl.program_id(1) + off
    g = g_ref[...]
    sh = sh_ref[0, 0]
    sc = sc_ref[0, 0]
    f = wdn_ref.shape[0]
    keep_prev = jnp.where((i == 0) | (i == ncb), 0.0, 1.0)
    keep_next = jnp.where((i == ncb - 1) | (i == nb - 1), 0.0, 1.0)
    u_prev = _norm_mod(xp_ref[0], g, sh, sc)[FFN_HALO - 1:FFN_HALO] * keep_prev
    u_next = _norm_mod(xn_ref[0], g, sh, sc)[0:1] * keep_next
    row = lax.broadcasted_iota(jnp.int32, (FFN_HALO, u_prev.shape[1]), 0)
    extra = jnp.where(row == 0, u_prev, jnp.where(row == 1, u_next, 0.0))
    u_scr[0:TM, :] = _norm_mod(x_ref[0], g, sh, sc).astype(BF16)
    u_scr[TM:, :] = extra.astype(BF16)
    h = _dot(u_scr[...], wup_ref[...])
    h_scr[FFN_PAD:FFN_PAD + TM, :] = h[0:TM]
    h_scr[FFN_PAD - 1:FFN_PAD, :] = h[TM:TM + 1]
    h_scr[FFN_PAD + TM:FFN_PAD + TM + 1, :] = h[TM + 1:TM + 2]
    w = cw_ref[...]
    hc = (h_scr[FFN_PAD - 1:FFN_PAD - 1 + TM, :] * w[0:1] + h_scr[FFN_PAD:FFN_PAD + TM, :] * w[1:2]
          + h_scr[FFN_PAD + 1:FFN_PAD + 1 + TM, :] * w[2:3] + cb_ref[...])
    gg = hc[:, f:]
    act = (hc[:, :f] * (gg * _sigmoid(gg))).astype(BF16)
    y = x_ref[0] + gate_ref[0, 0] * _dot(act, wdn_ref[...])
    if final:
        y = y * lax.rsqrt(jnp.mean(y * y, axis=-1, keepdims=True) + EPS) * fg_ref[...]
    o_ref[0] = y


def _conv_ffn(hs, g, sh, sc, gate, w_up, conv_w, conv_b, w_down, final_g=None, *, ncb):
    bsz, lt, d = hs.shape
    f = w_down.shape[0]
    nb = lt // TM
    final = final_g is not None
    off = ncb if final else 0
    hb = TM // FFN_HALO
    nhalo = lt // FFN_HALO
    seg = _seg(ncb)
    fg = (final_g if final else jnp.ones((d,), F32)).reshape(1, d)
    return pl.pallas_call(
        functools.partial(_ffn_kernel, ncb=ncb, nb=nb, off=off, final=final),
        grid=(bsz, nb - off),
        in_specs=[
            pl.BlockSpec((1, TM, d), lambda b, i: (b, i + off, 0)),
            pl.BlockSpec((1, FFN_HALO, d), lambda b, i: (b, jnp.maximum((i + off) * hb - 1, 0), 0)),
            pl.BlockSpec((1, FFN_HALO, d), lambda b, i: (b, jnp.minimum((i + off + 1) * hb, nhalo - 1), 0)),
            pl.BlockSpec((1, d), lambda b, i: (0, 0)),
            pl.BlockSpec((1, 1, 1, d), lambda b, i: (b, seg(i + off), 0, 0)),
            pl.BlockSpec((1, 1, 1, d), lambda b, i: (b, seg(i + off), 0, 0)),
            pl.BlockSpec((1, 1, 1, d), lambda b, i: (b, seg(i + off), 0, 0)),
            pl.BlockSpec((d, 2 * f), lambda b, i: (0, 0)),
            pl.BlockSpec((3, 2 * f), lambda b, i: (0, 0)),
            pl.BlockSpec((1, 2 * f), lambda b, i: (0, 0)),
            pl.BlockSpec((f, d), lambda b, i: (0, 0)),
            pl.BlockSpec((1, d), lambda b, i: (0, 0)),
        ],
        out_specs=pl.BlockSpec((1, TM, d), lambda b, i: (b, i, 0)),
        out_shape=jax.ShapeDtypeStruct((bsz, lt - off * TM, d), F32),
        scratch_shapes=[
            pltpu.VMEM((TM + FFN_HALO, d), BF16),
            pltpu.VMEM((TM + 2 * FFN_PAD, 2 * f), F32),
        ],
        compiler_params=_cparams("parallel", "parallel"),
        name="conv_ffn",
    )(hs, hs, hs, g.reshape(1, d), sh, sc, gate, w_up.astype(BF16), conv_w, conv_b.reshape(1, 2 * f),
      w_down.astype(BF16), fg)


def _rope_tables(lc, l):
    half = MLA_ROPE // 4
    inv = 1.0 / (ROPE_THETA ** (jnp.arange(half, dtype=F32) / half))
    t = jnp.arange(l)
    ang_r = (t // GRID_W).astype(F32)[:, None] * inv[None, :]
    ang_c = (t % GRID_W).astype(F32)[:, None] * inv[None, :]
    cos = jnp.concatenate([jnp.cos(ang_r), jnp.cos(ang_r), jnp.cos(ang_c), jnp.cos(ang_c)], axis=-1)
    sin = jnp.concatenate([-jnp.sin(ang_r), jnp.sin(ang_r), -jnp.sin(ang_c), jnp.sin(ang_c)], axis=-1)
    pad = ((lc, 0), (MLA_NOPE, LANES - MLA_QK))
    return jnp.pad(cos, pad, constant_values=1.0), jnp.pad(sin, pad)


def _mla_proj_kernel(x_ref, g_ref, sh_ref, sc_ref, win_ref, qg_ref, wq_ref, kvg_ref, wkn_ref, wv_ref,
                     cos_ref, sin_ref, q_ref, k_ref, v_ref):
    u = _norm_mod(x_ref[0], g_ref[...], sh_ref[0, 0], sc_ref[0, 0]).astype(BF16)
    lat = _dot(u, win_ref[...])
    q_lat = lat[:, :MLA_Q_RANK]
    kv_lat = lat[:, MLA_Q_RANK:MLA_Q_RANK + MLA_KV_RANK]
    kr = lat[:, MLA_Q_RANK + MLA_KV_RANK:]

    def rms(t, gg):
        return (t * lax.rsqrt(jnp.mean(t * t, axis=-1, keepdims=True) + EPS) * gg).astype(BF16)

    qn = rms(q_lat, qg_ref[...])
    kvn = rms(kv_lat, kvg_ref[...])
    cos = cos_ref[...]
    sin = sin_ref[...]
    lane = lax.broadcasted_iota(jnp.int32, (TM, LANES), 1)
    first_half = (lane % (MLA_ROPE // 2)) < (MLA_ROPE // 4)

    def rope(t):
        partner = jnp.where(first_half, pltpu.roll(t, LANES - MLA_ROPE // 4, 1), pltpu.roll(t, MLA_ROPE // 4, 1))
        return t * cos + partner * sin

    kr_rot = rope(kr)
    v_ref[0] = _dot(kvn, wv_ref[...]).astype(BF16)
    q_all = _dot(qn, wq_ref[...]) * ((MLA_QK ** -0.5) * LOG2E)
    kn_all = _dot(kvn, wkn_ref[...])
    for h in range(MLA_HEADS):
        sl = slice(h * LANES, (h + 1) * LANES)
        q_ref[0, :, sl] = rope(q_all[:, sl]).astype(BF16)
        k_ref[0, :, sl] = (kn_all[:, sl] + kr_rot).astype(BF16)


def _mla_attn_kernel(q_ref, k_ref, v_ref, o_ref, *, ncb, lc):
    i = pl.program_id(2)

    def attend(lk):
        q = q_ref[0]
        lane = lax.broadcasted_iota(jnp.int32, (q.shape[0], LANES), 1)
        for pair in range(q.shape[1] // (2 * LANES)):
            v = v_ref[0, 0:lk, pair * LANES:(pair + 1) * LANES]
            outs = []
            for e in range(2):
                sl = slice((2 * pair + e) * LANES, (2 * pair + e + 1) * LANES)
                s = _dot_nt(q[:, sl], k_ref[0, 0:lk, sl])
                m = jnp.max(s, axis=-1, keepdims=True)
                p = jnp.exp2(s - m)
                l = jnp.sum(p, axis=-1, keepdims=True)
                outs.append(_dot(p.astype(BF16), v) / l)
            o_ref[0, :, pair * LANES:(pair + 1) * LANES] = jnp.where(lane < MLA_V, outs[0], outs[1]).astype(o_ref.dtype)

    @pl.when(i < ncb)
    def _():
        attend(lc)

    @pl.when(i >= ncb)
    def _():
        attend(k_ref.shape[1])


def _mla_layer(hs, g, sh, sc, gate, w_in, q_norm_g, w_q_up, kv_norm_g, w_kv_up, w_out, *, ncb, lc):
    bsz, lt, d = hs.shape
    l = lt - lc
    nh = MLA_HEADS
    seg = _seg(ncb)
    w_in_p = jnp.zeros((d, 768), F32)
    w_in_p = w_in_p.at[:, :MLA_Q_RANK + MLA_KV_RANK].set(w_in[:, :MLA_Q_RANK + MLA_KV_RANK])
    w_in_p = w_in_p.at[:, 640 + MLA_NOPE:640 + MLA_QK].set(w_in[:, MLA_Q_RANK + MLA_KV_RANK:])
    wq = w_q_up.reshape(MLA_Q_RANK, nh, MLA_QK)
    wq = jnp.pad(wq, ((0, 0), (0, 0), (0, LANES - MLA_QK))).reshape(MLA_Q_RANK, nh * LANES)
    wkv = w_kv_up.reshape(MLA_KV_RANK, nh, MLA_NOPE + MLA_V)
    wkn = jnp.pad(wkv[:, :, :MLA_NOPE], ((0, 0), (0, 0), (0, LANES - MLA_NOPE))).reshape(MLA_KV_RANK, nh * LANES)
    wv = wkv[:, :, MLA_NOPE:].reshape(MLA_KV_RANK, nh * MLA_V)
    cos_t, sin_t = _rope_tables(lc, l)

    q, k, v = pl.pallas_call(
        _mla_proj_kernel,
        grid=(bsz, lt // TM),
        in_specs=[
            pl.BlockSpec((1, TM, d), lambda b, i: (b, i, 0)),
            pl.BlockSpec((1, d), lambda b, i: (0, 0)),
            pl.BlockSpec((1, 1, 1, d), lambda b, i: (b, seg(i), 0, 0)),
            pl.BlockSpec((1, 1, 1, d), lambda b, i: (b, seg(i), 0, 0)),
            pl.BlockSpec((d, 768), lambda b, i: (0, 0)),
            pl.BlockSpec((1, MLA_Q_RANK), lambda b, i: (0, 0)),
            pl.BlockSpec((MLA_Q_RANK, nh * LANES), lambda b, i: (0, 0)),
            pl.BlockSpec((1, MLA_KV_RANK), lambda b, i: (0, 0)),
            pl.BlockSpec((MLA_KV_RANK, nh * LANES), lambda b, i: (0, 0)),
            pl.BlockSpec((MLA_KV_RANK, nh * MLA_V), lambda b, i: (0, 0)),
            pl.BlockSpec((TM, LANES), lambda b, i: (i, 0)),
            pl.BlockSpec((TM, LANES), lambda b, i: (i, 0)),
        ],
        out_specs=[
            pl.BlockSpec((1, TM, nh * LANES), lambda b, i: (b, i, 0)),
            pl.BlockSpec((1, TM, nh * LANES), lambda b, i: (b, i, 0)),
            pl.BlockSpec((1, TM, nh * MLA_V), lambda b, i: (b, i, 0)),
        ],
        out_shape=[
            jax.ShapeDtypeStruct((bsz, lt, nh * LANES), BF16),
            jax.ShapeDtypeStruct((bsz, lt, nh * LANES), BF16),
            jax.ShapeDtypeStruct((bsz, lt, nh * MLA_V), BF16),
        ],
        compiler_params=_cparams("parallel", "parallel"),
        name="mla_proj",
    )(hs, g.reshape(1, d), sh, sc, w_in_p.astype(BF16), q_norm_g.reshape(1, -1), wq.astype(BF16),
      kv_norm_g.reshape(1, -1), wkn.astype(BF16), wv.astype(BF16), cos_t, sin_t)

    o = pl.pallas_call(
        functools.partial(_mla_attn_kernel, ncb=ncb, lc=lc),
        grid=(bsz, nh // MLA_HEADS_PER_STEP, lt // TM),
        in_specs=[
            pl.BlockSpec((1, TM, MLA_HEADS_PER_STEP * LANES), lambda b, p, i: (b, i, p)),
            pl.BlockSpec((1, lt, MLA_HEADS_PER_STEP * LANES), lambda b, p, i: (b, 0, p)),
            pl.BlockSpec((1, lt, MLA_HEADS_PER_STEP * MLA_V), lambda b, p, i: (b, 0, p)),
        ],
        out_specs=pl.BlockSpec((1, TM, MLA_HEADS_PER_STEP * MLA_V), lambda b, p, i: (b, i, p)),
        out_shape=jax.ShapeDtypeStruct((bsz, lt, nh * MLA_V), BF16),
        compiler_params=_cparams("parallel", "parallel", "parallel"),
        name="mla_attn",
    )(q, k, v)
    return _matmul_gate_residual(o, w_out.astype(BF16), hs, gate, ncb=ncb)


def _s5_matrices(lam_re, lam_im, log_dt, b_re, b_im, c_re, c_im):
    t_len = S5_T
    n_grp = lam_re.shape[1]
    ch = S5_GROUP_CH
    n_st = S5_STATE
    dt = jnp.exp(log_dt.astype(F32))[..., None]
    ld_re = lam_re.astype(F32) * dt
    ld_im = lam_im.astype(F32) * dt
    tau = jnp.arange(t_len + 1, dtype=F32)[:, None, None, None]
    mag = jnp.exp(tau * ld_re[None])
    pw_re = mag * jnp.cos(tau * ld_im[None])
    pw_im = mag * jnp.sin(tau * ld_im[None])
    tau_r = jnp.arange(t_len, -1, -1, dtype=F32)[:, None, None, None]
    mag_r = jnp.exp(tau_r * ld_re[None])
    pr_re = mag_r * jnp.cos(tau_r * ld_im[None])
    pr_im = mag_r * jnp.sin(tau_r * ld_im[None])
    lb_re, lb_im = pw_re[1] - 1.0, pw_im[1]
    den = lam_re * lam_re + lam_im * lam_im
    f_re = (lb_re * lam_re + lb_im * lam_im) / den
    f_im = (lb_im * lam_re - lb_re * lam_im) / den
    bb_re = f_re[..., None] * b_re - f_im[..., None] * b_im
    bb_im = f_re[..., None] * b_im + f_im[..., None] * b_re
    cp_re = c_re[None] * pw_re[:t_len, :, :, None, :] - c_im[None] * pw_im[:t_len, :, :, None, :]
    cp_im = c_re[None] * pw_im[:t_len, :, :, None, :] + c_im[None] * pw_re[:t_len, :, :, None, :]
    taps = (jnp.einsum("tdgon,dgni->tdgoi", cp_re, bb_re, precision=HI)
            - jnp.einsum("tdgon,dgni->tdgoi", cp_im, bb_im, precision=HI))
    s_idx = np.arange(t_len)[:, None, None]
    t_idx = np.arange(t_len)[None, :, None]
    u_idx = np.arange(t_len)[None, None, :]
    sel_f = jnp.asarray((t_idx - s_idx == u_idx).astype(np.float32))
    sel_b = jnp.asarray((s_idx - t_idx == u_idx).astype(np.float32))
    kf = jnp.einsum("stu,ugoi->stgoi", sel_f, taps[:, 0], precision=HI)
    kb = jnp.einsum("stu,ugoi->stgoi", sel_b, taps[:, 1], precision=HI)
    k_tot = (kf + kb).transpose(2, 0, 4, 1, 3).reshape(n_grp, t_len * ch, t_len * ch)

    def state_in(pw_r, pw_i, brr, bii):
        re = pw_r[..., None] * brr[None] - pw_i[..., None] * bii[None]
        im = pw_r[..., None] * bii[None] + pw_i[..., None] * brr[None]
        return (re.transpose(1, 0, 3, 2).reshape(n_grp, t_len * ch, n_st),
                im.transpose(1, 0, 3, 2).reshape(n_grp, t_len * ch, n_st))

    inf_re, inf_im = state_in(pr_re[1:, 0], pr_im[1:, 0], bb_re[0], bb_im[0])
    inb_re, inb_im = state_in(pw_re[:t_len, 1], pw_im[:t_len, 1], bb_re[1], bb_im[1])

    def state_out(pw_r, pw_i, crr, cii):
        re = crr[None] * pw_r[:, :, None, :] - cii[None] * pw_i[:, :, None, :]
        im = crr[None] * pw_i[:, :, None, :] + cii[None] * pw_r[:, :, None, :]
        return (re.transpose(1, 3, 0, 2).reshape(n_grp, n_st, t_len * ch),
                (-im).transpose(1, 3, 0, 2).reshape(n_grp, n_st, t_len * ch))

    outf_re, outf_im = state_out(pw_re[1:, 0], pw_im[1:, 0], c_re[0], c_im[0])
    outb_re, outb_im = state_out(pr_re[:t_len, 1], pr_im[:t_len, 1], c_re[1], c_im[1])
    nq = n_grp // S5_QGRP
    npair = n_grp // 2
    cols = t_len * ch
    x = jnp.stack([inf_re, inf_im, inb_re, inb_im], 0).reshape(4, npair, 2, cols, n_st)
    x = x.transpose(1, 2, 3, 0, 4)
    zx = jnp.zeros_like(x[:, 0])
    w_pair = jnp.concatenate([jnp.concatenate([x[:, 0], zx], -1), jnp.concatenate([zx, x[:, 1]], -1)], 1)
    w_pair = w_pair.reshape(nq, S5_QGRP // 2, 2 * cols, 4 * 2 * n_st)
    y = jnp.stack([outf_re, outf_im, outb_re, outb_im], 0).reshape(4, npair, 2, n_st, cols)
    y = y.transpose(1, 0, 2, 3, 4)
    zy = jnp.zeros_like(y[:, :, 0])
    m_pair = jnp.stack([jnp.concatenate([y[:, :, 0], zy], -1), jnp.concatenate([zy, y[:, :, 1]], -1)], 2)
    m_pair = m_pair.reshape(nq, S5_QGRP // 2, 4 * 2 * n_st, 2 * cols)
    decay = jnp.stack([pw_re[t_len, 0], pw_im[t_len, 0], pw_re[t_len, 1], pw_im[t_len, 1]], 0)
    decay = decay.reshape(4, nq, S5_QGRP * n_st).transpose(1, 0, 2).reshape(nq, 1, 4 * S5_QGRP * n_st)
    k_grp = k_tot.reshape(nq, S5_QGRP, cols, cols)
    return w_pair.astype(BF16), k_grp.astype(BF16), m_pair.astype(BF16), decay


def _s5_permutation():
    n = S5_T * LANES
    src = jnp.arange(n)
    t, g, c = src // LANES, (src % LANES) // S5_GROUP_CH, src % S5_GROUP_CH
    dst = g * (S5_T * S5_GROUP_CH) + t * S5_GROUP_CH + c
    return (dst[:, None] == jnp.arange(n)[None, :]).astype(BF16)


def _s5_chunk_rows(z_ref):
    jt = z_ref.shape[1] // S5_T
    return jnp.concatenate([z_ref[0, pl.ds(t, jt, stride=S5_T), :].astype(BF16) for t in range(S5_T)], axis=1)


def _s5_core_kernel(z_ref, perm_ref, win_ref, kg_ref, mout_ref, a_ref, y_ref, u_scr, p_scr, *, jc, jt):
    sc = a_ref.shape[2] // 4
    npair = win_ref.shape[1]
    pc = win_ref.shape[2]
    gc = pc // 2
    zg = _dot(_s5_chunk_rows(z_ref), perm_ref[...]).astype(BF16)
    for pp in range(npair):
        res = _dot(zg[:, pp * pc:(pp + 1) * pc], win_ref[0, pp])
        for k in range(4):
            u_scr[:, k * sc + pp * LANES:k * sc + (pp + 1) * LANES] = res[:, k * LANES:(k + 1) * LANES]
    a = a_ref[0]
    afr, afi, abr, abi = (a[:, k * sc:(k + 1) * sc] for k in range(4))
    zero = jnp.zeros((1, sc), F32)

    def body(jj, carry):
        fr, fi, br, bi = carry
        jb = jnp.where(jj < jc, jc - 1 - jj, jt - 1 - jj + jc)
        p_scr[pl.ds(jj, 1), 0:sc] = fr
        p_scr[pl.ds(jj, 1), sc:2 * sc] = fi
        p_scr[pl.ds(jb, 1), 2 * sc:3 * sc] = br
        p_scr[pl.ds(jb, 1), 3 * sc:4 * sc] = bi
        nfr = afr * fr - afi * fi + u_scr[pl.ds(jj, 1), 0:sc]
        nfi = afr * fi + afi * fr + u_scr[pl.ds(jj, 1), sc:2 * sc]
        nbr = abr * br - abi * bi + u_scr[pl.ds(jb, 1), 2 * sc:3 * sc]
        nbi = abr * bi + abi * br + u_scr[pl.ds(jb, 1), 3 * sc:4 * sc]
        return nfr, nfi, nbr, nbi

    lax.fori_loop(0, jt, body, (zero, zero, zero, zero), unroll=S5_SCAN_UNROLL)
    pieces = []
    for pp in range(npair):
        p_in = jnp.concatenate([p_scr[:, k * sc + pp * LANES:k * sc + (pp + 1) * LANES] for k in range(4)], axis=1)
        ys = _dot(p_in.astype(BF16), mout_ref[0, pp])
        for e in range(2):
            g = 2 * pp + e
            yg = ys[:, e * gc:(e + 1) * gc] + _dot(zg[:, g * gc:(g + 1) * gc], kg_ref[0, g])
            pieces.append(yg.astype(BF16))
    acc = _dot_nt(jnp.concatenate(pieces, axis=1), perm_ref[...])
    for t in range(S5_T):
        y_ref[0, pl.ds(t, jt, stride=S5_T), :] = acc[:, t * LANES:(t + 1) * LANES]


def _s5_glu_kernel(y_ref, z_ref, d_ref, w_ref, x_ref, gate_ref, o_ref):
    d = x_ref.shape[2]
    y = y_ref[0] + z_ref[0] * d_ref[...]
    ge = jax.nn.gelu(y).astype(BF16)
    ag = _dot(ge, w_ref[...])
    o_ref[0] = x_ref[0] + gate_ref[0, 0] * (ag[:, :d] * _sigmoid(ag[:, d:]))


def _s5_layer(hs, g, sh, sc, gate, w_in, lam_re, lam_im, log_dt, b_re, b_im, c_re, c_im, d_skip, w_glu,
              *, ncb, lc):
    bsz, lt, d = hs.shape
    width = w_in.shape[1]
    nq = width // LANES
    jt = lt // S5_T
    jc = lc // S5_T
    ccols = S5_T * LANES
    scols = 4 * S5_QGRP * S5_STATE
    seg = _seg(ncb)
    z = _norm_mod_matmul(hs, g, sh, sc, w_in.astype(BF16), jnp.ones((1, width), F32),
                         ncb=ncb, out_dtype=F32, tn=width)
    w_pair, k_grp, m_pair, decay = _s5_matrices(lam_re, lam_im, log_dt, b_re, b_im, c_re, c_im)

    def per_quarter(arr):
        return pl.BlockSpec((1,) + arr.shape[1:], lambda q, b: (q, 0, 0, 0))

    y = pl.pallas_call(
        functools.partial(_s5_core_kernel, jc=jc, jt=jt),
        grid=(nq, bsz),
        in_specs=[
            pl.BlockSpec((1, lt, LANES), lambda q, b: (b, 0, q)),
            pl.BlockSpec((ccols, ccols), lambda q, b: (0, 0), pipeline_mode=pl.Buffered(1)),
            per_quarter(w_pair),
            per_quarter(k_grp),
            per_quarter(m_pair),
            pl.BlockSpec((1, 1, scols), lambda q, b: (q, 0, 0)),
        ],
        out_specs=pl.BlockSpec((1, lt, LANES), lambda q, b: (b, 0, q)),
        out_shape=jax.ShapeDtypeStruct((bsz, lt, width), F32),
        scratch_shapes=[pltpu.VMEM((jt, scols), F32), pltpu.VMEM((jt, scols), F32)],
        compiler_params=_cparams("parallel", "parallel"),
        name="s5_core",
    )(z, _s5_permutation(), w_pair, k_grp, m_pair, decay)
    return pl.pallas_call(
        _s5_glu_kernel,
        grid=(bsz, lt // TM),
        in_specs=[
            pl.BlockSpec((1, TM, width), lambda b, i: (b, i, 0)),
            pl.BlockSpec((1, TM, width), lambda b, i: (b, i, 0)),
            pl.BlockSpec((1, width), lambda b, i: (0, 0)),
            pl.BlockSpec((width, 2 * d), lambda b, i: (0, 0)),
            pl.BlockSpec((1, TM, d), lambda b, i: (b, i, 0)),
            pl.BlockSpec((1, 1, 1, d), lambda b, i: (b, seg(i), 0, 0)),
        ],
        out_specs=pl.BlockSpec((1, TM, d), lambda b, i: (b, i, 0)),
        out_shape=jax.ShapeDtypeStruct((bsz, lt, d), F32),
        compiler_params=_cparams("parallel", "parallel"),
        name="s5_glu",
    )(y, z, d_skip.reshape(1, width), w_glu.astype(BF16), hs, gate)


def _gla_dir(zq, zf, zv, lb, states, incl, tri, rev):
    dk = HG_HEAD_DIM
    width = zq.shape[1]
    nchunks = zq.shape[0] // HG_CHUNK
    forget = lb + (1.0 - lb) * _sigmoid(zf)
    lf = jnp.log(forget)
    kk = 1.0 - forget
    v_all = zv.astype(BF16)
    hi = lf.astype(BF16)
    r1 = lf - hi.astype(F32)
    mid = r1.astype(BF16)
    lo = (r1 - mid.astype(F32)).astype(BF16)
    parts = _dot(tri, jnp.concatenate([hi, mid, lo], axis=1))
    bcum = parts[:, :width] + parts[:, width:2 * width] + parts[:, 2 * width:]
    btot = jnp.concatenate(
        [jnp.broadcast_to(bcum[c * HG_CHUNK:c * HG_CHUNK + 1] if rev else bcum[(c + 1) * HG_CHUNK - 1:(c + 1) * HG_CHUNK],
                          (HG_CHUNK, width)) for c in range(nchunks)], axis=0)
    q_in_all = (zq * _sigmoid(zq)) * (dk ** -0.5) * jnp.exp(bcum)
    k_in_all = (kk * jnp.exp(-bcum)).astype(BF16)
    k_out_all = kk * jnp.exp(btot - bcum)
    chunk_of_row = lax.broadcasted_iota(jnp.int32, (zq.shape[0], dk), 0) // HG_CHUNK
    zero = jnp.zeros((zq.shape[0], dk), F32)
    outs, new_states = [], []
    for h, st in enumerate(states):
        sl = slice(h * dk, (h + 1) * dk)
        q_in, k_out, v = q_in_all[:, sl], k_out_all[:, sl], v_all[:, sl]
        att = jnp.where(incl, _dot_nt(q_in.astype(BF16), k_in_all[:, sl]), 0.0)
        o_intra = _dot(att.astype(BF16), v)
        k_cat = jnp.concatenate([jnp.where(chunk_of_row == c, k_out, zero) for c in range(nchunks)], axis=1)
        q_cat = jnp.concatenate([jnp.where(chunk_of_row == c, q_in, zero) for c in range(nchunks)], axis=1)
        ds = _dot_tn(v, k_cat.astype(BF16))
        entering = [None] * nchunks
        for cc in range(nchunks):
            c = nchunks - 1 - cc if rev else cc
            entering[c] = st
            st = st * jnp.exp(btot[c * HG_CHUNK:c * HG_CHUNK + 1, sl]) + ds[:, c * dk:(c + 1) * dk]
        scat = jnp.concatenate(entering, axis=1).astype(BF16)
        outs.append(o_intra + _dot_nt(q_cat.astype(BF16), scat))
        new_states.append(st)
    return outs, new_states


def _gla_kernel(zqf_ref, zff_ref, zvf_ref, zqb_ref, zfb_ref, zvb_ref, lb_ref, of_ref, ob_ref, st_scr):
    @pl.when(pl.program_id(2) == 0)
    def _():
        st_scr[...] = jnp.zeros_like(st_scr)

    n = zqf_ref.shape[1]
    row = lax.broadcasted_iota(jnp.int32, (n, n), 0)
    col = lax.broadcasted_iota(jnp.int32, (n, n), 1)
    same = (row // HG_CHUNK) == (col // HG_CHUNK)
    dk = HG_HEAD_DIM
    for direction, (zq_ref, zf_ref, zv_ref, o_ref) in enumerate(
            ((zqf_ref, zff_ref, zvf_ref, of_ref), (zqb_ref, zfb_ref, zvb_ref, ob_ref))):
        rev = direction == 1
        incl = same & ((col >= row) if rev else (col <= row))
        tri = incl.astype(BF16)
        nheads = zq_ref.shape[2] // dk
        outs, states = _gla_dir(zq_ref[0].astype(F32), zf_ref[0], zv_ref[0], lb_ref[0],
                                [st_scr[direction, h] for h in range(nheads)], incl, tri, rev)
        for h in range(nheads):
            o_ref[0, :, h * dk:(h + 1) * dk] = outs[h].astype(o_ref.dtype)
            st_scr[direction, h] = states[h]


def _hg_out_kernel(of_ref, ob_ref, zg_ref, ng_ref, w_ref, x_ref, gate_ref, o_ref):
    o = of_ref[0].astype(F32) + ob_ref[0].astype(F32)
    gsig = zg_ref[0].astype(F32)
    gsig = gsig * _sigmoid(gsig)
    ng = ng_ref[...]
    parts = []
    for h in range(o.shape[1] // HG_HEAD_DIM):
        sl = slice(h * HG_HEAD_DIM, (h + 1) * HG_HEAD_DIM)
        oh = o[:, sl]
        on = oh * lax.rsqrt(jnp.mean(oh * oh, axis=-1, keepdims=True) + EPS)
        parts.append((on * ng[:, sl] * gsig[:, sl]).astype(BF16))
    a = jnp.concatenate(parts, axis=-1)
    o_ref[0] = x_ref[0] + gate_ref[0, 0] * _dot(a, w_ref[...])


def _hgrn2_layer(hs, g, sh, sc, gate, w_in, lower_bound, norm_g, w_out, *, ncb, lc):
    bsz, lt, d = hs.shape
    nh = d // HG_HEAD_DIM
    nb = lt // TM
    seg = _seg(ncb)
    w_r = jnp.concatenate([w_in[:, d:3 * d], w_in[:, :d], w_in[:, 3 * d:]], axis=1).astype(BF16)
    zf, zr = _norm_mod_matmul(hs, g, sh, sc, w_r, jnp.ones((1, 5 * d), F32),
                              ncb=ncb, out_dtype=((F32, 2 * d), (BF16, 3 * d)), tn=1024)
    lb = lower_bound.astype(F32).reshape(1, 1, d)
    hps = HG_HEADS_PER_STEP
    ng = nh // hps
    wb = hps * HG_HEAD_DIM

    def rblk(s):
        return jnp.where(s < ncb, ncb - 1 - s, nb - 1 - s + ncb)

    outs = pl.pallas_call(
        _gla_kernel,
        grid=(bsz, ng, nb),
        in_specs=[
            pl.BlockSpec((1, TM, wb), lambda b, h, s: (b, s, h)),
            pl.BlockSpec((1, TM, wb), lambda b, h, s: (b, s, h)),
            pl.BlockSpec((1, TM, wb), lambda b, h, s: (b, s, ng + h)),
            pl.BlockSpec((1, TM, wb), lambda b, h, s: (b, rblk(s), h)),
            pl.BlockSpec((1, TM, wb), lambda b, h, s: (b, rblk(s), ng + h)),
            pl.BlockSpec((1, TM, wb), lambda b, h, s: (b, rblk(s), ng + h)),
            pl.BlockSpec((1, 1, wb), lambda b, h, s: (0, 0, h)),
        ],
        out_specs=[
            pl.BlockSpec((1, TM, wb), lambda b, h, s: (b, s, h)),
            pl.BlockSpec((1, TM, wb), lambda b, h, s: (b, rblk(s), h)),
        ],
        out_shape=[jax.ShapeDtypeStruct((bsz, lt, d), BF16), jax.ShapeDtypeStruct((bsz, lt, d), BF16)],
        scratch_shapes=[pltpu.VMEM((2, hps, HG_HEAD_DIM, HG_HEAD_DIM), F32)],
        compiler_params=_cparams("parallel", "parallel", "arbitrary"),
        name="hgrn2_gla",
    )(zr, zf, zr, zr, zf, zr, lb)
    return pl.pallas_call(
        _hg_out_kernel,
        grid=(bsz, nb),
        in_specs=[
            pl.BlockSpec((1, TM, d), lambda b, i: (b, i, 0)),
            pl.BlockSpec((1, TM, d), lambda b, i: (b, i, 0)),
            pl.BlockSpec((1, TM, d), lambda b, i: (b, i, 2)),
            pl.BlockSpec((1, d), lambda b, i: (0, 0)),
            pl.BlockSpec((d, d), lambda b, i: (0, 0)),
            pl.BlockSpec((1, TM, d), lambda b, i: (b, i, 0)),
            pl.BlockSpec((1, 1, 1, d), lambda b, i: (b, seg(i), 0, 0)),
        ],
        out_specs=pl.BlockSpec((1, TM, d), lambda b, i: (b, i, 0)),
        out_shape=jax.ShapeDtypeStruct((bsz, lt, d), F32),
        compiler_params=_cparams("parallel", "parallel"),
        name="hgrn2_out",
    )(outs[0], outs[1], zr, norm_g.reshape(1, d), w_out.astype(BF16), hs, gate)


def _na_bias_table(rpb):
    w = np.arange(GRID_W)[:, None, None]
    kc = np.arange(GRID_W)[None, :, None]
    co = np.arange(2 * NA_KW - 1)[None, None, :]
    c0 = np.clip(w - NA_KW // 2, 0, GRID_W - NA_KW)
    valid = (kc >= c0) & (kc < c0 + NA_KW)
    onehot = jnp.asarray((valid & (kc - w + (NA_KW - 1) == co)).astype(np.float32))
    t = jnp.einsum("hrc,wkc->hrwk", rpb.astype(F32), onehot, precision=HI) * LOG2E
    t = jnp.where(jnp.asarray(valid[None, None, :, :, 0]), t, NEG_BIG)
    ext = jnp.pad(t, ((0, 0), (NA_KH, NA_KH), (0, 0), (0, 0)), constant_values=NEG_BIG)
    return jnp.concatenate([ext[:, :-1], ext[:, 1:]], axis=-1)


def _na_kernel(q_ref, k_ref, v_ref, bias_ref, o_ref, *, lc, rows):
    nblk = rows // NA_BLK_ROWS
    nq = NA_BLK_ROWS * GRID_W
    nwin = NA_WIN_ROWS * GRID_W
    o_ref[0, 0:lc, :] = jnp.zeros((lc, LANES), o_ref.dtype)
    kc = k_ref[0, 0:lc, :]
    vc = v_ref[0, 0:lc, :]
    lane = lax.broadcasted_iota(jnp.int32, (nq, LANES), 1)
    lo_half = lane < NA_HEAD_DIM
    key_lane = lax.broadcasted_iota(jnp.int32, (GRID_W, nwin), 1)
    half = NA_KH // 2

    def one_block(i):
        wr0 = jnp.clip(i * NA_BLK_ROWS - half, 0, rows - NA_WIN_ROWS)
        first, last = i == 0, i == nblk - 1
        delta0 = jnp.where(first, NA_KH - 1, jnp.where(last, -1, NA_KH - 1 - half))
        qoff = pl.multiple_of(lc + i * nq, GRID_W * NA_KH // 2)
        woff = pl.multiple_of(lc + wr0 * GRID_W, GRID_W * NA_KH // 2)
        q = q_ref[0, pl.ds(qoff, nq), :]
        kw = k_ref[0, pl.ds(woff, nwin), :]
        vw = v_ref[0, pl.ds(woff, nwin), :]
        zq = jnp.zeros_like(q)
        outs = []
        for e in range(2):
            qe = jnp.where(lo_half, q, zq) if e == 0 else jnp.where(lo_half, zq, q)
            s_all = _dot_nt(qe, kw)
            slabs = []
            for j in range(NA_BLK_ROWS):
                a = jnp.where(first, max(j - half, 0), jnp.where(last, min(j + half, NA_KH), j))
                valid = (key_lane >= a * GRID_W) & (key_lane < (a + NA_KH) * GRID_W)
                bias = jnp.concatenate([bias_ref[e, delta0 - j + NA_KH + 2 * m] for m in range(NA_WIN_ROWS // 2)],
                                       axis=1)
                slabs.append(jnp.where(valid, s_all[j * GRID_W:(j + 1) * GRID_W] + bias, NEG_BIG))
            s_loc = jnp.concatenate(slabs, axis=0)
            s_ctx = _dot_nt(qe, kc)
            m = jnp.maximum(jnp.max(s_loc, axis=-1, keepdims=True), jnp.max(s_ctx, axis=-1, keepdims=True))
            p_loc = jnp.exp2(s_loc - m)
            p_ctx = jnp.exp2(s_ctx - m)
            l = jnp.sum(p_loc, axis=-1, keepdims=True) + jnp.sum(p_ctx, axis=-1, keepdims=True)
            outs.append((_dot(p_loc.astype(BF16), vw) + _dot(p_ctx.astype(BF16), vc)) / l)
        o_ref[0, pl.ds(qoff, nq), :] = jnp.where(lo_half, outs[0], outs[1]).astype(o_ref.dtype)

    def body(ii, carry):
        for j in range(NA_BLK_UNROLL):
            one_block(ii * NA_BLK_UNROLL + j)
        return carry

    lax.fori_loop(0, nblk // NA_BLK_UNROLL, body, 0)


def _natten_layer(hs, g, sh, sc, gate, w_qkv, rpb, w_out, *, ncb, lc):
    bsz, lt, d = hs.shape
    rows = (lt - lc) // GRID_W
    assert rows >= NA_WIN_ROWS and rows % (NA_BLK_ROWS * NA_BLK_UNROLL) == 0
    npair = NA_HEADS // 2
    col_scale = jnp.concatenate([jnp.full((1, d), (NA_HEAD_DIM ** -0.5) * LOG2E, F32),
                                 jnp.ones((1, 2 * d), F32)], axis=-1)
    qkv = _norm_mod_matmul(hs, g, sh, sc, w_qkv.astype(BF16), col_scale, ncb=ncb, out_dtype=BF16, tn=1024)
    bias = _na_bias_table(rpb)
    o = pl.pallas_call(
        functools.partial(_na_kernel, lc=lc, rows=rows),
        grid=(npair, bsz),
        in_specs=[
            pl.BlockSpec((1, lt, LANES), lambda p, b: (b, 0, p)),
            pl.BlockSpec((1, lt, LANES), lambda p, b: (b, 0, npair + p)),
            pl.BlockSpec((1, lt, LANES), lambda p, b: (b, 0, 2 * npair + p)),
            pl.BlockSpec((2, NA_TABLE_PAIRS, GRID_W, 2 * GRID_W), lambda p, b: (p, 0, 0, 0)),
        ],
        out_specs=pl.BlockSpec((1, lt, LANES), lambda p, b: (b, 0, p)),
        out_shape=jax.ShapeDtypeStruct((bsz, lt, d), BF16),
        compiler_params=_cparams("parallel", "parallel"),
        name="natten",
    )(qkv, qkv, qkv, bias)
    return _matmul_gate_residual(o, w_out.astype(BF16), hs, gate, ncb=ncb)


def _layer_mods(mod_rows, bsz, d):
    mx = mod_rows[:bsz].reshape(bsz, 6, d)
    mc = jnp.broadcast_to(mod_rows[bsz].reshape(1, 6, d), (bsz, 6, d))
    m = jnp.stack([mc, mx], axis=1)
    return [m[:, :, k, None, :] for k in range(6)]


def kernel(x, c, ctx, c_ctx, ada_w, ada_b, norm1_g, norm2_g, mla_w_in, mla_q_norm_g, mla_w_q_up, mla_kv_norm_g, mla_w_kv_up, mla_w_out, s5_w_in, s5_lambda_re, s5_lambda_im, s5_log_dt, s5_b_re, s5_b_im, s5_c_re, s5_c_im, s5_d, s5_w_glu, hg_w_in, hg_lower_bound, hg_norm_g, hg_w_out, na_w_qkv, na_rpb, na_w_out, ffn_w_up, ffn_conv_w, ffn_conv_b, ffn_w_down, final_g):
    bsz, l, d = x.shape
    lc = ctx.shape[1]
    depth = ada_w.shape[0]
    assert lc % TM == 0 and l % TM == 0 and l % GRID_W == 0 and bsz + 1 <= 16
    ncb = lc // TM
    hs = jnp.concatenate([ctx, x], axis=1)
    cond_rows = jnp.zeros((16, d), F32).at[:bsz].set(c).at[bsz].set(c_ctx)
    mod_all = _ada_mod(cond_rows, ada_w, ada_b)
    lb_cum = jnp.cumsum(jax.nn.softmax(hg_lower_bound.astype(F32), axis=0), axis=0)
    lower_bounds = lb_cum - lb_cum[0]
    for i in range(depth):
        kind, j = i % 4, i // 4
        sh1, sc1, g1, sh2, sc2, g2 = _layer_mods(mod_all[i], bsz, d)
        if kind == 0:
            hs = _mla_layer(hs, norm1_g[i], sh1, sc1, g1, mla_w_in[j], mla_q_norm_g[j], mla_w_q_up[j],
                            mla_kv_norm_g[j], mla_w_kv_up[j], mla_w_out[j], ncb=ncb, lc=lc)
        elif kind == 1:
            hs = _s5_layer(hs, norm1_g[i], sh1, sc1, g1, s5_w_in[j], s5_lambda_re[j], s5_lambda_im[j],
                           s5_log_dt[j], s5_b_re[j], s5_b_im[j], s5_c_re[j], s5_c_im[j], s5_d[j], s5_w_glu[j],
                           ncb=ncb, lc=lc)
        elif kind == 2:
            hs = _hgrn2_layer(hs, norm1_g[i], sh1, sc1, g1, hg_w_in[j], lower_bounds[i], hg_norm_g[j],
                              hg_w_out[j], ncb=ncb, lc=lc)
        else:
            hs = _natten_layer(hs, norm1_g[i], sh1, sc1, g1, na_w_qkv[j], na_rpb[j], na_w_out[j],
                               ncb=ncb, lc=lc)
        hs = _conv_ffn(hs, norm2_g[i], sh2, sc2, g2, ffn_w_up[i], ffn_conv_w[i], ffn_conv_b[i],
                       ffn_w_down[i], final_g if i == depth - 1 else None, ncb=ncb)
    return hs
```

```python
import functools
import math

import jax
import jax.numpy as jnp
import numpy as np
from jax import lax
from jax.experimental import pallas as pl
from jax.experimental.pallas import tpu as pltpu

F32 = jnp.float32
BF16 = jnp.bfloat16
HI = lax.Precision.HIGHEST

EPS = 1e-6
GRID_W = 64
ROPE_THETA = 10000.0
LOG2E = math.log2(math.e)

LANES = 128
VMEM_LIMIT_BYTES = 56 * 1024 * 1024
TM = 256

MLA_HEADS = 16
MLA_Q_RANK = 384
MLA_KV_RANK = 256
MLA_NOPE = 64
MLA_ROPE = 32
MLA_V = 64
MLA_QK = MLA_NOPE + MLA_ROPE
MLA_HEADS_PER_STEP = 8

S5_GROUP_CH = 16
S5_STATE = 64
S5_T = 16
S5_SCAN_UNROLL = 4
S5_QGRP = LANES // S5_GROUP_CH

HG_HEAD_DIM = 128
HG_CHUNK = 64
HG_HEADS_PER_STEP = 8

NA_HEADS = 16
NA_HEAD_DIM = 64
NA_KH = 8
NA_KW = 16
NEG_BIG = -1e30
NA_BLK_ROWS = 8
NA_WIN_ROWS = 16
NA_TABLE_PAIRS = 4 * NA_KH - 2
NA_BLK_UNROLL = 4


def _cparams(*sem):
    return pltpu.CompilerParams(dimension_semantics=sem, vmem_limit_bytes=VMEM_LIMIT_BYTES)


def _seg(ncb):
    return lambda i: jnp.where(i >= ncb, 1, 0)


def _norm_mod(x, g, sh, sc):
    ms = jnp.mean(x * x, axis=-1, keepdims=True)
    return (x * lax.rsqrt(ms + EPS) * g) * (1.0 + sc) + sh


def _sigmoid(x):
    return 1.0 / (1.0 + jnp.exp(-x))


def _dot(a, b):
    return jnp.dot(a, b, preferred_element_type=F32)


def _dot_nt(a, b):
    return lax.dot_general(a, b, (((1,), (1,)), ((), ())), preferred_element_type=F32)


def _dot_tn(a, b):
    return lax.dot_general(a, b, (((0,), (0,)), ((), ())), preferred_element_type=F32)


def _ada_kernel(cond_ref, w_ref, b_ref, o_ref):
    cond = cond_ref[...]
    a = (cond * _sigmoid(cond)).astype(BF16)
    o_ref[0] = _dot(a, w_ref[0].astype(BF16)) + b_ref[0]


def _ada_mod(cond_rows, ada_w, ada_b):
    depth, d, n = ada_w.shape
    rows = cond_rows.shape[0]
    tn = 1536
    return pl.pallas_call(
        _ada_kernel,
        grid=(depth, n // tn),
        in_specs=[
            pl.BlockSpec((rows, d), lambda i, j: (0, 0)),
            pl.BlockSpec((1, d, tn), lambda i, j: (i, 0, j)),
            pl.BlockSpec((1, 1, tn), lambda i, j: (i, 0, j)),
        ],
        out_specs=pl.BlockSpec((1, rows, tn), lambda i, j: (i, 0, j)),
        out_shape=jax.ShapeDtypeStruct((depth, rows, n), F32),
        compiler_params=_cparams("parallel", "parallel"),
        name="ada_mod",
    )(cond_rows, ada_w, ada_b.reshape(depth, 1, n))


def _nmm_kernel(x_ref, g_ref, sh_ref, sc_ref, w_ref, cs_ref, *o_refs, tn):
    u = _norm_mod(x_ref[0], g_ref[...], sh_ref[0, 0], sc_ref[0, 0]).astype(BF16)
    start = 0
    for o_ref in o_refs:
        for j in range(o_ref.shape[2] // tn):
            sl = slice(start + j * tn, start + (j + 1) * tn)
            o_ref[0, :, j * tn:(j + 1) * tn] = (_dot(u, w_ref[:, sl]) * cs_ref[:, sl]).astype(o_ref.dtype)
        start += o_ref.shape[2]


def _norm_mod_matmul(hs, g, sh, sc, w, col_scale, *, ncb, out_dtype, tn):
    bsz, lt, d = hs.shape
    n = w.shape[1]
    seg = _seg(ncb)
    multi = isinstance(out_dtype, tuple)
    parts = out_dtype if multi else ((out_dtype, n),)
    assert sum(c for _, c in parts) == n
    outs = pl.pallas_call(
        functools.partial(_nmm_kernel, tn=tn),
        grid=(bsz, lt // TM),
        in_specs=[
            pl.BlockSpec((1, TM, d), lambda b, i: (b, i, 0)),
            pl.BlockSpec((1, d), lambda b, i: (0, 0)),
            pl.BlockSpec((1, 1, 1, d), lambda b, i: (b, seg(i), 0, 0)),
            pl.BlockSpec((1, 1, 1, d), lambda b, i: (b, seg(i), 0, 0)),
            pl.BlockSpec((d, n), lambda b, i: (0, 0)),
            pl.BlockSpec((1, n), lambda b, i: (0, 0)),
        ],
        out_specs=[pl.BlockSpec((1, TM, c), lambda b, i: (b, i, 0)) for _, c in parts],
        out_shape=[jax.ShapeDtypeStruct((bsz, lt, c), dt) for dt, c in parts],
        compiler_params=_cparams("parallel", "parallel"),
        name="norm_mod_matmul",
    )(hs, g.reshape(1, d), sh, sc, w, col_scale)
    return tuple(outs) if multi else outs[0]


def _mgr_kernel(a_ref, w_ref, x_ref, gate_ref, o_ref):
    o_ref[0] = x_ref[0] + gate_ref[0, 0] * _dot(a_ref[0], w_ref[...])


def _matmul_gate_residual(a, w, hs, gate, *, ncb):
    bsz, lt, d = hs.shape
    k = a.shape[2]
    seg = _seg(ncb)
    return pl.pallas_call(
        _mgr_kernel,
        grid=(bsz, lt // TM),
        in_specs=[
            pl.BlockSpec((1, TM, k), lambda b, i: (b, i, 0)),
            pl.BlockSpec((k, d), lambda b, i: (0, 0)),
            pl.BlockSpec((1, TM, d), lambda b, i: (b, i, 0)),
            pl.BlockSpec((1, 1, 1, d), lambda b, i: (b, seg(i), 0, 0)),
        ],
        out_specs=pl.BlockSpec((1, TM, d), lambda b, i: (b, i, 0)),
        out_shape=jax.ShapeDtypeStruct((bsz, lt, d), F32),
        compiler_params=_cparams("parallel", "parallel"),
        name="matmul_gate_residual",
    )(a, w, hs, gate)


FFN_HALO = 16
FFN_PAD = 8


def _ffn_kernel(x_ref, xp_ref, xn_ref, g_ref, sh_ref, sc_ref, gate_ref, wup_ref, cw_ref, cb_ref,
                wdn_ref, fg_ref, o_ref, u_scr, h_scr, *, ncb, nb, off, final):
    i = pl.program_id(1) + off
    g = g_ref[...]
    sh = sh_ref[0, 0]
    sc = sc_ref[0, 0]
    f = wdn_ref.shape[0]
    keep_prev = jnp.where((i == 0) | (i == ncb), 0.0, 1.0)
    keep_next = jnp.where((i == ncb - 1) | (i == nb - 1), 0.0, 1.0)
    u_prev = _norm_mod(xp_ref[0], g, sh, sc)[FFN_HALO - 1:FFN_HALO] * keep_prev
    u_next = _norm_mod(xn_ref[0], g, sh, sc)[0:1] * keep_next
    row = lax.broadcasted_iota(jnp.int32, (FFN_HALO, u_prev.shape[1]), 0)
    extra = jnp.where(row == 0, u_prev, jnp.where(row == 1, u_next, 0.0))
    u_scr[0:TM, :] = _norm_mod(x_ref[0], g, sh, sc).astype(BF16)
    u_scr[TM:, :] = extra.astype(BF16)
    h = _dot(u_scr[...], wup_ref[...])
    h_scr[FFN_PAD:FFN_PAD + TM, :] = h[0:TM]
    h_scr[FFN_PAD - 1:FFN_PAD, :] = h[TM:TM + 1]
    h_scr[FFN_PAD + TM:FFN_PAD + TM + 1, :] = h[TM + 1:TM + 2]
    w = cw_ref[...]
    hc = (h_scr[FFN_PAD - 1:FFN_PAD - 1 + TM, :] * w[0:1] + h_scr[FFN_PAD:FFN_PAD + TM, :] * w[1:2]
          + h_scr[FFN_PAD + 1:FFN_PAD + 1 + TM, :] * w[2:3] + cb_ref[...])
    gg = hc[:, f:]
    act = (hc[:, :f] * (gg * _sigmoid(gg))).astype(BF16)
    y = x_ref[0] + gate_ref[0, 0] * _dot(act, wdn_ref[...])
    if final:
        y = y * lax.rsqrt(jnp.mean(y * y, axis=-1, keepdims=True) + EPS) * fg_ref[...]
    o_ref[0] = y


def _conv_ffn(hs, g, sh, sc, gate, w_up, conv_w, conv_b, w_down, final_g=None, *, ncb):
    bsz, lt, d = hs.shape
    f = w_down.shape[0]
    nb = lt // TM
    final = final_g is not None
    off = ncb if final else 0
    hb = TM // FFN_HALO
    nhalo = lt // FFN_HALO
    seg = _seg(ncb)
    fg = (final_g if final else jnp.ones((d,), F32)).reshape(1, d)
    return pl.pallas_call(
        functools.partial(_ffn_kernel, ncb=ncb, nb=nb, off=off, final=final),
        grid=(bsz, nb - off),
        in_specs=[
            pl.BlockSpec((1, TM, d), lambda b, i: (b, i + off, 0)),
            pl.BlockSpec((1, FFN_HALO, d), lambda b, i: (b, jnp.maximum((i + off) * hb - 1, 0), 0)),
            pl.BlockSpec((1, FFN_HALO, d), lambda b, i: (b, jnp.minimum((i + off + 1) * hb, nhalo - 1), 0)),
            pl.BlockSpec((1, d), lambda b, i: (0, 0)),
            pl.BlockSpec((1, 1, 1, d), lambda b, i: (b, seg(i + off), 0, 0)),
            pl.BlockSpec((1, 1, 1, d), lambda b, i: (b, seg(i + off), 0, 0)),
            pl.BlockSpec((1, 1, 1, d), lambda b, i: (b, seg(i + off), 0, 0)),
            pl.BlockSpec((d, 2 * f), lambda b, i: (0, 0)),
            pl.BlockSpec((3, 2 * f), lambda b, i: (0, 0)),
            pl.BlockSpec((1, 2 * f), lambda b, i: (0, 0)),
            pl.BlockSpec((f, d), lambda b, i: (0, 0)),
            pl.BlockSpec((1, d), lambda b, i: (0, 0)),
        ],
        out_specs=pl.BlockSpec((1, TM, d), lambda b, i: (b, i, 0)),
        out_shape=jax.ShapeDtypeStruct((bsz, lt - off * TM, d), F32),
        scratch_shapes=[
            pltpu.VMEM((TM + FFN_HALO, d), BF16),
            pltpu.VMEM((TM + 2 * FFN_PAD, 2 * f), F32),
        ],
        compiler_params=_cparams("parallel", "parallel"),
        name="conv_ffn",
    )(hs, hs, hs, g.reshape(1, d), sh, sc, gate, w_up.astype(BF16), conv_w, conv_b.reshape(1, 2 * f),
      w_down.astype(BF16), fg)


def _rope_tables(lc, l):
    half = MLA_ROPE // 4
    inv = 1.0 / (ROPE_THETA ** (jnp.arange(half, dtype=F32) / half))
    t = jnp.arange(l)
    ang_r = (t // GRID_W).astype(F32)[:, None] * inv[None, :]
    ang_c = (t % GRID_W).astype(F32)[:, None] * inv[None, :]
    cos = jnp.concatenate([jnp.cos(ang_r), jnp.cos(ang_r), jnp.cos(ang_c), jnp.cos(ang_c)], axis=-1)
    sin = jnp.concatenate([-jnp.sin(ang_r), jnp.sin(ang_r), -jnp.sin(ang_c), jnp.sin(ang_c)], axis=-1)
    pad = ((lc, 0), (MLA_NOPE, LANES - MLA_QK))
    return jnp.pad(cos, pad, constant_values=1.0), jnp.pad(sin, pad)


def _mla_proj_kernel(x_ref, g_ref, sh_ref, sc_ref, win_ref, qg_ref, wq_ref, kvg_ref, wkn_ref, wv_ref,
                     cos_ref, sin_ref, q_ref, k_ref, v_ref):
    u = _norm_mod(x_ref[0], g_ref[...], sh_ref[0, 0], sc_ref[0, 0]).astype(BF16)
    lat = _dot(u, win_ref[...])
    q_lat = lat[:, :MLA_Q_RANK]
    kv_lat = lat[:, MLA_Q_RANK:MLA_Q_RANK + MLA_KV_RANK]
    kr = lat[:, MLA_Q_RANK + MLA_KV_RANK:]

    def rms(t, gg):
        return (t * lax.rsqrt(jnp.mean(t * t, axis=-1, keepdims=True) + EPS) * gg).astype(BF16)

    qn = rms(q_lat, qg_ref[...])
    kvn = rms(kv_lat, kvg_ref[...])
    cos = cos_ref[...]
    sin = sin_ref[...]
    lane = lax.broadcasted_iota(jnp.int32, (TM, LANES), 1)
    first_half = (lane % (MLA_ROPE // 2)) < (MLA_ROPE // 4)

    def rope(t):
        partner = jnp.where(first_half, pltpu.roll(t, LANES - MLA_ROPE // 4, 1), pltpu.roll(t, MLA_ROPE // 4, 1))
        return t * cos + partner * sin

    kr_rot = rope(kr)
    v_ref[0] = _dot(kvn, wv_ref[...]).astype(BF16)
    q_all = _dot(qn, wq_ref[...]) * ((MLA_QK ** -0.5) * LOG2E)
    kn_all = _dot(kvn, wkn_ref[...])
    for h in range(MLA_HEADS):
        sl = slice(h * LANES, (h + 1) * LANES)
        q_ref[0, :, sl] = rope(q_all[:, sl]).astype(BF16)
        k_ref[0, :, sl] = (kn_all[:, sl] + kr_rot).astype(BF16)


def _mla_attn_kernel(q_ref, k_ref, v_ref, o_ref, *, ncb, lc):
    i = pl.program_id(2)

    def attend(lk):
        q = q_ref[0]
        lane = lax.broadcasted_iota(jnp.int32, (q.shape[0], LANES), 1)
        for pair in range(q.shape[1] // (2 * LANES)):
            v = v_ref[0, 0:lk, pair * LANES:(pair + 1) * LANES]
            outs = []
            for e in range(2):
                sl = slice((2 * pair + e) * LANES, (2 * pair + e + 1) * LANES)
                s = _dot_nt(q[:, sl], k_ref[0, 0:lk, sl])
                m = jnp.max(s, axis=-1, keepdims=True)
                p = jnp.exp2(s - m)
                l = jnp.sum(p, axis=-1, keepdims=True)
                outs.append(_dot(p.astype(BF16), v) / l)
            o_ref[0, :, pair * LANES:(pair + 1) * LANES] = jnp.where(lane < MLA_V, outs[0], outs[1]).astype(o_ref.dtype)

    @pl.when(i < ncb)
    def _():
        attend(lc)

    @pl.when(i >= ncb)
    def _():
        attend(k_ref.shape[1])


def _mla_layer(hs, g, sh, sc, gate, w_in, q_norm_g, w_q_up, kv_norm_g, w_kv_up, w_out, *, ncb, lc):
    bsz, lt, d = hs.shape
    l = lt - lc
    nh = MLA_HEADS
    seg = _seg(ncb)
    n_lat = MLA_Q_RANK + MLA_KV_RANK
    w_in_p = jnp.concatenate([w_in[:, :n_lat], jnp.zeros((d, MLA_NOPE), w_in.dtype), w_in[:, n_lat:],
                              jnp.zeros((d, LANES - MLA_QK), w_in.dtype)], axis=1)
    wq = w_q_up.reshape(MLA_Q_RANK, nh, MLA_QK)
    wq = jnp.pad(wq, ((0, 0), (0, 0), (0, LANES - MLA_QK))).reshape(MLA_Q_RANK, nh * LANES)
    wkv = w_kv_up.reshape(MLA_KV_RANK, nh, MLA_NOPE + MLA_V)
    wkn = jnp.pad(wkv[:, :, :MLA_NOPE], ((0, 0), (0, 0), (0, LANES - MLA_NOPE))).reshape(MLA_KV_RANK, nh * LANES)
    wv = wkv[:, :, MLA_NOPE:].reshape(MLA_KV_RANK, nh * MLA_V)
    cos_t, sin_t = _rope_tables(lc, l)

    q, k, v = pl.pallas_call(
        _mla_proj_kernel,
        grid=(bsz, lt // TM),
        in_specs=[
            pl.BlockSpec((1, TM, d), lambda b, i: (b, i, 0)),
            pl.BlockSpec((1, d), lambda b, i: (0, 0)),
            pl.BlockSpec((1, 1, 1, d), lambda b, i: (b, seg(i), 0, 0)),
            pl.BlockSpec((1, 1, 1, d), lambda b, i: (b, seg(i), 0, 0)),
            pl.BlockSpec((d, 768), lambda b, i: (0, 0)),
            pl.BlockSpec((1, MLA_Q_RANK), lambda b, i: (0, 0)),
            pl.BlockSpec((MLA_Q_RANK, nh * LANES), lambda b, i: (0, 0)),
            pl.BlockSpec((1, MLA_KV_RANK), lambda b, i: (0, 0)),
            pl.BlockSpec((MLA_KV_RANK, nh * LANES), lambda b, i: (0, 0)),
            pl.BlockSpec((MLA_KV_RANK, nh * MLA_V), lambda b, i: (0, 0)),
            pl.BlockSpec((TM, LANES), lambda b, i: (i, 0)),
            pl.BlockSpec((TM, LANES), lambda b, i: (i, 0)),
        ],
        out_specs=[
            pl.BlockSpec((1, TM, nh * LANES), lambda b, i: (b, i, 0)),
            pl.BlockSpec((1, TM, nh * LANES), lambda b, i: (b, i, 0)),
            pl.BlockSpec((1, TM, nh * MLA_V), lambda b, i: (b, i, 0)),
        ],
        out_shape=[
            jax.ShapeDtypeStruct((bsz, lt, nh * LANES), BF16),
            jax.ShapeDtypeStruct((bsz, lt, nh * LANES), BF16),
            jax.ShapeDtypeStruct((bsz, lt, nh * MLA_V), BF16),
        ],
        compiler_params=_cparams("parallel", "parallel"),
        name="mla_proj",
    )(hs, g.reshape(1, d), sh, sc, w_in_p.astype(BF16), q_norm_g.reshape(1, -1), wq.astype(BF16),
      kv_norm_g.reshape(1, -1), wkn.astype(BF16), wv.astype(BF16), cos_t, sin_t)

    o = pl.pallas_call(
        functools.partial(_mla_attn_kernel, ncb=ncb, lc=lc),
        grid=(bsz, nh // MLA_HEADS_PER_STEP, lt // TM),
        in_specs=[
            pl.BlockSpec((1, TM, MLA_HEADS_PER_STEP * LANES), lambda b, p, i: (b, i, p)),
            pl.BlockSpec((1, lt, MLA_HEADS_PER_STEP * LANES), lambda b, p, i: (b, 0, p)),
            pl.BlockSpec((1, lt, MLA_HEADS_PER_STEP * MLA_V), lambda b, p, i: (b, 0, p)),
        ],
        out_specs=pl.BlockSpec((1, TM, MLA_HEADS_PER_STEP * MLA_V), lambda b, p, i: (b, i, p)),
        out_shape=jax.ShapeDtypeStruct((bsz, lt, nh * MLA_V), BF16),
        compiler_params=_cparams("parallel", "parallel", "parallel"),
        name="mla_attn",
    )(q, k, v)
    return _matmul_gate_residual(o, w_out.astype(BF16), hs, gate, ncb=ncb)


def _s5_matrices(lam_re, lam_im, log_dt, b_re, b_im, c_re, c_im):
    t_len = S5_T
    n_grp = lam_re.shape[1]
    ch = S5_GROUP_CH
    n_st = S5_STATE
    dt = jnp.exp(log_dt.astype(F32))[..., None]
    ld_re = lam_re.astype(F32) * dt
    ld_im = lam_im.astype(F32) * dt
    tau = jnp.arange(t_len + 1, dtype=F32)[:, None, None, None]
    mag = jnp.exp(tau * ld_re[None])
    pw_re = mag * jnp.cos(tau * ld_im[None])
    pw_im = mag * jnp.sin(tau * ld_im[None])
    tau_r = jnp.arange(t_len, -1, -1, dtype=F32)[:, None, None, None]
    mag_r = jnp.exp(tau_r * ld_re[None])
    pr_re = mag_r * jnp.cos(tau_r * ld_im[None])
    pr_im = mag_r * jnp.sin(tau_r * ld_im[None])
    lb_re, lb_im = pw_re[1] - 1.0, pw_im[1]
    den = lam_re * lam_re + lam_im * lam_im
    f_re = (lb_re * lam_re + lb_im * lam_im) / den
    f_im = (lb_im * lam_re - lb_re * lam_im) / den
    bb_re = f_re[..., None] * b_re - f_im[..., None] * b_im
    bb_im = f_re[..., None] * b_im + f_im[..., None] * b_re
    cp_re = c_re[None] * pw_re[:t_len, :, :, None, :] - c_im[None] * pw_im[:t_len, :, :, None, :]
    cp_im = c_re[None] * pw_im[:t_len, :, :, None, :] + c_im[None] * pw_re[:t_len, :, :, None, :]
    taps = (jnp.einsum("tdgon,dgni->tdgoi", cp_re, bb_re, precision=HI)
            - jnp.einsum("tdgon,dgni->tdgoi", cp_im, bb_im, precision=HI))
    s_idx = np.arange(t_len)[:, None, None]
    t_idx = np.arange(t_len)[None, :, None]
    u_idx = np.arange(t_len)[None, None, :]
    sel_f = jnp.asarray((t_idx - s_idx == u_idx).astype(np.float32))
    sel_b = jnp.asarray((s_idx - t_idx == u_idx).astype(np.float32))
    kf = jnp.einsum("stu,ugoi->stgoi", sel_f, taps[:, 0], precision=HI)
    kb = jnp.einsum("stu,ugoi->stgoi", sel_b, taps[:, 1], precision=HI)
    k_tot = (kf + kb).transpose(2, 0, 4, 1, 3).reshape(n_grp, t_len * ch, t_len * ch)

    def state_in(pw_r, pw_i, brr, bii):
        re = pw_r[..., None] * brr[None] - pw_i[..., None] * bii[None]
        im = pw_r[..., None] * bii[None] + pw_i[..., None] * brr[None]
        return (re.transpose(1, 0, 3, 2).reshape(n_grp, t_len * ch, n_st),
                im.transpose(1, 0, 3, 2).reshape(n_grp, t_len * ch, n_st))

    inf_re, inf_im = state_in(pr_re[1:, 0], pr_im[1:, 0], bb_re[0], bb_im[0])
    inb_re, inb_im = state_in(pw_re[:t_len, 1], pw_im[:t_len, 1], bb_re[1], bb_im[1])

    def state_out(pw_r, pw_i, crr, cii):
        re = crr[None] * pw_r[:, :, None, :] - cii[None] * pw_i[:, :, None, :]
        im = crr[None] * pw_i[:, :, None, :] + cii[None] * pw_r[:, :, None, :]
        return (re.transpose(1, 3, 0, 2).reshape(n_grp, n_st, t_len * ch),
                (-im).transpose(1, 3, 0, 2).reshape(n_grp, n_st, t_len * ch))

    outf_re, outf_im = state_out(pw_re[1:, 0], pw_im[1:, 0], c_re[0], c_im[0])
    outb_re, outb_im = state_out(pr_re[:t_len, 1], pr_im[:t_len, 1], c_re[1], c_im[1])
    nq = n_grp // S5_QGRP
    npair = n_grp // 2
    cols = t_len * ch
    x = jnp.stack([inf_re, inf_im, inb_re, inb_im], 0).reshape(4, npair, 2, cols, n_st)
    x = x.transpose(1, 2, 3, 0, 4)
    zx = jnp.zeros_like(x[:, 0])
    w_pair = jnp.concatenate([jnp.concatenate([x[:, 0], zx], -1), jnp.concatenate([zx, x[:, 1]], -1)], 1)
    w_pair = w_pair.reshape(nq, S5_QGRP // 2, 2 * cols, 4 * 2 * n_st)
    y = jnp.stack([outf_re, outf_im, outb_re, outb_im], 0).reshape(4, npair, 2, n_st, cols)
    y = y.transpose(1, 0, 2, 3, 4)
    zy = jnp.zeros_like(y[:, :, 0])
    m_pair = jnp.stack([jnp.concatenate([y[:, :, 0], zy], -1), jnp.concatenate([zy, y[:, :, 1]], -1)], 2)
    m_pair = m_pair.reshape(nq, S5_QGRP // 2, 4 * 2 * n_st, 2 * cols)
    decay = jnp.stack([pw_re[t_len, 0], pw_im[t_len, 0], pw_re[t_len, 1], pw_im[t_len, 1]], 0)
    decay = decay.reshape(4, nq, S5_QGRP * n_st).transpose(1, 0, 2).reshape(nq, 1, 4 * S5_QGRP * n_st)
    k_grp = k_tot.reshape(nq, S5_QGRP, cols, cols)
    return w_pair.astype(BF16), k_grp.astype(BF16), m_pair.astype(BF16), decay


def _s5_permutation():
    n = S5_T * LANES
    src = jnp.arange(n)
    t, g, c = src // LANES, (src % LANES) // S5_GROUP_CH, src % S5_GROUP_CH
    dst = g * (S5_T * S5_GROUP_CH) + t * S5_GROUP_CH + c
    return (dst[:, None] == jnp.arange(n)[None, :]).astype(BF16)


def _s5_chunk_rows(z_ref):
    jt = z_ref.shape[1] // S5_T
    return jnp.concatenate([z_ref[0, pl.ds(t, jt, stride=S5_T), :].astype(BF16) for t in range(S5_T)], axis=1)


def _s5_core_kernel(z_ref, perm_ref, win_ref, kg_ref, mout_ref, a_ref, y_ref, u_scr, p_scr, *, jc, jt):
    sc = a_ref.shape[2] // 4
    npair = win_ref.shape[1]
    pc = win_ref.shape[2]
    gc = pc // 2
    zg = _dot(_s5_chunk_rows(z_ref), perm_ref[...]).astype(BF16)
    for pp in range(npair):
        res = _dot(zg[:, pp * pc:(pp + 1) * pc], win_ref[0, pp])
        for k in range(4):
            u_scr[:, k * sc + pp * LANES:k * sc + (pp + 1) * LANES] = res[:, k * LANES:(k + 1) * LANES]
    a = a_ref[0]
    afr, afi, abr, abi = (a[:, k * sc:(k + 1) * sc] for k in range(4))
    zero = jnp.zeros((1, sc), F32)

    def body(jj, carry):
        fr, fi, br, bi = carry
        jb = jnp.where(jj < jc, jc - 1 - jj, jt - 1 - jj + jc)
        p_scr[pl.ds(jj, 1), 0:sc] = fr
        p_scr[pl.ds(jj, 1), sc:2 * sc] = fi
        p_scr[pl.ds(jb, 1), 2 * sc:3 * sc] = br
        p_scr[pl.ds(jb, 1), 3 * sc:4 * sc] = bi
        nfr = afr * fr - afi * fi + u_scr[pl.ds(jj, 1), 0:sc]
        nfi = afr * fi + afi * fr + u_scr[pl.ds(jj, 1), sc:2 * sc]
        nbr = abr * br - abi * bi + u_scr[pl.ds(jb, 1), 2 * sc:3 * sc]
        nbi = abr * bi + abi * br + u_scr[pl.ds(jb, 1), 3 * sc:4 * sc]
        return nfr, nfi, nbr, nbi

    lax.fori_loop(0, jt, body, (zero, zero, zero, zero), unroll=S5_SCAN_UNROLL)
    pieces = []
    for pp in range(npair):
        p_in = jnp.concatenate([p_scr[:, k * sc + pp * LANES:k * sc + (pp + 1) * LANES] for k in range(4)], axis=1)
        ys = _dot(p_in.astype(BF16), mout_ref[0, pp])
        for e in range(2):
            g = 2 * pp + e
            yg = ys[:, e * gc:(e + 1) * gc] + _dot(zg[:, g * gc:(g + 1) * gc], kg_ref[0, g])
            pieces.append(yg.astype(BF16))
    acc = _dot_nt(jnp.concatenate(pieces, axis=1), perm_ref[...])
    for t in range(S5_T):
        y_ref[0, pl.ds(t, jt, stride=S5_T), :] = acc[:, t * LANES:(t + 1) * LANES]


def _s5_glu_kernel(y_ref, z_ref, d_ref, w_ref, x_ref, gate_ref, o_ref):
    d = x_ref.shape[2]
    y = y_ref[0] + z_ref[0] * d_ref[...]
    ge = jax.nn.gelu(y).astype(BF16)
    ag = _dot(ge, w_ref[...])
    o_ref[0] = x_ref[0] + gate_ref[0, 0] * (ag[:, :d] * _sigmoid(ag[:, d:]))


def _s5_layer(hs, g, sh, sc, gate, w_in, lam_re, lam_im, log_dt, b_re, b_im, c_re, c_im, d_skip, w_glu,
              *, ncb, lc):
    bsz, lt, d = hs.shape
    width = w_in.shape[1]
    nq = width // LANES
    jt = lt // S5_T
    jc = lc // S5_T
    ccols = S5_T * LANES
    scols = 4 * S5_QGRP * S5_STATE
    seg = _seg(ncb)
    z = _norm_mod_matmul(hs, g, sh, sc, w_in.astype(BF16), jnp.ones((1, width), F32),
                         ncb=ncb, out_dtype=F32, tn=width)
    w_pair, k_grp, m_pair, decay = _s5_matrices(lam_re, lam_im, log_dt, b_re, b_im, c_re, c_im)

    def per_quarter(arr):
        return pl.BlockSpec((1,) + arr.shape[1:], lambda q, b: (q, 0, 0, 0))

    y = pl.pallas_call(
        functools.partial(_s5_core_kernel, jc=jc, jt=jt),
        grid=(nq, bsz),
        in_specs=[
            pl.BlockSpec((1, lt, LANES), lambda q, b: (b, 0, q)),
            pl.BlockSpec((ccols, ccols), lambda q, b: (0, 0), pipeline_mode=pl.Buffered(1)),
            per_quarter(w_pair),
            per_quarter(k_grp),
            per_quarter(m_pair),
            pl.BlockSpec((1, 1, scols), lambda q, b: (q, 0, 0)),
        ],
        out_specs=pl.BlockSpec((1, lt, LANES), lambda q, b: (b, 0, q)),
        out_shape=jax.ShapeDtypeStruct((bsz, lt, width), F32),
        scratch_shapes=[pltpu.VMEM((jt, scols), F32), pltpu.VMEM((jt, scols), F32)],
        compiler_params=_cparams("parallel", "parallel"),
        name="s5_core",
    )(z, _s5_permutation(), w_pair, k_grp, m_pair, decay)
    return pl.pallas_call(
        _s5_glu_kernel,
        grid=(bsz, lt // TM),
        in_specs=[
            pl.BlockSpec((1, TM, width), lambda b, i: (b, i, 0)),
            pl.BlockSpec((1, TM, width), lambda b, i: (b, i, 0)),
            pl.BlockSpec((1, width), lambda b, i: (0, 0)),
            pl.BlockSpec((width, 2 * d), lambda b, i: (0, 0)),
            pl.BlockSpec((1, TM, d), lambda b, i: (b, i, 0)),
            pl.BlockSpec((1, 1, 1, d), lambda b, i: (b, seg(i), 0, 0)),
        ],
        out_specs=pl.BlockSpec((1, TM, d), lambda b, i: (b, i, 0)),
        out_shape=jax.ShapeDtypeStruct((bsz, lt, d), F32),
        compiler_params=_cparams("parallel", "parallel"),
        name="s5_glu",
    )(y, z, d_skip.reshape(1, width), w_glu.astype(BF16), hs, gate)


def _gla_dir(zq, zf, zv, lb, states, incl, tri, rev):
    dk = HG_HEAD_DIM
    width = zq.shape[1]
    nchunks = zq.shape[0] // HG_CHUNK
    forget = lb + (1.0 - lb) * _sigmoid(zf)
    lf = jnp.log(forget)
    kk = 1.0 - forget
    v_all = zv.astype(BF16)
    hi = lf.astype(BF16)
    r1 = lf - hi.astype(F32)
    mid = r1.astype(BF16)
    lo = (r1 - mid.astype(F32)).astype(BF16)
    parts = _dot(tri, jnp.concatenate([hi, mid, lo], axis=1))
    bcum = parts[:, :width] + parts[:, width:2 * width] + parts[:, 2 * width:]
    btot = jnp.concatenate(
        [jnp.broadcast_to(bcum[c * HG_CHUNK:c * HG_CHUNK + 1] if rev else bcum[(c + 1) * HG_CHUNK - 1:(c + 1) * HG_CHUNK],
                          (HG_CHUNK, width)) for c in range(nchunks)], axis=0)
    q_in_all = (zq * _sigmoid(zq)) * (dk ** -0.5) * jnp.exp(bcum)
    k_in_all = (kk * jnp.exp(-bcum)).astype(BF16)
    k_out_all = kk * jnp.exp(btot - bcum)
    chunk_of_row = lax.broadcasted_iota(jnp.int32, (zq.shape[0], dk), 0) // HG_CHUNK
    zero = jnp.zeros((zq.shape[0], dk), F32)
    outs, new_states = [], []
    for h, st in enumerate(states):
        sl = slice(h * dk, (h + 1) * dk)
        q_in, k_out, v = q_in_all[:, sl], k_out_all[:, sl], v_all[:, sl]
        att = jnp.where(incl, _dot_nt(q_in.astype(BF16), k_in_all[:, sl]), 0.0)
        o_intra = _dot(att.astype(BF16), v)
        k_cat = jnp.concatenate([jnp.where(chunk_of_row == c, k_out, zero) for c in range(nchunks)], axis=1)
        q_cat = jnp.concatenate([jnp.where(chunk_of_row == c, q_in, zero) for c in range(nchunks)], axis=1)
        ds = _dot_tn(v, k_cat.astype(BF16))
        entering = [None] * nchunks
        for cc in range(nchunks):
            c = nchunks - 1 - cc if rev else cc
            entering[c] = st
            st = st * jnp.exp(btot[c * HG_CHUNK:c * HG_CHUNK + 1, sl]) + ds[:, c * dk:(c + 1) * dk]
        scat = jnp.concatenate(entering, axis=1).astype(BF16)
        outs.append(o_intra + _dot_nt(q_cat.astype(BF16), scat))
        new_states.append(st)
    return outs, new_states


def _gla_kernel(zqf_ref, zff_ref, zvf_ref, zqb_ref, zfb_ref, zvb_ref, lb_ref, of_ref, ob_ref, st_scr):
    @pl.when(pl.program_id(2) == 0)
    def _():
        st_scr[...] = jnp.zeros_like(st_scr)

    n = zqf_ref.shape[1]
    row = lax.broadcasted_iota(jnp.int32, (n, n), 0)
    col = lax.broadcasted_iota(jnp.int32, (n, n), 1)
    same = (row // HG_CHUNK) == (col // HG_CHUNK)
    dk = HG_HEAD_DIM
    for direction, (zq_ref, zf_ref, zv_ref, o_ref) in enumerate(
            ((zqf_ref, zff_ref, zvf_ref, of_ref), (zqb_ref, zfb_ref, zvb_ref, ob_ref))):
        rev = direction == 1
        incl = same & ((col >= row) if rev else (col <= row))
        tri = incl.astype(BF16)
        nheads = zq_ref.shape[2] // dk
        outs, states = _gla_dir(zq_ref[0].astype(F32), zf_ref[0], zv_ref[0], lb_ref[0],
                                [st_scr[direction, h] for h in range(nheads)], incl, tri, rev)
        for h in range(nheads):
            o_ref[0, :, h * dk:(h + 1) * dk] = outs[h].astype(o_ref.dtype)
            st_scr[direction, h] = states[h]


def _hg_out_kernel(of_ref, ob_ref, zg_ref, ng_ref, w_ref, x_ref, gate_ref, o_ref):
    o = of_ref[0].astype(F32) + ob_ref[0].astype(F32)
    gsig = zg_ref[0].astype(F32)
    gsig = gsig * _sigmoid(gsig)
    ng = ng_ref[...]
    parts = []
    for h in range(o.shape[1] // HG_HEAD_DIM):
        sl = slice(h * HG_HEAD_DIM, (h + 1) * HG_HEAD_DIM)
        oh = o[:, sl]
        on = oh * lax.rsqrt(jnp.mean(oh * oh, axis=-1, keepdims=True) + EPS)
        parts.append((on * ng[:, sl] * gsig[:, sl]).astype(BF16))
    a = jnp.concatenate(parts, axis=-1)
    o_ref[0] = x_ref[0] + gate_ref[0, 0] * _dot(a, w_ref[...])


def _hgrn2_layer(hs, g, sh, sc, gate, w_in, lower_bound, norm_g, w_out, *, ncb, lc):
    bsz, lt, d = hs.shape
    nh = d // HG_HEAD_DIM
    nb = lt // TM
    seg = _seg(ncb)
    w_r = jnp.concatenate([w_in[:, d:3 * d], w_in[:, :d], w_in[:, 3 * d:]], axis=1).astype(BF16)
    zf, zr = _norm_mod_matmul(hs, g, sh, sc, w_r, jnp.ones((1, 5 * d), F32),
                              ncb=ncb, out_dtype=((F32, 2 * d), (BF16, 3 * d)), tn=1024)
    lb = lower_bound.astype(F32).reshape(1, 1, d)
    hps = HG_HEADS_PER_STEP
    ng = nh // hps
    wb = hps * HG_HEAD_DIM

    def rblk(s):
        return jnp.where(s < ncb, ncb - 1 - s, nb - 1 - s + ncb)

    outs = pl.pallas_call(
        _gla_kernel,
        grid=(bsz, ng, nb),
        in_specs=[
            pl.BlockSpec((1, TM, wb), lambda b, h, s: (b, s, h)),
            pl.BlockSpec((1, TM, wb), lambda b, h, s: (b, s, h)),
            pl.BlockSpec((1, TM, wb), lambda b, h, s: (b, s, ng + h)),
            pl.BlockSpec((1, TM, wb), lambda b, h, s: (b, rblk(s), h)),
            pl.BlockSpec((1, TM, wb), lambda b, h, s: (b, rblk(s), ng + h)),
            pl.BlockSpec((1, TM, wb), lambda b, h, s: (b, rblk(s), ng + h)),
            pl.BlockSpec((1, 1, wb), lambda b, h, s: (0, 0, h)),
        ],
        out_specs=[
            pl.BlockSpec((1, TM, wb), lambda b, h, s: (b, s, h)),
            pl.BlockSpec((1, TM, wb), lambda b, h, s: (b, rblk(s), h)),
        ],
        out_shape=[jax.ShapeDtypeStruct((bsz, lt, d), BF16), jax.ShapeDtypeStruct((bsz, lt, d), BF16)],
        scratch_shapes=[pltpu.VMEM((2, hps, HG_HEAD_DIM, HG_HEAD_DIM), F32)],
        compiler_params=_cparams("parallel", "parallel", "arbitrary"),
        name="hgrn2_gla",
    )(zr, zf, zr, zr, zf, zr, lb)
    return pl.pallas_call(
        _hg_out_kernel,
        grid=(bsz, nb),
        in_specs=[
            pl.BlockSpec((1, TM, d), lambda b, i: (b, i, 0)),
            pl.BlockSpec((1, TM, d), lambda b, i: (b, i, 0)),
            pl.BlockSpec((1, TM, d), lambda b, i: (b, i, 2)),
            pl.BlockSpec((1, d), lambda b, i: (0, 0)),
            pl.BlockSpec((d, d), lambda b, i: (0, 0)),
            pl.BlockSpec((1, TM, d), lambda b, i: (b, i, 0)),
            pl.BlockSpec((1, 1, 1, d), lambda b, i: (b, seg(i), 0, 0)),
        ],
        out_specs=pl.BlockSpec((1, TM, d), lambda b, i: (b, i, 0)),
        out_shape=jax.ShapeDtypeStruct((bsz, lt, d), F32),
        compiler_params=_cparams("parallel", "parallel"),
        name="hgrn2_out",
    )(outs[0], outs[1], zr, norm_g.reshape(1, d), w_out.astype(BF16), hs, gate)


def _na_bias_table(rpb):
    w = np.arange(GRID_W)[:, None, None]
    kc = np.arange(GRID_W)[None, :, None]
    co = np.arange(2 * NA_KW - 1)[None, None, :]
    c0 = np.clip(w - NA_KW // 2, 0, GRID_W - NA_KW)
    valid = (kc >= c0) & (kc < c0 + NA_KW)
    onehot = jnp.asarray((valid & (kc - w + (NA_KW - 1) == co)).astype(np.float32))
    t = jnp.einsum("hrc,wkc->hrwk", rpb.astype(F32), onehot, precision=HI) * LOG2E
    t = jnp.where(jnp.asarray(valid[None, None, :, :, 0]), t, NEG_BIG)
    ext = jnp.pad(t, ((0, 0), (NA_KH, NA_KH), (0, 0), (0, 0)), constant_values=NEG_BIG)
    return jnp.concatenate([ext[:, :-1], ext[:, 1:]], axis=-1)


def _na_kernel(q_ref, k_ref, v_ref, bias_ref, o_ref, *, lc, rows):
    nblk = rows // NA_BLK_ROWS
    nq = NA_BLK_ROWS * GRID_W
    nwin = NA_WIN_ROWS * GRID_W
    o_ref[0, 0:lc, :] = jnp.zeros((lc, LANES), o_ref.dtype)
    kc = k_ref[0, 0:lc, :]
    vc = v_ref[0, 0:lc, :]
    lane = lax.broadcasted_iota(jnp.int32, (nq, LANES), 1)
    lo_half = lane < NA_HEAD_DIM
    key_lane = lax.broadcasted_iota(jnp.int32, (GRID_W, nwin), 1)
    half = NA_KH // 2

    def one_block(i):
        wr0 = jnp.clip(i * NA_BLK_ROWS - half, 0, rows - NA_WIN_ROWS)
        first, last = i == 0, i == nblk - 1
        delta0 = jnp.where(first, NA_KH - 1, jnp.where(last, -1, NA_KH - 1 - half))
        qoff = pl.multiple_of(lc + i * nq, GRID_W * NA_KH // 2)
        woff = pl.multiple_of(lc + wr0 * GRID_W, GRID_W * NA_KH // 2)
        q = q_ref[0, pl.ds(qoff, nq), :]
        kw = k_ref[0, pl.ds(woff, nwin), :]
        vw = v_ref[0, pl.ds(woff, nwin), :]
        zq = jnp.zeros_like(q)
        outs = []
        for e in range(2):
            qe = jnp.where(lo_half, q, zq) if e == 0 else jnp.where(lo_half, zq, q)
            s_all = _dot_nt(qe, kw)
            slabs = []
            for j in range(NA_BLK_ROWS):
                a = jnp.where(first, max(j - half, 0), jnp.where(last, min(j + half, NA_KH), j))
                valid = (key_lane >= a * GRID_W) & (key_lane < (a + NA_KH) * GRID_W)
                bias = jnp.concatenate([bias_ref[e, delta0 - j + NA_KH + 2 * m] for m in range(NA_WIN_ROWS // 2)],
                                       axis=1)
                slabs.append(jnp.where(valid, s_all[j * GRID_W:(j + 1) * GRID_W] + bias, NEG_BIG))
            s_loc = jnp.concatenate(slabs, axis=0)
            s_ctx = _dot_nt(qe, kc)
            m = jnp.maximum(jnp.max(s_loc, axis=-1, keepdims=True), jnp.max(s_ctx, axis=-1, keepdims=True))
            p_loc = jnp.exp2(s_loc - m)
            p_ctx = jnp.exp2(s_ctx - m)
            l = jnp.sum(p_loc, axis=-1, keepdims=True) + jnp.sum(p_ctx, axis=-1, keepdims=True)
            outs.append((_dot(p_loc.astype(BF16), vw) + _dot(p_ctx.astype(BF16), vc)) / l)
        o_ref[0, pl.ds(qoff, nq), :] = jnp.where(lo_half, outs[0], outs[1]).astype(o_ref.dtype)

    def body(ii, carry):
        for j in range(NA_BLK_UNROLL):
            one_block(ii * NA_BLK_UNROLL + j)
        return carry

    lax.fori_loop(0, nblk // NA_BLK_UNROLL, body, 0)


def _natten_layer(hs, g, sh, sc, gate, w_qkv, rpb, w_out, *, ncb, lc):
    bsz, lt, d = hs.shape
    rows = (lt - lc) // GRID_W
    assert rows >= NA_WIN_ROWS and rows % (NA_BLK_ROWS * NA_BLK_UNROLL) == 0
    npair = NA_HEADS // 2
    col_scale = jnp.concatenate([jnp.full((1, d), (NA_HEAD_DIM ** -0.5) * LOG2E, F32),
                                 jnp.ones((1, 2 * d), F32)], axis=-1)
    qkv = _norm_mod_matmul(hs, g, sh, sc, w_qkv.astype(BF16), col_scale, ncb=ncb, out_dtype=BF16, tn=1024)
    bias = _na_bias_table(rpb)
    o = pl.pallas_call(
        functools.partial(_na_kernel, lc=lc, rows=rows),
        grid=(npair, bsz),
        in_specs=[
            pl.BlockSpec((1, lt, LANES), lambda p, b: (b, 0, p)),
            pl.BlockSpec((1, lt, LANES), lambda p, b: (b, 0, npair + p)),
            pl.BlockSpec((1, lt, LANES), lambda p, b: (b, 0, 2 * npair + p)),
            pl.BlockSpec((2, NA_TABLE_PAIRS, GRID_W, 2 * GRID_W), lambda p, b: (p, 0, 0, 0)),
        ],
        out_specs=pl.BlockSpec((1, lt, LANES), lambda p, b: (b, 0, p)),
        out_shape=jax.ShapeDtypeStruct((bsz, lt, d), BF16),
        compiler_params=_cparams("parallel", "parallel"),
        name="natten",
    )(qkv, qkv, qkv, bias)
    return _matmul_gate_residual(o, w_out.astype(BF16), hs, gate, ncb=ncb)


def _layer_mods(mod_rows, bsz, d):
    mx = mod_rows[:bsz].reshape(bsz, 6, d)
    mc = jnp.broadcast_to(mod_rows[bsz].reshape(1, 6, d), (bsz, 6, d))
    m = jnp.stack([mc, mx], axis=1)
    return [m[:, :, k, None, :] for k in range(6)]


def kernel(x, c, ctx, c_ctx, ada_w, ada_b, norm1_g, norm2_g, mla_w_in, mla_q_norm_g, mla_w_q_up, mla_kv_norm_g, mla_w_kv_up, mla_w_out, s5_w_in, s5_lambda_re, s5_lambda_im, s5_log_dt, s5_b_re, s5_b_im, s5_c_re, s5_c_im, s5_d, s5_w_glu, hg_w_in, hg_lower_bound, hg_norm_g, hg_w_out, na_w_qkv, na_rpb, na_w_out, ffn_w_up, ffn_conv_w, ffn_conv_b, ffn_w_down, final_g):
    bsz, l, d = x.shape
    lc = ctx.shape[1]
    depth = ada_w.shape[0]
    assert lc % TM == 0 and l % TM == 0 and l % GRID_W == 0 and bsz + 1 <= 16
    ncb = lc // TM
    hs = jnp.concatenate([ctx, x], axis=1)
    cond_rows = jnp.concatenate([c, c_ctx[None], jnp.zeros((16 - bsz - 1, d), F32)], axis=0)
    mod_all = _ada_mod(cond_rows, ada_w, ada_b)
    lb_cum = jnp.cumsum(jax.nn.softmax(hg_lower_bound.astype(F32), axis=0), axis=0)
    lower_bounds = lb_cum - lb_cum[0]
    for i in range(depth):
        kind, j = i % 4, i // 4
        sh1, sc1, g1, sh2, sc2, g2 = _layer_mods(mod_all[i], bsz, d)
        if kind == 0:
            hs = _mla_layer(hs, norm1_g[i], sh1, sc1, g1, mla_w_in[j], mla_q_norm_g[j], mla_w_q_up[j],
                            mla_kv_norm_g[j], mla_w_kv_up[j], mla_w_out[j], ncb=ncb, lc=lc)
        elif kind == 1:
            hs = _s5_layer(hs, norm1_g[i], sh1, sc1, g1, s5_w_in[j], s5_lambda_re[j], s5_lambda_im[j],
                           s5_log_dt[j], s5_b_re[j], s5_b_im[j], s5_c_re[j], s5_c_im[j], s5_d[j], s5_w_glu[j],
                           ncb=ncb, lc=lc)
        elif kind == 2:
            hs = _hgrn2_layer(hs, norm1_g[i], sh1, sc1, g1, hg_w_in[j], lower_bounds[i], hg_norm_g[j],
                              hg_w_out[j], ncb=ncb, lc=lc)
        else:
            hs = _natten_layer(hs, norm1_g[i], sh1, sc1, g1, na_w_qkv[j], na_rpb[j], na_w_out[j],
                               ncb=ncb, lc=lc)
        hs = _conv_ffn(hs, norm2_g[i], sh2, sc2, g2, ffn_w_up[i], ffn_conv_w[i], ffn_conv_b[i],
                       ffn_w_down[i], final_g if i == depth - 1 else None, ncb=ncb)
    return hs
```

```python
import functools
import math

import jax
import jax.numpy as jnp
import numpy as np
from jax import lax
from jax.experimental import pallas as pl
from jax.experimental.pallas import tpu as pltpu

F32 = jnp.float32
BF16 = jnp.bfloat16
HI = lax.Precision.HIGHEST

EPS = 1e-6
GRID_W = 64
ROPE_THETA = 10000.0
LOG2E = math.log2(math.e)

LANES = 128
VMEM_LIMIT_BYTES = 56 * 1024 * 1024
TM = 256

MLA_HEADS = 16
MLA_Q_RANK = 384
MLA_KV_RANK = 256
MLA_NOPE = 64
MLA_ROPE = 32
MLA_V = 64
MLA_QK = MLA_NOPE + MLA_ROPE
MLA_HEADS_PER_STEP = 8

S5_GROUP_CH = 16
S5_STATE = 64
S5_T = 16
S5_SCAN_UNROLL = 4
S5_QGRP = LANES // S5_GROUP_CH

HG_HEAD_DIM = 128
HG_CHUNK = 64
HG_HEADS_PER_STEP = 8

NA_HEADS = 16
NA_HEAD_DIM = 64
NA_KH = 8
NA_KW = 16
NEG_BIG = -1e30
NA_BLK_ROWS = 8
NA_WIN_ROWS = 16
NA_TABLE_PAIRS = 4 * NA_KH - 2
NA_BLK_UNROLL = 4


def _cparams(*sem):
    return pltpu.CompilerParams(dimension_semantics=sem, vmem_limit_bytes=VMEM_LIMIT_BYTES)


def _seg(ncb):
    return lambda i: jnp.where(i >= ncb, 1, 0)


def _norm_mod(x, g, sh, sc):
    ms = jnp.mean(x * x, axis=-1, keepdims=True)
    return (x * lax.rsqrt(ms + EPS) * g) * (1.0 + sc) + sh


def _sigmoid(x):
    return 1.0 / (1.0 + jnp.exp(-x))


def _dot(a, b):
    return jnp.dot(a, b, preferred_element_type=F32)


def _dot_nt(a, b):
    return lax.dot_general(a, b, (((1,), (1,)), ((), ())), preferred_element_type=F32)


def _dot_tn(a, b):
    return lax.dot_general(a, b, (((0,), (0,)), ((), ())), preferred_element_type=F32)


def _ada_kernel(cond_ref, w_ref, b_ref, o_ref):
    cond = cond_ref[...]
    a = (cond * _sigmoid(cond)).astype(BF16)
    o_ref[0] = _dot(a, w_ref[0].astype(BF16)) + b_ref[0]


def _ada_mod(cond_rows, ada_w, ada_b):
    depth, d, n = ada_w.shape
    rows = cond_rows.shape[0]
    tn = 1536
    return pl.pallas_call(
        _ada_kernel,
        grid=(depth, n // tn),
        in_specs=[
            pl.BlockSpec((rows, d), lambda i, j: (0, 0)),
            pl.BlockSpec((1, d, tn), lambda i, j: (i, 0, j)),
            pl.BlockSpec((1, 1, tn), lambda i, j: (i, 0, j)),
        ],
        out_specs=pl.BlockSpec((1, rows, tn), lambda i, j: (i, 0, j)),
        out_shape=jax.ShapeDtypeStruct((depth, rows, n), F32),
        compiler_params=_cparams("parallel", "parallel"),
        name="ada_mod",
    )(cond_rows, ada_w, ada_b.reshape(depth, 1, n))


def _nmm_kernel(x_ref, g_ref, sh_ref, sc_ref, w_ref, cs_ref, *o_refs, tn):
    u = _norm_mod(x_ref[0], g_ref[...], sh_ref[0, 0], sc_ref[0, 0]).astype(BF16)
    start = 0
    for o_ref in o_refs:
        for j in range(o_ref.shape[2] // tn):
            sl = slice(start + j * tn, start + (j + 1) * tn)
            o_ref[0, :, j * tn:(j + 1) * tn] = (_dot(u, w_ref[:, sl]) * cs_ref[:, sl]).astype(o_ref.dtype)
        start += o_ref.shape[2]


def _norm_mod_matmul(hs, g, sh, sc, w, col_scale, *, ncb, out_dtype, tn):
    bsz, lt, d = hs.shape
    n = w.shape[1]
    seg = _seg(ncb)
    multi = isinstance(out_dtype, tuple)
    parts = out_dtype if multi else ((out_dtype, n),)
    assert sum(c for _, c in parts) == n
    outs = pl.pallas_call(
        functools.partial(_nmm_kernel, tn=tn),
        grid=(bsz, lt // TM),
        in_specs=[
            pl.BlockSpec((1, TM, d), lambda b, i: (b, i, 0)),
            pl.BlockSpec((1, d), lambda b, i: (0, 0)),
            pl.BlockSpec((1, 1, 1, d), lambda b, i: (b, seg(i), 0, 0)),
            pl.BlockSpec((1, 1, 1, d), lambda b, i: (b, seg(i), 0, 0)),
            pl.BlockSpec((d, n), lambda b, i: (0, 0)),
            pl.BlockSpec((1, n), lambda b, i: (0, 0)),
        ],
        out_specs=[pl.BlockSpec((1, TM, c), lambda b, i: (b, i, 0)) for _, c in parts],
        out_shape=[jax.ShapeDtypeStruct((bsz, lt, c), dt) for dt, c in parts],
        compiler_params=_cparams("parallel", "parallel"),
        name="norm_mod_matmul",
    )(hs, g.reshape(1, d), sh, sc, w, col_scale)
    return tuple(outs) if multi else outs[0]


def _mgr_kernel(a_ref, w_ref, x_ref, gate_ref, o_ref):
    o_ref[0] = x_ref[0] + gate_ref[0, 0] * _dot(a_ref[0], w_ref[...])


def _matmul_gate_residual(a, w, hs, gate, *, ncb):
    bsz, lt, d = hs.shape
    k = a.shape[2]
    seg = _seg(ncb)
    return pl.pallas_call(
        _mgr_kernel,
        grid=(bsz, lt // TM),
        in_specs=[
            pl.BlockSpec((1, TM, k), lambda b, i: (b, i, 0)),
            pl.BlockSpec((k, d), lambda b, i: (0, 0)),
            pl.BlockSpec((1, TM, d), lambda b, i: (b, i, 0)),
            pl.BlockSpec((1, 1, 1, d), lambda b, i: (b, seg(i), 0, 0)),
        ],
        out_specs=pl.BlockSpec((1, TM, d), lambda b, i: (b, i, 0)),
        out_shape=jax.ShapeDtypeStruct((bsz, lt, d), F32),
        compiler_params=_cparams("parallel", "parallel"),
        name="matmul_gate_residual",
    )(a, w, hs, gate)


FFN_HALO = 16
FFN_PAD = 8


def _ffn_kernel(x_ref, xp_ref, xn_ref, g_ref, sh_ref, sc_ref, gate_ref, wup_ref, cw_ref, cb_ref,
                wdn_ref, fg_ref, o_ref, u_scr, h_scr, *, ncb, nb, off, final):
    i = pl.program_id(1) + off
    g = g_ref[...]
    sh = sh_ref[0, 0]
    sc = sc_ref[0, 0]
    f = wdn_ref.shape[0]
    keep_prev = jnp.where((i == 0) | (i == ncb), 0.0, 1.0)
    keep_next = jnp.where((i == ncb - 1) | (i == nb - 1), 0.0, 1.0)
    u_prev = _norm_mod(xp_ref[0], g, sh, sc)[FFN_HALO - 1:FFN_HALO] * keep_prev
    u_next = _norm_mod(xn_ref[0], g, sh, sc)[0:1] * keep_next
    row = lax.broadcasted_iota(jnp.int32, (FFN_HALO, u_prev.shape[1]), 0)
    extra = jnp.where(row == 0, u_prev, jnp.where(row == 1, u_next, 0.0))
    u_scr[0:TM, :] = _norm_mod(x_ref[0], g, sh, sc).astype(BF16)
    u_scr[TM:, :] = extra.astype(BF16)
    h = _dot(u_scr[...], wup_ref[...])
    h_scr[FFN_PAD:FFN_PAD + TM, :] = h[0:TM]
    h_scr[FFN_PAD - 1:FFN_PAD, :] = h[TM:TM + 1]
    h_scr[FFN_PAD + TM:FFN_PAD + TM + 1, :] = h[TM + 1:TM + 2]
    w = cw_ref[...]
    hc = (h_scr[FFN_PAD - 1:FFN_PAD - 1 + TM, :] * w[0:1] + h_scr[FFN_PAD:FFN_PAD + TM, :] * w[1:2]
          + h_scr[FFN_PAD + 1:FFN_PAD + 1 + TM, :] * w[2:3] + cb_ref[...])
    gg = hc[:, f:]
    act = (hc[:, :f] * (gg * _sigmoid(gg))).astype(BF16)
    y = x_ref[0] + gate_ref[0, 0] * _dot(act, wdn_ref[...])
    if final:
        y = y * lax.rsqrt(jnp.mean(y * y, axis=-1, keepdims=True) + EPS) * fg_ref[...]
    o_ref[0] = y


def _conv_ffn(hs, g, sh, sc, gate, w_up, conv_w, conv_b, w_down, final_g=None, *, ncb):
    bsz, lt, d = hs.shape
    f = w_down.shape[0]
    nb = lt // TM
    final = final_g is not None
    off = ncb if final else 0
    hb = TM // FFN_HALO
    nhalo = lt // FFN_HALO
    seg = _seg(ncb)
    fg = (final_g if final else jnp.ones((d,), F32)).reshape(1, d)
    return pl.pallas_call(
        functools.partial(_ffn_kernel, ncb=ncb, nb=nb, off=off, final=final),
        grid=(bsz, nb - off),
        in_specs=[
            pl.BlockSpec((1, TM, d), lambda b, i: (b, i + off, 0)),
            pl.BlockSpec((1, FFN_HALO, d), lambda b, i: (b, jnp.maximum((i + off) * hb - 1, 0), 0)),
            pl.BlockSpec((1, FFN_HALO, d), lambda b, i: (b, jnp.minimum((i + off + 1) * hb, nhalo - 1), 0)),
            pl.BlockSpec((1, d), lambda b, i: (0, 0)),
            pl.BlockSpec((1, 1, 1, d), lambda b, i: (b, seg(i + off), 0, 0)),
            pl.BlockSpec((1, 1, 1, d), lambda b, i: (b, seg(i + off), 0, 0)),
            pl.BlockSpec((1, 1, 1, d), lambda b, i: (b, seg(i + off), 0, 0)),
            pl.BlockSpec((d, 2 * f), lambda b, i: (0, 0)),
            pl.BlockSpec((3, 2 * f), lambda b, i: (0, 0)),
            pl.BlockSpec((1, 2 * f), lambda b, i: (0, 0)),
            pl.BlockSpec((f, d), lambda b, i: (0, 0)),
            pl.BlockSpec((1, d), lambda b, i: (0, 0)),
        ],
        out_specs=pl.BlockSpec((1, TM, d), lambda b, i: (b, i, 0)),
        out_shape=jax.ShapeDtypeStruct((bsz, lt - off * TM, d), F32),
        scratch_shapes=[
            pltpu.VMEM((TM + FFN_HALO, d), BF16),
            pltpu.VMEM((TM + 2 * FFN_PAD, 2 * f), F32),
        ],
        compiler_params=_cparams("parallel", "parallel"),
        name="conv_ffn",
    )(hs, hs, hs, g.reshape(1, d), sh, sc, gate, w_up.astype(BF16), conv_w, conv_b.reshape(1, 2 * f),
      w_down.astype(BF16), fg)


def _rope_tables(lc, l):
    half = MLA_ROPE // 4
    inv = 1.0 / (ROPE_THETA ** (jnp.arange(half, dtype=F32) / half))
    t = jnp.arange(l)
    ang_r = (t // GRID_W).astype(F32)[:, None] * inv[None, :]
    ang_c = (t % GRID_W).astype(F32)[:, None] * inv[None, :]
    cos = jnp.concatenate([jnp.cos(ang_r), jnp.cos(ang_r), jnp.cos(ang_c), jnp.cos(ang_c)], axis=-1)
    sin = jnp.concatenate([-jnp.sin(ang_r), jnp.sin(ang_r), -jnp.sin(ang_c), jnp.sin(ang_c)], axis=-1)
    pad = ((lc, 0), (MLA_NOPE, LANES - MLA_QK))
    return jnp.pad(cos, pad, constant_values=1.0), jnp.pad(sin, pad)


def _mla_proj_kernel(x_ref, g_ref, sh_ref, sc_ref, win_ref, qg_ref, wq_ref, kvg_ref, wkn_ref, wv_ref,
                     cos_ref, sin_ref, q_ref, k_ref, v_ref):
    u = _norm_mod(x_ref[0], g_ref[...], sh_ref[0, 0], sc_ref[0, 0]).astype(BF16)
    lat = _dot(u, win_ref[...])
    q_lat = lat[:, :MLA_Q_RANK]
    kv_lat = lat[:, MLA_Q_RANK:MLA_Q_RANK + MLA_KV_RANK]
    kr = lat[:, MLA_Q_RANK + MLA_KV_RANK:]

    def rms(t, gg):
        return (t * lax.rsqrt(jnp.mean(t * t, axis=-1, keepdims=True) + EPS) * gg).astype(BF16)

    qn = rms(q_lat, qg_ref[...])
    kvn = rms(kv_lat, kvg_ref[...])
    cos = cos_ref[...]
    sin = sin_ref[...]
    lane = lax.broadcasted_iota(jnp.int32, (TM, LANES), 1)
    first_half = (lane % (MLA_ROPE // 2)) < (MLA_ROPE // 4)

    def rope(t):
        partner = jnp.where(first_half, pltpu.roll(t, LANES - MLA_ROPE // 4, 1), pltpu.roll(t, MLA_ROPE // 4, 1))
        return t * cos + partner * sin

    kr_rot = rope(kr)
    v_all = _dot(kvn, wv_ref[...]).astype(BF16)
    ones = jnp.ones((v_all.shape[0], LANES), BF16)
    for pr in range(MLA_HEADS // 2):
        v_ref[0, :, 2 * pr * LANES:(2 * pr + 1) * LANES] = v_all[:, pr * LANES:(pr + 1) * LANES]
        v_ref[0, :, (2 * pr + 1) * LANES:(2 * pr + 2) * LANES] = ones
    q_all = _dot(qn, wq_ref[...]) * ((MLA_QK ** -0.5) * LOG2E)
    kn_all = _dot(kvn, wkn_ref[...])
    for h in range(MLA_HEADS):
        sl = slice(h * LANES, (h + 1) * LANES)
        q_ref[0, :, sl] = rope(q_all[:, sl]).astype(BF16)
        k_ref[0, :, sl] = (kn_all[:, sl] + kr_rot).astype(BF16)


def _mla_attn_kernel(q_ref, k_ref, v_ref, o_ref, *, ncb, lc):
    i = pl.program_id(2)

    def attend(lk):
        q = q_ref[0]
        lane = lax.broadcasted_iota(jnp.int32, (q.shape[0], LANES), 1)
        for pair in range(q.shape[1] // (2 * LANES)):
            v = v_ref[0, 0:lk, 2 * pair * LANES:(2 * pair + 2) * LANES]
            outs = []
            for e in range(2):
                sl = slice((2 * pair + e) * LANES, (2 * pair + e + 1) * LANES)
                s = _dot_nt(q[:, sl], k_ref[0, 0:lk, sl])
                m = jnp.max(s, axis=-1, keepdims=True)
                p = jnp.exp2(s - m)
                pvl = _dot(p.astype(BF16), v)
                outs.append(pvl[:, :LANES] / pvl[:, LANES:])
            o_ref[0, :, pair * LANES:(pair + 1) * LANES] = jnp.where(lane < MLA_V, outs[0], outs[1]).astype(o_ref.dtype)

    @pl.when(i < ncb)
    def _():
        attend(lc)

    @pl.when(i >= ncb)
    def _():
        attend(k_ref.shape[1])


def _mla_layer(hs, g, sh, sc, gate, w_in, q_norm_g, w_q_up, kv_norm_g, w_kv_up, w_out, *, ncb, lc):
    bsz, lt, d = hs.shape
    l = lt - lc
    nh = MLA_HEADS
    seg = _seg(ncb)
    n_lat = MLA_Q_RANK + MLA_KV_RANK
    w_in_p = jnp.concatenate([w_in[:, :n_lat], jnp.zeros((d, MLA_NOPE), w_in.dtype), w_in[:, n_lat:],
                              jnp.zeros((d, LANES - MLA_QK), w_in.dtype)], axis=1)
    wq = w_q_up.reshape(MLA_Q_RANK, nh, MLA_QK)
    wq = jnp.pad(wq, ((0, 0), (0, 0), (0, LANES - MLA_QK))).reshape(MLA_Q_RANK, nh * LANES)
    wkv = w_kv_up.reshape(MLA_KV_RANK, nh, MLA_NOPE + MLA_V)
    wkn = jnp.pad(wkv[:, :, :MLA_NOPE], ((0, 0), (0, 0), (0, LANES - MLA_NOPE))).reshape(MLA_KV_RANK, nh * LANES)
    wv = wkv[:, :, MLA_NOPE:].reshape(MLA_KV_RANK, nh * MLA_V)
    cos_t, sin_t = _rope_tables(lc, l)

    q, k, v = pl.pallas_call(
        _mla_proj_kernel,
        grid=(bsz, lt // TM),
        in_specs=[
            pl.BlockSpec((1, TM, d), lambda b, i: (b, i, 0)),
            pl.BlockSpec((1, d), lambda b, i: (0, 0)),
            pl.BlockSpec((1, 1, 1, d), lambda b, i: (b, seg(i), 0, 0)),
            pl.BlockSpec((1, 1, 1, d), lambda b, i: (b, seg(i), 0, 0)),
            pl.BlockSpec((d, 768), lambda b, i: (0, 0)),
            pl.BlockSpec((1, MLA_Q_RANK), lambda b, i: (0, 0)),
            pl.BlockSpec((MLA_Q_RANK, nh * LANES), lambda b, i: (0, 0)),
            pl.BlockSpec((1, MLA_KV_RANK), lambda b, i: (0, 0)),
            pl.BlockSpec((MLA_KV_RANK, nh * LANES), lambda b, i: (0, 0)),
            pl.BlockSpec((MLA_KV_RANK, nh * MLA_V), lambda b, i: (0, 0)),
            pl.BlockSpec((TM, LANES), lambda b, i: (i, 0)),
            pl.BlockSpec((TM, LANES), lambda b, i: (i, 0)),
        ],
        out_specs=[
            pl.BlockSpec((1, TM, nh * LANES), lambda b, i: (b, i, 0)),
            pl.BlockSpec((1, TM, nh * LANES), lambda b, i: (b, i, 0)),
            pl.BlockSpec((1, TM, nh * LANES), lambda b, i: (b, i, 0)),
        ],
        out_shape=[
            jax.ShapeDtypeStruct((bsz, lt, nh * LANES), BF16),
            jax.ShapeDtypeStruct((bsz, lt, nh * LANES), BF16),
            jax.ShapeDtypeStruct((bsz, lt, nh * LANES), BF16),
        ],
        compiler_params=_cparams("parallel", "parallel"),
        name="mla_proj",
    )(hs, g.reshape(1, d), sh, sc, w_in_p.astype(BF16), q_norm_g.reshape(1, -1), wq.astype(BF16),
      kv_norm_g.reshape(1, -1), wkn.astype(BF16), wv.astype(BF16), cos_t, sin_t)

    o = pl.pallas_call(
        functools.partial(_mla_attn_kernel, ncb=ncb, lc=lc),
        grid=(bsz, nh // MLA_HEADS_PER_STEP, lt // TM),
        in_specs=[
            pl.BlockSpec((1, TM, MLA_HEADS_PER_STEP * LANES), lambda b, p, i: (b, i, p)),
            pl.BlockSpec((1, lt, MLA_HEADS_PER_STEP * LANES), lambda b, p, i: (b, 0, p)),
            pl.BlockSpec((1, lt, MLA_HEADS_PER_STEP * LANES), lambda b, p, i: (b, 0, p)),
        ],
        out_specs=pl.BlockSpec((1, TM, MLA_HEADS_PER_STEP * MLA_V), lambda b, p, i: (b, i, p)),
        out_shape=jax.ShapeDtypeStruct((bsz, lt, nh * MLA_V), BF16),
        compiler_params=_cparams("parallel", "parallel", "parallel"),
        name="mla_attn",
    )(q, k, v)
    return _matmul_gate_residual(o, w_out.astype(BF16), hs, gate, ncb=ncb)


def _s5_matrices(lam_re, lam_im, log_dt, b_re, b_im, c_re, c_im):
    t_len = S5_T
    n_grp = lam_re.shape[1]
    ch = S5_GROUP_CH
    n_st = S5_STATE
    dt = jnp.exp(log_dt.astype(F32))[..., None]
    ld_re = lam_re.astype(F32) * dt
    ld_im = lam_im.astype(F32) * dt
    tau = jnp.arange(t_len + 1, dtype=F32)[:, None, None, None]
    mag = jnp.exp(tau * ld_re[None])
    pw_re = mag * jnp.cos(tau * ld_im[None])
    pw_im = mag * jnp.sin(tau * ld_im[None])
    tau_r = jnp.arange(t_len, -1, -1, dtype=F32)[:, None, None, None]
    mag_r = jnp.exp(tau_r * ld_re[None])
    pr_re = mag_r * jnp.cos(tau_r * ld_im[None])
    pr_im = mag_r * jnp.sin(tau_r * ld_im[None])
    lb_re, lb_im = pw_re[1] - 1.0, pw_im[1]
    den = lam_re * lam_re + lam_im * lam_im
    f_re = (lb_re * lam_re + lb_im * lam_im) / den
    f_im = (lb_im * lam_re - lb_re * lam_im) / den
    bb_re = f_re[..., None] * b_re - f_im[..., None] * b_im
    bb_im = f_re[..., None] * b_im + f_im[..., None] * b_re
    cp_re = c_re[None] * pw_re[:t_len, :, :, None, :] - c_im[None] * pw_im[:t_len, :, :, None, :]
    cp_im = c_re[None] * pw_im[:t_len, :, :, None, :] + c_im[None] * pw_re[:t_len, :, :, None, :]
    taps = (jnp.einsum("tdgon,dgni->tdgoi", cp_re, bb_re, precision=HI)
            - jnp.einsum("tdgon,dgni->tdgoi", cp_im, bb_im, precision=HI))
    s_idx = np.arange(t_len)[:, None, None]
    t_idx = np.arange(t_len)[None, :, None]
    u_idx = np.arange(t_len)[None, None, :]
    sel_f = jnp.asarray((t_idx - s_idx == u_idx).astype(np.float32))
    sel_b = jnp.asarray((s_idx - t_idx == u_idx).astype(np.float32))
    kf = jnp.einsum("stu,ugoi->stgoi", sel_f, taps[:, 0], precision=HI)
    kb = jnp.einsum("stu,ugoi->stgoi", sel_b, taps[:, 1], precision=HI)
    k_tot = (kf + kb).transpose(2, 0, 4, 1, 3).reshape(n_grp, t_len * ch, t_len * ch)

    def state_in(pw_r, pw_i, brr, bii):
        re = pw_r[..., None] * brr[None] - pw_i[..., None] * bii[None]
        im = pw_r[..., None] * bii[None] + pw_i[..., None] * brr[None]
        return (re.transpose(1, 0, 3, 2).reshape(n_grp, t_len * ch, n_st),
                im.transpose(1, 0, 3, 2).reshape(n_grp, t_len * ch, n_st))

    inf_re, inf_im = state_in(pr_re[1:, 0], pr_im[1:, 0], bb_re[0], bb_im[0])
    inb_re, inb_im = state_in(pw_re[:t_len, 1], pw_im[:t_len, 1], bb_re[1], bb_im[1])

    def state_out(pw_r, pw_i, crr, cii):
        re = crr[None] * pw_r[:, :, None, :] - cii[None] * pw_i[:, :, None, :]
        im = crr[None] * pw_i[:, :, None, :] + cii[None] * pw_r[:, :, None, :]
        return (re.transpose(1, 3, 0, 2).reshape(n_grp, n_st, t_len * ch),
                (-im).transpose(1, 3, 0, 2).reshape(n_grp, n_st, t_len * ch))

    outf_re, outf_im = state_out(pw_re[1:, 0], pw_im[1:, 0], c_re[0], c_im[0])
    outb_re, outb_im = state_out(pr_re[:t_len, 1], pr_im[:t_len, 1], c_re[1], c_im[1])
    nq = n_grp // S5_QGRP
    npair = n_grp // 2
    cols = t_len * ch
    x = jnp.stack([inf_re, inf_im, inb_re, inb_im], 0).reshape(4, npair, 2, cols, n_st)
    x = x.transpose(1, 2, 3, 0, 4)
    zx = jnp.zeros_like(x[:, 0])
    w_pair = jnp.concatenate([jnp.concatenate([x[:, 0], zx], -1), jnp.concatenate([zx, x[:, 1]], -1)], 1)
    w_pair = w_pair.reshape(nq, S5_QGRP // 2, 2 * cols, 4 * 2 * n_st)
    y = jnp.stack([outf_re, outf_im, outb_re, outb_im], 0).reshape(4, npair, 2, n_st, cols)
    y = y.transpose(1, 0, 2, 3, 4)
    zy = jnp.zeros_like(y[:, :, 0])
    m_pair = jnp.stack([jnp.concatenate([y[:, :, 0], zy], -1), jnp.concatenate([zy, y[:, :, 1]], -1)], 2)
    m_pair = m_pair.reshape(nq, S5_QGRP // 2, 4 * 2 * n_st, 2 * cols)
    decay = jnp.stack([pw_re[t_len, 0], pw_im[t_len, 0], pw_re[t_len, 1], pw_im[t_len, 1]], 0)
    decay = decay.reshape(4, nq, S5_QGRP * n_st).transpose(1, 0, 2).reshape(nq, 1, 4 * S5_QGRP * n_st)
    k_grp = k_tot.reshape(nq, S5_QGRP, cols, cols)
    return w_pair.astype(BF16), k_grp.astype(BF16), m_pair.astype(BF16), decay


def _s5_permutation():
    n = S5_T * LANES
    src = jnp.arange(n)
    t, g, c = src // LANES, (src % LANES) // S5_GROUP_CH, src % S5_GROUP_CH
    dst = g * (S5_T * S5_GROUP_CH) + t * S5_GROUP_CH + c
    return (dst[:, None] == jnp.arange(n)[None, :]).astype(BF16)


def _s5_chunk_rows(z_ref):
    jt = z_ref.shape[1] // S5_T
    return jnp.concatenate([z_ref[0, pl.ds(t, jt, stride=S5_T), :].astype(BF16) for t in range(S5_T)], axis=1)


def _s5_core_kernel(z_ref, perm_ref, win_ref, kg_ref, mout_ref, a_ref, y_ref, u_scr, p_scr, *, jc, jt):
    sc = a_ref.shape[2] // 4
    npair = win_ref.shape[1]
    pc = win_ref.shape[2]
    gc = pc // 2
    zg = _dot(_s5_chunk_rows(z_ref), perm_ref[...]).astype(BF16)
    for pp in range(npair):
        res = _dot(zg[:, pp * pc:(pp + 1) * pc], win_ref[0, pp])
        for k in range(4):
            u_scr[:, k * sc + pp * LANES:k * sc + (pp + 1) * LANES] = res[:, k * LANES:(k + 1) * LANES]
    a = a_ref[0]
    afr, afi, abr, abi = (a[:, k * sc:(k + 1) * sc] for k in range(4))
    zero = jnp.zeros((1, sc), F32)

    def body(jj, carry):
        fr, fi, br, bi = carry
        jb = jnp.where(jj < jc, jc - 1 - jj, jt - 1 - jj + jc)
        p_scr[pl.ds(jj, 1), 0:sc] = fr
        p_scr[pl.ds(jj, 1), sc:2 * sc] = fi
        p_scr[pl.ds(jb, 1), 2 * sc:3 * sc] = br
        p_scr[pl.ds(jb, 1), 3 * sc:4 * sc] = bi
        nfr = afr * fr - afi * fi + u_scr[pl.ds(jj, 1), 0:sc]
        nfi = afr * fi + afi * fr + u_scr[pl.ds(jj, 1), sc:2 * sc]
        nbr = abr * br - abi * bi + u_scr[pl.ds(jb, 1), 2 * sc:3 * sc]
        nbi = abr * bi + abi * br + u_scr[pl.ds(jb, 1), 3 * sc:4 * sc]
        return nfr, nfi, nbr, nbi

    lax.fori_loop(0, jt, body, (zero, zero, zero, zero), unroll=S5_SCAN_UNROLL)
    pieces = []
    for pp in range(npair):
        p_in = jnp.concatenate([p_scr[:, k * sc + pp * LANES:k * sc + (pp + 1) * LANES] for k in range(4)], axis=1)
        ys = _dot(p_in.astype(BF16), mout_ref[0, pp])
        for e in range(2):
            g = 2 * pp + e
            yg = ys[:, e * gc:(e + 1) * gc] + _dot(zg[:, g * gc:(g + 1) * gc], kg_ref[0, g])
            pieces.append(yg.astype(BF16))
    acc = _dot_nt(jnp.concatenate(pieces, axis=1), perm_ref[...])
    for t in range(S5_T):
        y_ref[0, pl.ds(t, jt, stride=S5_T), :] = acc[:, t * LANES:(t + 1) * LANES]


def _s5_glu_kernel(y_ref, z_ref, d_ref, w_ref, x_ref, gate_ref, o_ref):
    d = x_ref.shape[2]
    y = y_ref[0] + z_ref[0] * d_ref[...]
    ge = jax.nn.gelu(y).astype(BF16)
    ag = _dot(ge, w_ref[...])
    o_ref[0] = x_ref[0] + gate_ref[0, 0] * (ag[:, :d] * _sigmoid(ag[:, d:]))


def _s5_layer(hs, g, sh, sc, gate, w_in, lam_re, lam_im, log_dt, b_re, b_im, c_re, c_im, d_skip, w_glu,
              *, ncb, lc):
    bsz, lt, d = hs.shape
    width = w_in.shape[1]
    nq = width // LANES
    jt = lt // S5_T
    jc = lc // S5_T
    ccols = S5_T * LANES
    scols = 4 * S5_QGRP * S5_STATE
    seg = _seg(ncb)
    z = _norm_mod_matmul(hs, g, sh, sc, w_in.astype(BF16), jnp.ones((1, width), F32),
                         ncb=ncb, out_dtype=F32, tn=width)
    w_pair, k_grp, m_pair, decay = _s5_matrices(lam_re, lam_im, log_dt, b_re, b_im, c_re, c_im)

    def per_quarter(arr):
        return pl.BlockSpec((1,) + arr.shape[1:], lambda q, b: (q, 0, 0, 0))

    y = pl.pallas_call(
        functools.partial(_s5_core_kernel, jc=jc, jt=jt),
        grid=(nq, bsz),
        in_specs=[
            pl.BlockSpec((1, lt, LANES), lambda q, b: (b, 0, q)),
            pl.BlockSpec((ccols, ccols), lambda q, b: (0, 0), pipeline_mode=pl.Buffered(1)),
            per_quarter(w_pair),
            per_quarter(k_grp),
            per_quarter(m_pair),
            pl.BlockSpec((1, 1, scols), lambda q, b: (q, 0, 0)),
        ],
        out_specs=pl.BlockSpec((1, lt, LANES), lambda q, b: (b, 0, q)),
        out_shape=jax.ShapeDtypeStruct((bsz, lt, width), F32),
        scratch_shapes=[pltpu.VMEM((jt, scols), F32), pltpu.VMEM((jt, scols), F32)],
        compiler_params=_cparams("parallel", "parallel"),
        name="s5_core",
    )(z, _s5_permutation(), w_pair, k_grp, m_pair, decay)
    return pl.pallas_call(
        _s5_glu_kernel,
        grid=(bsz, lt // TM),
        in_specs=[
            pl.BlockSpec((1, TM, width), lambda b, i: (b, i, 0)),
            pl.BlockSpec((1, TM, width), lambda b, i: (b, i, 0)),
            pl.BlockSpec((1, width), lambda b, i: (0, 0)),
            pl.BlockSpec((width, 2 * d), lambda b, i: (0, 0)),
            pl.BlockSpec((1, TM, d), lambda b, i: (b, i, 0)),
            pl.BlockSpec((1, 1, 1, d), lambda b, i: (b, seg(i), 0, 0)),
        ],
        out_specs=pl.BlockSpec((1, TM, d), lambda b, i: (b, i, 0)),
        out_shape=jax.ShapeDtypeStruct((bsz, lt, d), F32),
        compiler_params=_cparams("parallel", "parallel"),
        name="s5_glu",
    )(y, z, d_skip.reshape(1, width), w_glu.astype(BF16), hs, gate)


def _gla_dir(zq, zf, zv, lb, states, incl, tri, rev):
    dk = HG_HEAD_DIM
    width = zq.shape[1]
    nchunks = zq.shape[0] // HG_CHUNK
    forget = lb + (1.0 - lb) * _sigmoid(zf)
    lf = jnp.log(forget)
    kk = 1.0 - forget
    v_all = zv.astype(BF16)
    hi = lf.astype(BF16)
    r1 = lf - hi.astype(F32)
    mid = r1.astype(BF16)
    lo = (r1 - mid.astype(F32)).astype(BF16)
    parts = _dot(tri, jnp.concatenate([hi, mid, lo], axis=1))
    bcum = parts[:, :width] + parts[:, width:2 * width] + parts[:, 2 * width:]
    btot = jnp.concatenate(
        [jnp.broadcast_to(bcum[c * HG_CHUNK:c * HG_CHUNK + 1] if rev else bcum[(c + 1) * HG_CHUNK - 1:(c + 1) * HG_CHUNK],
                          (HG_CHUNK, width)) for c in range(nchunks)], axis=0)
    q_in_all = (zq * _sigmoid(zq)) * (dk ** -0.5) * jnp.exp(bcum)
    k_in_all = (kk * jnp.exp(-bcum)).astype(BF16)
    k_out_all = kk * jnp.exp(btot - bcum)
    chunk_of_row = lax.broadcasted_iota(jnp.int32, (zq.shape[0], dk), 0) // HG_CHUNK
    zero = jnp.zeros((zq.shape[0], dk), F32)
    outs, new_states = [], []
    for h, st in enumerate(states):
        sl = slice(h * dk, (h + 1) * dk)
        q_in, k_out, v = q_in_all[:, sl], k_out_all[:, sl], v_all[:, sl]
        att = jnp.where(incl, _dot_nt(q_in.astype(BF16), k_in_all[:, sl]), 0.0)
        o_intra = _dot(att.astype(BF16), v)
        k_cat = jnp.concatenate([jnp.where(chunk_of_row == c, k_out, zero) for c in range(nchunks)], axis=1)
        q_cat = jnp.concatenate([jnp.where(chunk_of_row == c, q_in, zero) for c in range(nchunks)], axis=1)
        ds = _dot_tn(v, k_cat.astype(BF16))
        entering = [None] * nchunks
        for cc in range(nchunks):
            c = nchunks - 1 - cc if rev else cc
            entering[c] = st
            st = st * jnp.exp(btot[c * HG_CHUNK:c * HG_CHUNK + 1, sl]) + ds[:, c * dk:(c + 1) * dk]
        scat = jnp.concatenate(entering, axis=1).astype(BF16)
        outs.append(o_intra + _dot_nt(q_cat.astype(BF16), scat))
        new_states.append(st)
    return outs, new_states


def _gla_kernel(zqf_ref, zff_ref, zvf_ref, zqb_ref, zfb_ref, zvb_ref, lb_ref, of_ref, ob_ref, st_scr):
    @pl.when(pl.program_id(2) == 0)
    def _():
        st_scr[...] = jnp.zeros_like(st_scr)

    n = zqf_ref.shape[1]
    row = lax.broadcasted_iota(jnp.int32, (n, n), 0)
    col = lax.broadcasted_iota(jnp.int32, (n, n), 1)
    same = (row // HG_CHUNK) == (col // HG_CHUNK)
    dk = HG_HEAD_DIM
    for direction, (zq_ref, zf_ref, zv_ref, o_ref) in enumerate(
            ((zqf_ref, zff_ref, zvf_ref, of_ref), (zqb_ref, zfb_ref, zvb_ref, ob_ref))):
        rev = direction == 1
        incl = same & ((col >= row) if rev else (col <= row))
        tri = incl.astype(BF16)
        nheads = zq_ref.shape[2] // dk
        outs, states = _gla_dir(zq_ref[0].astype(F32), zf_ref[0], zv_ref[0], lb_ref[0],
                                [st_scr[direction, h] for h in range(nheads)], incl, tri, rev)
        for h in range(nheads):
            o_ref[0, :, h * dk:(h + 1) * dk] = outs[h].astype(o_ref.dtype)
            st_scr[direction, h] = states[h]


def _hg_out_kernel(of_ref, ob_ref, zg_ref, ng_ref, w_ref, x_ref, gate_ref, o_ref):
    o = of_ref[0].astype(F32) + ob_ref[0].astype(F32)
    gsig = zg_ref[0].astype(F32)
    gsig = gsig * _sigmoid(gsig)
    ng = ng_ref[...]
    parts = []
    for h in range(o.shape[1] // HG_HEAD_DIM):
        sl = slice(h * HG_HEAD_DIM, (h + 1) * HG_HEAD_DIM)
        oh = o[:, sl]
        on = oh * lax.rsqrt(jnp.mean(oh * oh, axis=-1, keepdims=True) + EPS)
        parts.append((on * ng[:, sl] * gsig[:, sl]).astype(BF16))
    a = jnp.concatenate(parts, axis=-1)
    o_ref[0] = x_ref[0] + gate_ref[0, 0] * _dot(a, w_ref[...])


def _hgrn2_layer(hs, g, sh, sc, gate, w_in, lower_bound, norm_g, w_out, *, ncb, lc):
    bsz, lt, d = hs.shape
    nh = d // HG_HEAD_DIM
    nb = lt // TM
    seg = _seg(ncb)
    w_r = jnp.concatenate([w_in[:, d:3 * d], w_in[:, :d], w_in[:, 3 * d:]], axis=1).astype(BF16)
    zf, zr = _norm_mod_matmul(hs, g, sh, sc, w_r, jnp.ones((1, 5 * d), F32),
                              ncb=ncb, out_dtype=((F32, 2 * d), (BF16, 3 * d)), tn=1024)
    lb = lower_bound.astype(F32).reshape(1, 1, d)
    hps = HG_HEADS_PER_STEP
    ng = nh // hps
    wb = hps * HG_HEAD_DIM

    def rblk(s):
        return jnp.where(s < ncb, ncb - 1 - s, nb - 1 - s + ncb)

    outs = pl.pallas_call(
        _gla_kernel,
        grid=(bsz, ng, nb),
        in_specs=[
            pl.BlockSpec((1, TM, wb), lambda b, h, s: (b, s, h)),
            pl.BlockSpec((1, TM, wb), lambda b, h, s: (b, s, h)),
            pl.BlockSpec((1, TM, wb), lambda b, h, s: (b, s, ng + h)),
            pl.BlockSpec((1, TM, wb), lambda b, h, s: (b, rblk(s), h)),
            pl.BlockSpec((1, TM, wb), lambda b, h, s: (b, rblk(s), ng + h)),
            pl.BlockSpec((1, TM, wb), lambda b, h, s: (b, rblk(s), ng + h)),
            pl.BlockSpec((1, 1, wb), lambda b, h, s: (0, 0, h)),
        ],
        out_specs=[
            pl.BlockSpec((1, TM, wb), lambda b, h, s: (b, s, h)),
            pl.BlockSpec((1, TM, wb), lambda b, h, s: (b, rblk(s), h)),
        ],
        out_shape=[jax.ShapeDtypeStruct((bsz, lt, d), BF16), jax.ShapeDtypeStruct((bsz, lt, d), BF16)],
        scratch_shapes=[pltpu.VMEM((2, hps, HG_HEAD_DIM, HG_HEAD_DIM), F32)],
        compiler_params=_cparams("parallel", "parallel", "arbitrary"),
        name="hgrn2_gla",
    )(zr, zf, zr, zr, zf, zr, lb)
    return pl.pallas_call(
        _hg_out_kernel,
        grid=(bsz, nb),
        in_specs=[
            pl.BlockSpec((1, TM, d), lambda b, i: (b, i, 0)),
            pl.BlockSpec((1, TM, d), lambda b, i: (b, i, 0)),
            pl.BlockSpec((1, TM, d), lambda b, i: (b, i, 2)),
            pl.BlockSpec((1, d), lambda b, i: (0, 0)),
            pl.BlockSpec((d, d), lambda b, i: (0, 0)),
            pl.BlockSpec((1, TM, d), lambda b, i: (b, i, 0)),
            pl.BlockSpec((1, 1, 1, d), lambda b, i: (b, seg(i), 0, 0)),
        ],
        out_specs=pl.BlockSpec((1, TM, d), lambda b, i: (b, i, 0)),
        out_shape=jax.ShapeDtypeStruct((bsz, lt, d), F32),
        compiler_params=_cparams("parallel", "parallel"),
        name="hgrn2_out",
    )(outs[0], outs[1], zr, norm_g.reshape(1, d), w_out.astype(BF16), hs, gate)


def _na_bias_table(rpb):
    w = np.arange(GRID_W)[:, None, None]
    kc = np.arange(GRID_W)[None, :, None]
    co = np.arange(2 * NA_KW - 1)[None, None, :]
    c0 = np.clip(w - NA_KW // 2, 0, GRID_W - NA_KW)
    valid = (kc >= c0) & (kc < c0 + NA_KW)
    onehot = jnp.asarray((valid & (kc - w + (NA_KW - 1) == co)).astype(np.float32))
    t = jnp.einsum("hrc,wkc->hrwk", rpb.astype(F32), onehot, precision=HI) * LOG2E
    t = jnp.where(jnp.asarray(valid[None, None, :, :, 0]), t, NEG_BIG)
    ext = jnp.pad(t, ((0, 0), (NA_KH, NA_KH), (0, 0), (0, 0)), constant_values=NEG_BIG)
    return jnp.concatenate([ext[:, :-1], ext[:, 1:]], axis=-1)


def _na_kernel(q_ref, k_ref, v_ref, bias_ref, o_ref, *, lc, rows):
    nblk = rows // NA_BLK_ROWS
    nq = NA_BLK_ROWS * GRID_W
    nwin = NA_WIN_ROWS * GRID_W
    o_ref[0, 0:lc, :] = jnp.zeros((lc, LANES), o_ref.dtype)
    kc = k_ref[0, 0:lc, :]
    vc = v_ref[0, 0:lc, :]
    lane = lax.broadcasted_iota(jnp.int32, (nq, LANES), 1)
    lo_half = lane < NA_HEAD_DIM
    key_lane = lax.broadcasted_iota(jnp.int32, (GRID_W, nwin), 1)
    half = NA_KH // 2

    def one_block(i):
        wr0 = jnp.clip(i * NA_BLK_ROWS - half, 0, rows - NA_WIN_ROWS)
        first, last = i == 0, i == nblk - 1
        delta0 = jnp.where(first, NA_KH - 1, jnp.where(last, -1, NA_KH - 1 - half))
        qoff = pl.multiple_of(lc + i * nq, GRID_W * NA_KH // 2)
        woff = pl.multiple_of(lc + wr0 * GRID_W, GRID_W * NA_KH // 2)
        q = q_ref[0, pl.ds(qoff, nq), :]
        kw = k_ref[0, pl.ds(woff, nwin), :]
        vw = v_ref[0, pl.ds(woff, nwin), :]
        zq = jnp.zeros_like(q)
        outs = []
        for e in range(2):
            qe = jnp.where(lo_half, q, zq) if e == 0 else jnp.where(lo_half, zq, q)
            s_all = _dot_nt(qe, kw)
            slabs = []
            for j in range(NA_BLK_ROWS):
                a = jnp.where(first, max(j - half, 0), jnp.where(last, min(j + half, NA_KH), j))
                valid = (key_lane >= a * GRID_W) & (key_lane < (a + NA_KH) * GRID_W)
                bias = jnp.concatenate([bias_ref[e, delta0 - j + NA_KH + 2 * m] for m in range(NA_WIN_ROWS // 2)],
                                       axis=1)
                slabs.append(jnp.where(valid, s_all[j * GRID_W:(j + 1) * GRID_W] + bias, NEG_BIG))
            s_loc = jnp.concatenate(slabs, axis=0)
            s_ctx = _dot_nt(qe, kc)
            m = jnp.maximum(jnp.max(s_loc, axis=-1, keepdims=True), jnp.max(s_ctx, axis=-1, keepdims=True))
            p_loc = jnp.exp2(s_loc - m)
            p_ctx = jnp.exp2(s_ctx - m)
            l = jnp.sum(p_loc, axis=-1, keepdims=True) + jnp.sum(p_ctx, axis=-1, keepdims=True)
            outs.append((_dot(p_loc.astype(BF16), vw) + _dot(p_ctx.astype(BF16), vc)) / l)
        o_ref[0, pl.ds(qoff, nq), :] = jnp.where(lo_half, outs[0], outs[1]).astype(o_ref.dtype)

    def body(ii, carry):
        for j in range(NA_BLK_UNROLL):
            one_block(ii * NA_BLK_UNROLL + j)
        return carry

    lax.fori_loop(0, nblk // NA_BLK_UNROLL, body, 0)


def _natten_layer(hs, g, sh, sc, gate, w_qkv, rpb, w_out, *, ncb, lc):
    bsz, lt, d = hs.shape
    rows = (lt - lc) // GRID_W
    assert rows >= NA_WIN_ROWS and rows % (NA_BLK_ROWS * NA_BLK_UNROLL) == 0
    npair = NA_HEADS // 2
    col_scale = jnp.concatenate([jnp.full((1, d), (NA_HEAD_DIM ** -0.5) * LOG2E, F32),
                                 jnp.ones((1, 2 * d), F32)], axis=-1)
    qkv = _norm_mod_matmul(hs, g, sh, sc, w_qkv.astype(BF16), col_scale, ncb=ncb, out_dtype=BF16, tn=1024)
    bias = _na_bias_table(rpb)
    o = pl.pallas_call(
        functools.partial(_na_kernel, lc=lc, rows=rows),
        grid=(npair, bsz),
        in_specs=[
            pl.BlockSpec((1, lt, LANES), lambda p, b: (b, 0, p)),
            pl.BlockSpec((1, lt, LANES), lambda p, b: (b, 0, npair + p)),
            pl.BlockSpec((1, lt, LANES), lambda p, b: (b, 0, 2 * npair + p)),
            pl.BlockSpec((2, NA_TABLE_PAIRS, GRID_W, 2 * GRID_W), lambda p, b: (p, 0, 0, 0)),
        ],
        out_specs=pl.BlockSpec((1, lt, LANES), lambda p, b: (b, 0, p)),
        out_shape=jax.ShapeDtypeStruct((bsz, lt, d), BF16),
        compiler_params=_cparams("parallel", "parallel"),
        name="natten",
    )(qkv, qkv, qkv, bias)
    return _matmul_gate_residual(o, w_out.astype(BF16), hs, gate, ncb=ncb)


def _layer_mods(mod_rows, bsz, d):
    mx = mod_rows[:bsz].reshape(bsz, 6, d)
    mc = jnp.broadcast_to(mod_rows[bsz].reshape(1, 6, d), (bsz, 6, d))
    m = jnp.stack([mc, mx], axis=1)
    return [m[:, :, k, None, :] for k in range(6)]


def kernel(x, c, ctx, c_ctx, ada_w, ada_b, norm1_g, norm2_g, mla_w_in, mla_q_norm_g, mla_w_q_up, mla_kv_norm_g, mla_w_kv_up, mla_w_out, s5_w_in, s5_lambda_re, s5_lambda_im, s5_log_dt, s5_b_re, s5_b_im, s5_c_re, s5_c_im, s5_d, s5_w_glu, hg_w_in, hg_lower_bound, hg_norm_g, hg_w_out, na_w_qkv, na_rpb, na_w_out, ffn_w_up, ffn_conv_w, ffn_conv_b, ffn_w_down, final_g):
    bsz, l, d = x.shape
    lc = ctx.shape[1]
    depth = ada_w.shape[0]
    assert lc % TM == 0 and l % TM == 0 and l % GRID_W == 0 and bsz + 1 <= 16
    ncb = lc // TM
    hs = jnp.concatenate([ctx, x], axis=1)
    cond_rows = jnp.concatenate([c, c_ctx[None], jnp.zeros((16 - bsz - 1, d), F32)], axis=0)
    mod_all = _ada_mod(cond_rows, ada_w, ada_b)
    lb_cum = jnp.cumsum(jax.nn.softmax(hg_lower_bound.astype(F32), axis=0), axis=0)
    lower_bounds = lb_cum - lb_cum[0]
    for i in range(depth):
        kind, j = i % 4, i // 4
        sh1, sc1, g1, sh2, sc2, g2 = _layer_mods(mod_all[i], bsz, d)
        if kind == 0:
            hs = _mla_layer(hs, norm1_g[i], sh1, sc1, g1, mla_w_in[j], mla_q_norm_g[j], mla_w_q_up[j],
                            mla_kv_norm_g[j], mla_w_kv_up[j], mla_w_out[j], ncb=ncb, lc=lc)
        elif kind == 1:
            hs = _s5_layer(hs, norm1_g[i], sh1, sc1, g1, s5_w_in[j], s5_lambda_re[j], s5_lambda_im[j],
                           s5_log_dt[j], s5_b_re[j], s5_b_im[j], s5_c_re[j], s5_c_im[j], s5_d[j], s5_w_glu[j],
                           ncb=ncb, lc=lc)
        elif kind == 2:
            hs = _hgrn2_layer(hs, norm1_g[i], sh1, sc1, g1, hg_w_in[j], lower_bounds[i], hg_norm_g[j],
                              hg_w_out[j], ncb=ncb, lc=lc)
        else:
            hs = _natten_layer(hs, norm1_g[i], sh1, sc1, g1, na_w_qkv[j], na_rpb[j], na_w_out[j],
                               ncb=ncb, lc=lc)
        hs = _conv_ffn(hs, norm2_g[i], sh2, sc2, g2, ffn_w_up[i], ffn_conv_w[i], ffn_conv_b[i],
                       ffn_w_down[i], final_g if i == depth - 1 else None, ncb=ncb)
    return hs
```

```python
import functools
import math

import jax
import jax.numpy as jnp
import numpy as np
from jax import lax
from jax.experimental import pallas as pl
from jax.experimental.pallas import tpu as pltpu

F32 = jnp.float32
BF16 = jnp.bfloat16
HI = lax.Precision.HIGHEST

EPS = 1e-6
GRID_W = 64
ROPE_THETA = 10000.0
LOG2E = math.log2(math.e)

LANES = 128
VMEM_LIMIT_BYTES = 56 * 1024 * 1024
TM = 256

MLA_HEADS = 16
MLA_Q_RANK = 384
MLA_KV_RANK = 256
MLA_NOPE = 64
MLA_ROPE = 32
MLA_V = 64
MLA_QK = MLA_NOPE + MLA_ROPE
MLA_HEADS_PER_STEP = 8

S5_GROUP_CH = 16
S5_STATE = 64
S5_T = 16
S5_SCAN_UNROLL = 4
S5_QGRP = LANES // S5_GROUP_CH

HG_HEAD_DIM = 128
HG_CHUNK = 64
HG_HEADS_PER_STEP = 8

NA_HEADS = 16
NA_HEAD_DIM = 64
NA_KH = 8
NA_KW = 16
NEG_BIG = -1e30
NA_BLK_ROWS = 8
NA_WIN_ROWS = 16
NA_TABLE_PAIRS = 4 * NA_KH - 2
NA_BLK_UNROLL = 4


def _cparams(*sem):
    return pltpu.CompilerParams(dimension_semantics=sem, vmem_limit_bytes=VMEM_LIMIT_BYTES)


def _seg(ncb):
    return lambda i: jnp.where(i >= ncb, 1, 0)


def _norm_mod(x, g, sh, sc):
    ms = jnp.mean(x * x, axis=-1, keepdims=True)
    return (x * lax.rsqrt(ms + EPS) * g) * (1.0 + sc) + sh


def _sigmoid(x):
    return 1.0 / (1.0 + jnp.exp(-x))


def _dot(a, b):
    return jnp.dot(a, b, preferred_element_type=F32)


def _dot_nt(a, b):
    return lax.dot_general(a, b, (((1,), (1,)), ((), ())), preferred_element_type=F32)


def _dot_tn(a, b):
    return lax.dot_general(a, b, (((0,), (0,)), ((), ())), preferred_element_type=F32)


def _ada_kernel(cond_ref, w_ref, b_ref, o_ref):
    cond = cond_ref[...]
    a = (cond * _sigmoid(cond)).astype(BF16)
    o_ref[0] = _dot(a, w_ref[0].astype(BF16)) + b_ref[0]


def _ada_mod(cond_rows, ada_w, ada_b):
    depth, d, n = ada_w.shape
    rows = cond_rows.shape[0]
    tn = 1536
    return pl.pallas_call(
        _ada_kernel,
        grid=(depth, n // tn),
        in_specs=[
            pl.BlockSpec((rows, d), lambda i, j: (0, 0)),
            pl.BlockSpec((1, d, tn), lambda i, j: (i, 0, j)),
            pl.BlockSpec((1, 1, tn), lambda i, j: (i, 0, j)),
        ],
        out_specs=pl.BlockSpec((1, rows, tn), lambda i, j: (i, 0, j)),
        out_shape=jax.ShapeDtypeStruct((depth, rows, n), F32),
        compiler_params=_cparams("parallel", "parallel"),
        name="ada_mod",
    )(cond_rows, ada_w, ada_b.reshape(depth, 1, n))


def _nmm_kernel(x_ref, g_ref, sh_ref, sc_ref, w_ref, cs_ref, *o_refs, tn):
    u = _norm_mod(x_ref[0], g_ref[...], sh_ref[0, 0], sc_ref[0, 0]).astype(BF16)
    start = 0
    for o_ref in o_refs:
        for j in range(o_ref.shape[2] // tn):
            sl = slice(start + j * tn, start + (j + 1) * tn)
            o_ref[0, :, j * tn:(j + 1) * tn] = (_dot(u, w_ref[:, sl]) * cs_ref[:, sl]).astype(o_ref.dtype)
        start += o_ref.shape[2]


def _norm_mod_matmul(hs, g, sh, sc, w, col_scale, *, ncb, out_dtype, tn):
    bsz, lt, d = hs.shape
    n = w.shape[1]
    seg = _seg(ncb)
    multi = isinstance(out_dtype, tuple)
    parts = out_dtype if multi else ((out_dtype, n),)
    assert sum(c for _, c in parts) == n
    outs = pl.pallas_call(
        functools.partial(_nmm_kernel, tn=tn),
        grid=(bsz, lt // TM),
        in_specs=[
            pl.BlockSpec((1, TM, d), lambda b, i: (b, i, 0)),
            pl.BlockSpec((1, d), lambda b, i: (0, 0)),
            pl.BlockSpec((1, 1, 1, d), lambda b, i: (b, seg(i), 0, 0)),
            pl.BlockSpec((1, 1, 1, d), lambda b, i: (b, seg(i), 0, 0)),
            pl.BlockSpec((d, n), lambda b, i: (0, 0)),
            pl.BlockSpec((1, n), lambda b, i: (0, 0)),
        ],
        out_specs=[pl.BlockSpec((1, TM, c), lambda b, i: (b, i, 0)) for _, c in parts],
        out_shape=[jax.ShapeDtypeStruct((bsz, lt, c), dt) for dt, c in parts],
        compiler_params=_cparams("parallel", "parallel"),
        name="norm_mod_matmul",
    )(hs, g.reshape(1, d), sh, sc, w, col_scale)
    return tuple(outs) if multi else outs[0]


def _mgr_kernel(a_ref, w_ref, x_ref, gate_ref, o_ref):
    o_ref[0] = x_ref[0] + gate_ref[0, 0] * _dot(a_ref[0], w_ref[...])


def _matmul_gate_residual(a, w, hs, gate, *, ncb):
    bsz, lt, d = hs.shape
    k = a.shape[2]
    seg = _seg(ncb)
    return pl.pallas_call(
        _mgr_kernel,
        grid=(bsz, lt // TM),
        in_specs=[
            pl.BlockSpec((1, TM, k), lambda b, i: (b, i, 0)),
            pl.BlockSpec((k, d), lambda b, i: (0, 0)),
            pl.BlockSpec((1, TM, d), lambda b, i: (b, i, 0)),
            pl.BlockSpec((1, 1, 1, d), lambda b, i: (b, seg(i), 0, 0)),
        ],
        out_specs=pl.BlockSpec((1, TM, d), lambda b, i: (b, i, 0)),
        out_shape=jax.ShapeDtypeStruct((bsz, lt, d), F32),
        compiler_params=_cparams("parallel", "parallel"),
        name="matmul_gate_residual",
    )(a, w, hs, gate)


FFN_HALO = 16
FFN_PAD = 8


def _ffn_kernel(x_ref, xp_ref, xn_ref, g_ref, sh_ref, sc_ref, gate_ref, wup_ref, cw_ref, cb_ref,
                wdn_ref, fg_ref, o_ref, u_scr, h_scr, *, ncb, nb, off, final):
    i = pl.program_id(1) + off
    g = g_ref[...]
    sh = sh_ref[0, 0]
    sc = sc_ref[0, 0]
    f = wdn_ref.shape[0]
    keep_prev = jnp.where((i == 0) | (i == ncb), 0.0, 1.0)
    keep_next = jnp.where((i == ncb - 1) | (i == nb - 1), 0.0, 1.0)
    u_prev = _norm_mod(xp_ref[0], g, sh, sc)[FFN_HALO - 1:FFN_HALO] * keep_prev
    u_next = _norm_mod(xn_ref[0], g, sh, sc)[0:1] * keep_next
    row = lax.broadcasted_iota(jnp.int32, (FFN_HALO, u_prev.shape[1]), 0)
    extra = jnp.where(row == 0, u_prev, jnp.where(row == 1, u_next, 0.0))
    u_scr[0:TM, :] = _norm_mod(x_ref[0], g, sh, sc).astype(BF16)
    u_scr[TM:, :] = extra.astype(BF16)
    h = _dot(u_scr[...], wup_ref[...])
    h_scr[FFN_PAD:FFN_PAD + TM, :] = h[0:TM]
    h_scr[FFN_PAD - 1:FFN_PAD, :] = h[TM:TM + 1]
    h_scr[FFN_PAD + TM:FFN_PAD + TM + 1, :] = h[TM + 1:TM + 2]
    w = cw_ref[...]
    hc = (h_scr[FFN_PAD - 1:FFN_PAD - 1 + TM, :] * w[0:1] + h_scr[FFN_PAD:FFN_PAD + TM, :] * w[1:2]
          + h_scr[FFN_PAD + 1:FFN_PAD + 1 + TM, :] * w[2:3] + cb_ref[...])
    gg = hc[:, f:]
    act = (hc[:, :f] * (gg * _sigmoid(gg))).astype(BF16)
    y = x_ref[0] + gate_ref[0, 0] * _dot(act, wdn_ref[...])
    if final:
        y = y * lax.rsqrt(jnp.mean(y * y, axis=-1, keepdims=True) + EPS) * fg_ref[...]
    o_ref[0] = y


def _conv_ffn(hs, g, sh, sc, gate, w_up, conv_w, conv_b, w_down, final_g=None, *, ncb):
    bsz, lt, d = hs.shape
    f = w_down.shape[0]
    nb = lt // TM
    final = final_g is not None
    off = ncb if final else 0
    hb = TM // FFN_HALO
    nhalo = lt // FFN_HALO
    seg = _seg(ncb)
    fg = (final_g if final else jnp.ones((d,), F32)).reshape(1, d)
    return pl.pallas_call(
        functools.partial(_ffn_kernel, ncb=ncb, nb=nb, off=off, final=final),
        grid=(bsz, nb - off),
        in_specs=[
            pl.BlockSpec((1, TM, d), lambda b, i: (b, i + off, 0)),
            pl.BlockSpec((1, FFN_HALO, d), lambda b, i: (b, jnp.maximum((i + off) * hb - 1, 0), 0)),
            pl.BlockSpec((1, FFN_HALO, d), lambda b, i: (b, jnp.minimum((i + off + 1) * hb, nhalo - 1), 0)),
            pl.BlockSpec((1, d), lambda b, i: (0, 0)),
            pl.BlockSpec((1, 1, 1, d), lambda b, i: (b, seg(i + off), 0, 0)),
            pl.BlockSpec((1, 1, 1, d), lambda b, i: (b, seg(i + off), 0, 0)),
            pl.BlockSpec((1, 1, 1, d), lambda b, i: (b, seg(i + off), 0, 0)),
            pl.BlockSpec((d, 2 * f), lambda b, i: (0, 0)),
            pl.BlockSpec((3, 2 * f), lambda b, i: (0, 0)),
            pl.BlockSpec((1, 2 * f), lambda b, i: (0, 0)),
            pl.BlockSpec((f, d), lambda b, i: (0, 0)),
            pl.BlockSpec((1, d), lambda b, i: (0, 0)),
        ],
        out_specs=pl.BlockSpec((1, TM, d), lambda b, i: (b, i, 0)),
        out_shape=jax.ShapeDtypeStruct((bsz, lt - off * TM, d), F32),
        scratch_shapes=[
            pltpu.VMEM((TM + FFN_HALO, d), BF16),
            pltpu.VMEM((TM + 2 * FFN_PAD, 2 * f), F32),
        ],
        compiler_params=_cparams("parallel", "parallel"),
        name="conv_ffn",
    )(hs, hs, hs, g.reshape(1, d), sh, sc, gate, w_up.astype(BF16), conv_w, conv_b.reshape(1, 2 * f),
      w_down.astype(BF16), fg)


def _rope_tables(lc, l):
    half = MLA_ROPE // 4
    inv = 1.0 / (ROPE_THETA ** (jnp.arange(half, dtype=F32) / half))
    t = jnp.arange(l)
    ang_r = (t // GRID_W).astype(F32)[:, None] * inv[None, :]
    ang_c = (t % GRID_W).astype(F32)[:, None] * inv[None, :]
    cos = jnp.concatenate([jnp.cos(ang_r), jnp.cos(ang_r), jnp.cos(ang_c), jnp.cos(ang_c)], axis=-1)
    sin = jnp.concatenate([-jnp.sin(ang_r), jnp.sin(ang_r), -jnp.sin(ang_c), jnp.sin(ang_c)], axis=-1)
    pad = ((lc, 0), (MLA_NOPE, LANES - MLA_QK))
    return jnp.pad(cos, pad, constant_values=1.0), jnp.pad(sin, pad)


def _mla_proj_kernel(x_ref, g_ref, sh_ref, sc_ref, win_ref, qg_ref, wq_ref, kvg_ref, wkn_ref, wv_ref,
                     cos_ref, sin_ref, q_ref, k_ref, v_ref):
    u = _norm_mod(x_ref[0], g_ref[...], sh_ref[0, 0], sc_ref[0, 0]).astype(BF16)
    lat = _dot(u, win_ref[...])
    q_lat = lat[:, :MLA_Q_RANK]
    kv_lat = lat[:, MLA_Q_RANK:MLA_Q_RANK + MLA_KV_RANK]
    kr = lat[:, MLA_Q_RANK + MLA_KV_RANK:]

    def rms(t, gg):
        return (t * lax.rsqrt(jnp.mean(t * t, axis=-1, keepdims=True) + EPS) * gg).astype(BF16)

    qn = rms(q_lat, qg_ref[...])
    kvn = rms(kv_lat, kvg_ref[...])
    cos = cos_ref[...]
    sin = sin_ref[...]
    lane = lax.broadcasted_iota(jnp.int32, (TM, LANES), 1)
    first_half = (lane % (MLA_ROPE // 2)) < (MLA_ROPE // 4)

    def rope(t):
        partner = jnp.where(first_half, pltpu.roll(t, LANES - MLA_ROPE // 4, 1), pltpu.roll(t, MLA_ROPE // 4, 1))
        return t * cos + partner * sin

    kr_rot = rope(kr)
    v_all = _dot(kvn, wv_ref[...]).astype(BF16)
    ones = jnp.ones((v_all.shape[0], LANES), BF16)
    for pr in range(MLA_HEADS // 2):
        v_ref[0, :, 2 * pr * LANES:(2 * pr + 1) * LANES] = v_all[:, pr * LANES:(pr + 1) * LANES]
        v_ref[0, :, (2 * pr + 1) * LANES:(2 * pr + 2) * LANES] = ones
    q_all = _dot(qn, wq_ref[...]) * ((MLA_QK ** -0.5) * LOG2E)
    kn_all = _dot(kvn, wkn_ref[...])
    for h in range(MLA_HEADS):
        sl = slice(h * LANES, (h + 1) * LANES)
        q_ref[0, :, sl] = rope(q_all[:, sl]).astype(BF16)
        k_ref[0, :, sl] = (kn_all[:, sl] + kr_rot).astype(BF16)


def _mla_attn_kernel(q_ref, k_ref, v_ref, o_ref, *, ncb, lc):
    i = pl.program_id(2)

    def attend(lk):
        q = q_ref[0]
        lane = lax.broadcasted_iota(jnp.int32, (q.shape[0], LANES), 1)
        for pair in range(q.shape[1] // (2 * LANES)):
            v = v_ref[0, 0:lk, 2 * pair * LANES:(2 * pair + 2) * LANES]
            outs = []
            for e in range(2):
                sl = slice((2 * pair + e) * LANES, (2 * pair + e + 1) * LANES)
                s = _dot_nt(q[:, sl], k_ref[0, 0:lk, sl])
                m = jnp.max(s, axis=-1, keepdims=True)
                p = jnp.exp2(s - m)
                pvl = _dot(p.astype(BF16), v)
                outs.append(pvl[:, :LANES] / pvl[:, LANES:])
            o_ref[0, :, pair * LANES:(pair + 1) * LANES] = jnp.where(lane < MLA_V, outs[0], outs[1]).astype(o_ref.dtype)

    @pl.when(i < ncb)
    def _():
        attend(lc)

    @pl.when(i >= ncb)
    def _():
        attend(k_ref.shape[1])


def _mla_layer(hs, g, sh, sc, gate, w_in, q_norm_g, w_q_up, kv_norm_g, w_kv_up, w_out, *, ncb, lc):
    bsz, lt, d = hs.shape
    l = lt - lc
    nh = MLA_HEADS
    seg = _seg(ncb)
    n_lat = MLA_Q_RANK + MLA_KV_RANK
    w_in_p = jnp.concatenate([w_in[:, :n_lat], jnp.zeros((d, MLA_NOPE), w_in.dtype), w_in[:, n_lat:],
                              jnp.zeros((d, LANES - MLA_QK), w_in.dtype)], axis=1)
    wq = w_q_up.reshape(MLA_Q_RANK, nh, MLA_QK)
    wq = jnp.pad(wq, ((0, 0), (0, 0), (0, LANES - MLA_QK))).reshape(MLA_Q_RANK, nh * LANES)
    wkv = w_kv_up.reshape(MLA_KV_RANK, nh, MLA_NOPE + MLA_V)
    wkn = jnp.pad(wkv[:, :, :MLA_NOPE], ((0, 0), (0, 0), (0, LANES - MLA_NOPE))).reshape(MLA_KV_RANK, nh * LANES)
    wv = wkv[:, :, MLA_NOPE:].reshape(MLA_KV_RANK, nh * MLA_V)
    cos_t, sin_t = _rope_tables(lc, l)

    q, k, v = pl.pallas_call(
        _mla_proj_kernel,
        grid=(bsz, lt // TM),
        in_specs=[
            pl.BlockSpec((1, TM, d), lambda b, i: (b, i, 0)),
            pl.BlockSpec((1, d), lambda b, i: (0, 0)),
            pl.BlockSpec((1, 1, 1, d), lambda b, i: (b, seg(i), 0, 0)),
            pl.BlockSpec((1, 1, 1, d), lambda b, i: (b, seg(i), 0, 0)),
            pl.BlockSpec((d, 768), lambda b, i: (0, 0)),
            pl.BlockSpec((1, MLA_Q_RANK), lambda b, i: (0, 0)),
            pl.BlockSpec((MLA_Q_RANK, nh * LANES), lambda b, i: (0, 0)),
            pl.BlockSpec((1, MLA_KV_RANK), lambda b, i: (0, 0)),
            pl.BlockSpec((MLA_KV_RANK, nh * LANES), lambda b, i: (0, 0)),
            pl.BlockSpec((MLA_KV_RANK, nh * MLA_V), lambda b, i: (0, 0)),
            pl.BlockSpec((TM, LANES), lambda b, i: (i, 0)),
            pl.BlockSpec((TM, LANES), lambda b, i: (i, 0)),
        ],
        out_specs=[
            pl.BlockSpec((1, TM, nh * LANES), lambda b, i: (b, i, 0)),
            pl.BlockSpec((1, TM, nh * LANES), lambda b, i: (b, i, 0)),
            pl.BlockSpec((1, TM, nh * LANES), lambda b, i: (b, i, 0)),
        ],
        out_shape=[
            jax.ShapeDtypeStruct((bsz, lt, nh * LANES), BF16),
            jax.ShapeDtypeStruct((bsz, lt, nh * LANES), BF16),
            jax.ShapeDtypeStruct((bsz, lt, nh * LANES), BF16),
        ],
        compiler_params=_cparams("parallel", "parallel"),
        name="mla_proj",
    )(hs, g.reshape(1, d), sh, sc, w_in_p.astype(BF16), q_norm_g.reshape(1, -1), wq.astype(BF16),
      kv_norm_g.reshape(1, -1), wkn.astype(BF16), wv.astype(BF16), cos_t, sin_t)

    o = pl.pallas_call(
        functools.partial(_mla_attn_kernel, ncb=ncb, lc=lc),
        grid=(bsz, nh // MLA_HEADS_PER_STEP, lt // TM),
        in_specs=[
            pl.BlockSpec((1, TM, MLA_HEADS_PER_STEP * LANES), lambda b, p, i: (b, i, p)),
            pl.BlockSpec((1, lt, MLA_HEADS_PER_STEP * LANES), lambda b, p, i: (b, 0, p)),
            pl.BlockSpec((1, lt, MLA_HEADS_PER_STEP * LANES), lambda b, p, i: (b, 0, p)),
        ],
        out_specs=pl.BlockSpec((1, TM, MLA_HEADS_PER_STEP * MLA_V), lambda b, p, i: (b, i, p)),
        out_shape=jax.ShapeDtypeStruct((bsz, lt, nh * MLA_V), BF16),
        compiler_params=_cparams("parallel", "parallel", "parallel"),
        name="mla_attn",
    )(q, k, v)
    return _matmul_gate_residual(o, w_out.astype(BF16), hs, gate, ncb=ncb)


def _s5_matrices(lam_re, lam_im, log_dt, b_re, b_im, c_re, c_im):
    t_len = S5_T
    n_grp = lam_re.shape[1]
    ch = S5_GROUP_CH
    n_st = S5_STATE
    dt = jnp.exp(log_dt.astype(F32))[..., None]
    ld_re = lam_re.astype(F32) * dt
    ld_im = lam_im.astype(F32) * dt
    tau = jnp.arange(t_len + 1, dtype=F32)[:, None, None, None]
    mag = jnp.exp(tau * ld_re[None])
    pw_re = mag * jnp.cos(tau * ld_im[None])
    pw_im = mag * jnp.sin(tau * ld_im[None])
    tau_r = jnp.arange(t_len, -1, -1, dtype=F32)[:, None, None, None]
    mag_r = jnp.exp(tau_r * ld_re[None])
    pr_re = mag_r * jnp.cos(tau_r * ld_im[None])
    pr_im = mag_r * jnp.sin(tau_r * ld_im[None])
    lb_re, lb_im = pw_re[1] - 1.0, pw_im[1]
    den = lam_re * lam_re + lam_im * lam_im
    f_re = (lb_re * lam_re + lb_im * lam_im) / den
    f_im = (lb_im * lam_re - lb_re * lam_im) / den
    bb_re = f_re[..., None] * b_re - f_im[..., None] * b_im
    bb_im = f_re[..., None] * b_im + f_im[..., None] * b_re
    cp_re = c_re[None] * pw_re[:t_len, :, :, None, :] - c_im[None] * pw_im[:t_len, :, :, None, :]
    cp_im = c_re[None] * pw_im[:t_len, :, :, None, :] + c_im[None] * pw_re[:t_len, :, :, None, :]
    taps = (jnp.einsum("tdgon,dgni->tdgoi", cp_re, bb_re, precision=HI)
            - jnp.einsum("tdgon,dgni->tdgoi", cp_im, bb_im, precision=HI))
    s_idx = np.arange(t_len)[:, None, None]
    t_idx = np.arange(t_len)[None, :, None]
    u_idx = np.arange(t_len)[None, None, :]
    sel_f = jnp.asarray((t_idx - s_idx == u_idx).astype(np.float32))
    sel_b = jnp.asarray((s_idx - t_idx == u_idx).astype(np.float32))
    kf = jnp.einsum("stu,ugoi->stgoi", sel_f, taps[:, 0], precision=HI)
    kb = jnp.einsum("stu,ugoi->stgoi", sel_b, taps[:, 1], precision=HI)
    k_tot = (kf + kb).transpose(2, 0, 4, 1, 3).reshape(n_grp, t_len * ch, t_len * ch)

    def state_in(pw_r, pw_i, brr, bii):
        re = pw_r[..., None] * brr[None] - pw_i[..., None] * bii[None]
        im = pw_r[..., None] * bii[None] + pw_i[..., None] * brr[None]
        return (re.transpose(1, 0, 3, 2).reshape(n_grp, t_len * ch, n_st),
                im.transpose(1, 0, 3, 2).reshape(n_grp, t_len * ch, n_st))

    inf_re, inf_im = state_in(pr_re[1:, 0], pr_im[1:, 0], bb_re[0], bb_im[0])
    inb_re, inb_im = state_in(pw_re[:t_len, 1], pw_im[:t_len, 1], bb_re[1], bb_im[1])

    def state_out(pw_r, pw_i, crr, cii):
        re = crr[None] * pw_r[:, :, None, :] - cii[None] * pw_i[:, :, None, :]
        im = crr[None] * pw_i[:, :, None, :] + cii[None] * pw_r[:, :, None, :]
        return (re.transpose(1, 3, 0, 2).reshape(n_grp, n_st, t_len * ch),
                (-im).transpose(1, 3, 0, 2).reshape(n_grp, n_st, t_len * ch))

    outf_re, outf_im = state_out(pw_re[1:, 0], pw_im[1:, 0], c_re[0], c_im[0])
    outb_re, outb_im = state_out(pr_re[:t_len, 1], pr_im[:t_len, 1], c_re[1], c_im[1])
    nq = n_grp // S5_QGRP
    npair = n_grp // 2
    cols = t_len * ch
    x = jnp.stack([inf_re, inf_im, inb_re, inb_im], 0).reshape(4, npair, 2, cols, n_st)
    x = x.transpose(1, 2, 3, 0, 4)
    zx = jnp.zeros_like(x[:, 0])
    w_pair = jnp.concatenate([jnp.concatenate([x[:, 0], zx], -1), jnp.concatenate([zx, x[:, 1]], -1)], 1)
    w_pair = w_pair.reshape(nq, S5_QGRP // 2, 2 * cols, 4 * 2 * n_st)
    y = jnp.stack([outf_re, outf_im, outb_re, outb_im], 0).reshape(4, npair, 2, n_st, cols)
    y = y.transpose(1, 0, 2, 3, 4)
    zy = jnp.zeros_like(y[:, :, 0])
    m_pair = jnp.stack([jnp.concatenate([y[:, :, 0], zy], -1), jnp.concatenate([zy, y[:, :, 1]], -1)], 2)
    m_pair = m_pair.reshape(nq, S5_QGRP // 2, 4 * 2 * n_st, 2 * cols)
    decay = jnp.stack([pw_re[t_len, 0], pw_im[t_len, 0], pw_re[t_len, 1], pw_im[t_len, 1]], 0)
    decay = decay.reshape(4, nq, S5_QGRP * n_st).transpose(1, 0, 2).reshape(nq, 1, 4 * S5_QGRP * n_st)
    k_grp = k_tot.reshape(nq, S5_QGRP, cols, cols)
    return w_pair.astype(BF16), k_grp.astype(BF16), m_pair.astype(BF16), decay


def _s5_permutation():
    n = S5_T * LANES
    src = jnp.arange(n)
    t, g, c = src // LANES, (src % LANES) // S5_GROUP_CH, src % S5_GROUP_CH
    dst = g * (S5_T * S5_GROUP_CH) + t * S5_GROUP_CH + c
    return (dst[:, None] == jnp.arange(n)[None, :]).astype(BF16)


def _s5_chunk_rows(z_ref):
    jt = z_ref.shape[1] // S5_T
    return jnp.concatenate([z_ref[0, pl.ds(t, jt, stride=S5_T), :].astype(BF16) for t in range(S5_T)], axis=1)


def _s5_core_kernel(z_ref, perm_ref, win_ref, kg_ref, mout_ref, a_ref, y_ref, u_scr, p_scr, *, jc, jt):
    sc = a_ref.shape[2] // 4
    npair = win_ref.shape[1]
    pc = win_ref.shape[2]
    gc = pc // 2
    zg = _dot(_s5_chunk_rows(z_ref), perm_ref[...]).astype(BF16)
    for pp in range(npair):
        res = _dot(zg[:, pp * pc:(pp + 1) * pc], win_ref[0, pp])
        for k in range(4):
            u_scr[:, k * sc + pp * LANES:k * sc + (pp + 1) * LANES] = res[:, k * LANES:(k + 1) * LANES]
    a = a_ref[0]
    afr, afi, abr, abi = (a[:, k * sc:(k + 1) * sc] for k in range(4))
    zero = jnp.zeros((1, sc), F32)

    def body(jj, carry):
        fr, fi, br, bi = carry
        jb = jnp.where(jj < jc, jc - 1 - jj, jt - 1 - jj + jc)
        p_scr[pl.ds(jj, 1), 0:sc] = fr
        p_scr[pl.ds(jj, 1), sc:2 * sc] = fi
        p_scr[pl.ds(jb, 1), 2 * sc:3 * sc] = br
        p_scr[pl.ds(jb, 1), 3 * sc:4 * sc] = bi
        nfr = afr * fr - afi * fi + u_scr[pl.ds(jj, 1), 0:sc]
        nfi = afr * fi + afi * fr + u_scr[pl.ds(jj, 1), sc:2 * sc]
        nbr = abr * br - abi * bi + u_scr[pl.ds(jb, 1), 2 * sc:3 * sc]
        nbi = abr * bi + abi * br + u_scr[pl.ds(jb, 1), 3 * sc:4 * sc]
        return nfr, nfi, nbr, nbi

    lax.fori_loop(0, jt, body, (zero, zero, zero, zero), unroll=S5_SCAN_UNROLL)
    pieces = []
    for pp in range(npair):
        p_in = jnp.concatenate([p_scr[:, k * sc + pp * LANES:k * sc + (pp + 1) * LANES] for k in range(4)], axis=1)
        ys = _dot(p_in.astype(BF16), mout_ref[0, pp])
        for e in range(2):
            g = 2 * pp + e
            yg = ys[:, e * gc:(e + 1) * gc] + _dot(zg[:, g * gc:(g + 1) * gc], kg_ref[0, g])
            pieces.append(yg.astype(BF16))
    acc = _dot_nt(jnp.concatenate(pieces, axis=1), perm_ref[...])
    for t in range(S5_T):
        y_ref[0, pl.ds(t, jt, stride=S5_T), :] = acc[:, t * LANES:(t + 1) * LANES]


def _s5_glu_kernel(y_ref, z_ref, d_ref, w_ref, x_ref, gate_ref, o_ref):
    d = x_ref.shape[2]
    y = y_ref[0] + z_ref[0] * d_ref[...]
    ge = jax.nn.gelu(y).astype(BF16)
    ag = _dot(ge, w_ref[...])
    o_ref[0] = x_ref[0] + gate_ref[0, 0] * (ag[:, :d] * _sigmoid(ag[:, d:]))


def _s5_layer(hs, g, sh, sc, gate, w_in, lam_re, lam_im, log_dt, b_re, b_im, c_re, c_im, d_skip, w_glu,
              *, ncb, lc):
    bsz, lt, d = hs.shape
    width = w_in.shape[1]
    nq = width // LANES
    jt = lt // S5_T
    jc = lc // S5_T
    ccols = S5_T * LANES
    scols = 4 * S5_QGRP * S5_STATE
    seg = _seg(ncb)
    z = _norm_mod_matmul(hs, g, sh, sc, w_in.astype(BF16), jnp.ones((1, width), F32),
                         ncb=ncb, out_dtype=F32, tn=width)
    w_pair, k_grp, m_pair, decay = _s5_matrices(lam_re, lam_im, log_dt, b_re, b_im, c_re, c_im)

    def per_quarter(arr):
        return pl.BlockSpec((1,) + arr.shape[1:], lambda q, b: (q, 0, 0, 0))

    y = pl.pallas_call(
        functools.partial(_s5_core_kernel, jc=jc, jt=jt),
        grid=(nq, bsz),
        in_specs=[
            pl.BlockSpec((1, lt, LANES), lambda q, b: (b, 0, q)),
            pl.BlockSpec((ccols, ccols), lambda q, b: (0, 0), pipeline_mode=pl.Buffered(1)),
            per_quarter(w_pair),
            per_quarter(k_grp),
            per_quarter(m_pair),
            pl.BlockSpec((1, 1, scols), lambda q, b: (q, 0, 0)),
        ],
        out_specs=pl.BlockSpec((1, lt, LANES), lambda q, b: (b, 0, q)),
        out_shape=jax.ShapeDtypeStruct((bsz, lt, width), F32),
        scratch_shapes=[pltpu.VMEM((jt, scols), F32), pltpu.VMEM((jt, scols), F32)],
        compiler_params=_cparams("parallel", "parallel"),
        name="s5_core",
    )(z, _s5_permutation(), w_pair, k_grp, m_pair, decay)
    return pl.pallas_call(
        _s5_glu_kernel,
        grid=(bsz, lt // TM),
        in_specs=[
            pl.BlockSpec((1, TM, width), lambda b, i: (b, i, 0)),
            pl.BlockSpec((1, TM, width), lambda b, i: (b, i, 0)),
            pl.BlockSpec((1, width), lambda b, i: (0, 0)),
            pl.BlockSpec((width, 2 * d), lambda b, i: (0, 0)),
            pl.BlockSpec((1, TM, d), lambda b, i: (b, i, 0)),
            pl.BlockSpec((1, 1, 1, d), lambda b, i: (b, seg(i), 0, 0)),
        ],
        out_specs=pl.BlockSpec((1, TM, d), lambda b, i: (b, i, 0)),
        out_shape=jax.ShapeDtypeStruct((bsz, lt, d), F32),
        compiler_params=_cparams("parallel", "parallel"),
        name="s5_glu",
    )(y, z, d_skip.reshape(1, width), w_glu.astype(BF16), hs, gate)


def _gla_dir(zq, zf, zv, lb, states, incl, tri, rev):
    dk = HG_HEAD_DIM
    width = zq.shape[1]
    nchunks = zq.shape[0] // HG_CHUNK
    forget = lb + (1.0 - lb) * _sigmoid(zf)
    lf = jnp.log(forget)
    kk = 1.0 - forget
    v_all = zv.astype(BF16)
    hi = lf.astype(BF16)
    r1 = lf - hi.astype(F32)
    mid = r1.astype(BF16)
    lo = (r1 - mid.astype(F32)).astype(BF16)
    parts = _dot(tri, jnp.concatenate([hi, mid, lo], axis=1))
    bcum = parts[:, :width] + parts[:, width:2 * width] + parts[:, 2 * width:]
    btot = jnp.concatenate(
        [jnp.broadcast_to(bcum[c * HG_CHUNK:c * HG_CHUNK + 1] if rev else bcum[(c + 1) * HG_CHUNK - 1:(c + 1) * HG_CHUNK],
                          (HG_CHUNK, width)) for c in range(nchunks)], axis=0)
    q_in_all = (zq * _sigmoid(zq)) * (dk ** -0.5) * jnp.exp(bcum)
    k_in_all = (kk * jnp.exp(-bcum)).astype(BF16)
    k_out_all = kk * jnp.exp(btot - bcum)
    chunk_of_row = lax.broadcasted_iota(jnp.int32, (zq.shape[0], dk), 0) // HG_CHUNK
    zero = jnp.zeros((zq.shape[0], dk), F32)
    outs, new_states = [], []
    for h, st in enumerate(states):
        sl = slice(h * dk, (h + 1) * dk)
        q_in, k_out, v = q_in_all[:, sl], k_out_all[:, sl], v_all[:, sl]
        att = jnp.where(incl, _dot_nt(q_in.astype(BF16), k_in_all[:, sl]), 0.0)
        o_intra = _dot(att.astype(BF16), v)
        k_cat = jnp.concatenate([jnp.where(chunk_of_row == c, k_out, zero) for c in range(nchunks)], axis=1)
        q_cat = jnp.concatenate([jnp.where(chunk_of_row == c, q_in, zero) for c in range(nchunks)], axis=1)
        ds = _dot_tn(v, k_cat.astype(BF16))
        entering = [None] * nchunks
        for cc in range(nchunks):
            c = nchunks - 1 - cc if rev else cc
            entering[c] = st
            st = st * jnp.exp(btot[c * HG_CHUNK:c * HG_CHUNK + 1, sl]) + ds[:, c * dk:(c + 1) * dk]
        scat = jnp.concatenate(entering, axis=1).astype(BF16)
        outs.append(o_intra + _dot_nt(q_cat.astype(BF16), scat))
        new_states.append(st)
    return outs, new_states


def _gla_kernel(zqf_ref, zff_ref, zvf_ref, zqb_ref, zfb_ref, zvb_ref, lb_ref, of_ref, ob_ref, st_scr):
    @pl.when(pl.program_id(2) == 0)
    def _():
        st_scr[...] = jnp.zeros_like(st_scr)

    n = zqf_ref.shape[1]
    row = lax.broadcasted_iota(jnp.int32, (n, n), 0)
    col = lax.broadcasted_iota(jnp.int32, (n, n), 1)
    same = (row // HG_CHUNK) == (col // HG_CHUNK)
    dk = HG_HEAD_DIM
    for direction, (zq_ref, zf_ref, zv_ref, o_ref) in enumerate(
            ((zqf_ref, zff_ref, zvf_ref, of_ref), (zqb_ref, zfb_ref, zvb_ref, ob_ref))):
        rev = direction == 1
        incl = same & ((col >= row) if rev else (col <= row))
        tri = incl.astype(BF16)
        nheads = zq_ref.shape[2] // dk
        outs, states = _gla_dir(zq_ref[0].astype(F32), zf_ref[0], zv_ref[0], lb_ref[0],
                                [st_scr[direction, h] for h in range(nheads)], incl, tri, rev)
        for h in range(nheads):
            o_ref[0, :, h * dk:(h + 1) * dk] = outs[h].astype(o_ref.dtype)
            st_scr[direction, h] = states[h]


def _hg_out_kernel(of_ref, ob_ref, zg_ref, ng_ref, w_ref, x_ref, gate_ref, o_ref):
    o = of_ref[0].astype(F32) + ob_ref[0].astype(F32)
    gsig = zg_ref[0].astype(F32)
    gsig = gsig * _sigmoid(gsig)
    ng = ng_ref[...]
    parts = []
    for h in range(o.shape[1] // HG_HEAD_DIM):
        sl = slice(h * HG_HEAD_DIM, (h + 1) * HG_HEAD_DIM)
        oh = o[:, sl]
        on = oh * lax.rsqrt(jnp.mean(oh * oh, axis=-1, keepdims=True) + EPS)
        parts.append((on * ng[:, sl] * gsig[:, sl]).astype(BF16))
    a = jnp.concatenate(parts, axis=-1)
    o_ref[0] = x_ref[0] + gate_ref[0, 0] * _dot(a, w_ref[...])


def _hgrn2_layer(hs, g, sh, sc, gate, w_in, lower_bound, norm_g, w_out, *, ncb, lc):
    bsz, lt, d = hs.shape
    nh = d // HG_HEAD_DIM
    nb = lt // TM
    seg = _seg(ncb)
    w_r = jnp.concatenate([w_in[:, d:3 * d], w_in[:, :d], w_in[:, 3 * d:]], axis=1).astype(BF16)
    zf, zr = _norm_mod_matmul(hs, g, sh, sc, w_r, jnp.ones((1, 5 * d), F32),
                              ncb=ncb, out_dtype=((F32, 2 * d), (BF16, 3 * d)), tn=1024)
    lb = lower_bound.astype(F32).reshape(1, 1, d)
    hps = HG_HEADS_PER_STEP
    ng = nh // hps
    wb = hps * HG_HEAD_DIM

    def rblk(s):
        return jnp.where(s < ncb, ncb - 1 - s, nb - 1 - s + ncb)

    outs = pl.pallas_call(
        _gla_kernel,
        grid=(bsz, ng, nb),
        in_specs=[
            pl.BlockSpec((1, TM, wb), lambda b, h, s: (b, s, h)),
            pl.BlockSpec((1, TM, wb), lambda b, h, s: (b, s, h)),
            pl.BlockSpec((1, TM, wb), lambda b, h, s: (b, s, ng + h)),
            pl.BlockSpec((1, TM, wb), lambda b, h, s: (b, rblk(s), h)),
            pl.BlockSpec((1, TM, wb), lambda b, h, s: (b, rblk(s), ng + h)),
            pl.BlockSpec((1, TM, wb), lambda b, h, s: (b, rblk(s), ng + h)),
            pl.BlockSpec((1, 1, wb), lambda b, h, s: (0, 0, h)),
        ],
        out_specs=[
            pl.BlockSpec((1, TM, wb), lambda b, h, s: (b, s, h)),
            pl.BlockSpec((1, TM, wb), lambda b, h, s: (b, rblk(s), h)),
        ],
        out_shape=[jax.ShapeDtypeStruct((bsz, lt, d), BF16), jax.ShapeDtypeStruct((bsz, lt, d), BF16)],
        scratch_shapes=[pltpu.VMEM((2, hps, HG_HEAD_DIM, HG_HEAD_DIM), F32)],
        compiler_params=_cparams("parallel", "parallel", "arbitrary"),
        name="hgrn2_gla",
    )(zr, zf, zr, zr, zf, zr, lb)
    return pl.pallas_call(
        _hg_out_kernel,
        grid=(bsz, nb),
        in_specs=[
            pl.BlockSpec((1, TM, d), lambda b, i: (b, i, 0)),
            pl.BlockSpec((1, TM, d), lambda b, i: (b, i, 0)),
            pl.BlockSpec((1, TM, d), lambda b, i: (b, i, 2)),
            pl.BlockSpec((1, d), lambda b, i: (0, 0)),
            pl.BlockSpec((d, d), lambda b, i: (0, 0)),
            pl.BlockSpec((1, TM, d), lambda b, i: (b, i, 0)),
            pl.BlockSpec((1, 1, 1, d), lambda b, i: (b, seg(i), 0, 0)),
        ],
        out_specs=pl.BlockSpec((1, TM, d), lambda b, i: (b, i, 0)),
        out_shape=jax.ShapeDtypeStruct((bsz, lt, d), F32),
        compiler_params=_cparams("parallel", "parallel"),
        name="hgrn2_out",
    )(outs[0], outs[1], zr, norm_g.reshape(1, d), w_out.astype(BF16), hs, gate)


def _na_bias_table(rpb):
    w = np.arange(GRID_W)[:, None, None]
    kc = np.arange(GRID_W)[None, :, None]
    co = np.arange(2 * NA_KW - 1)[None, None, :]
    c0 = np.clip(w - NA_KW // 2, 0, GRID_W - NA_KW)
    valid = (kc >= c0) & (kc < c0 + NA_KW)
    onehot = jnp.asarray((valid & (kc - w + (NA_KW - 1) == co)).astype(np.float32))
    t = jnp.einsum("hrc,wkc->hrwk", rpb.astype(F32), onehot, precision=HI) * LOG2E
    t = jnp.where(jnp.asarray(valid[None, None, :, :, 0]), t, NEG_BIG)
    ext = jnp.pad(t, ((0, 0), (NA_KH, NA_KH), (0, 0), (0, 0)), constant_values=NEG_BIG)
    return jnp.concatenate([ext[:, :-1], ext[:, 1:]], axis=-1)


def _na_kernel(q_ref, k_ref, v_ref, bias_ref, o_ref, vx_scr, *, lc, rows):
    nblk = rows // NA_BLK_ROWS
    nq = NA_BLK_ROWS * GRID_W
    nwin = NA_WIN_ROWS * GRID_W
    o_ref[0, 0:lc, :] = jnp.zeros((lc, LANES), o_ref.dtype)
    vx_scr[:, 0:LANES] = v_ref[0]
    vx_scr[:, LANES:] = jnp.ones((vx_scr.shape[0], LANES), vx_scr.dtype)
    kc = k_ref[0, 0:lc, :]
    vc = vx_scr[0:lc, :]
    lane = lax.broadcasted_iota(jnp.int32, (nq, LANES), 1)
    lo_half = lane < NA_HEAD_DIM
    key_lane = lax.broadcasted_iota(jnp.int32, (GRID_W, nwin), 1)
    half = NA_KH // 2

    def one_block(i):
        wr0 = jnp.clip(i * NA_BLK_ROWS - half, 0, rows - NA_WIN_ROWS)
        first, last = i == 0, i == nblk - 1
        delta0 = jnp.where(first, NA_KH - 1, jnp.where(last, -1, NA_KH - 1 - half))
        qoff = pl.multiple_of(lc + i * nq, GRID_W * NA_KH // 2)
        woff = pl.multiple_of(lc + wr0 * GRID_W, GRID_W * NA_KH // 2)
        q = q_ref[0, pl.ds(qoff, nq), :]
        kw = k_ref[0, pl.ds(woff, nwin), :]
        vw = vx_scr[pl.ds(woff, nwin), :]
        zq = jnp.zeros_like(q)
        outs = []
        for e in range(2):
            qe = jnp.where(lo_half, q, zq) if e == 0 else jnp.where(lo_half, zq, q)
            s_all = _dot_nt(qe, kw)
            slabs = []
            for j in range(NA_BLK_ROWS):
                a = jnp.where(first, max(j - half, 0), jnp.where(last, min(j + half, NA_KH), j))
                valid = (key_lane >= a * GRID_W) & (key_lane < (a + NA_KH) * GRID_W)
                bias = jnp.concatenate([bias_ref[e, delta0 - j + NA_KH + 2 * m] for m in range(NA_WIN_ROWS // 2)],
                                       axis=1)
                slabs.append(jnp.where(valid, s_all[j * GRID_W:(j + 1) * GRID_W] + bias, NEG_BIG))
            s_loc = jnp.concatenate(slabs, axis=0)
            s_ctx = _dot_nt(qe, kc)
            m = jnp.maximum(jnp.max(s_loc, axis=-1, keepdims=True), jnp.max(s_ctx, axis=-1, keepdims=True))
            p_loc = jnp.exp2(s_loc - m)
            p_ctx = jnp.exp2(s_ctx - m)
            pvl = _dot(p_loc.astype(BF16), vw) + _dot(p_ctx.astype(BF16), vc)
            outs.append(pvl[:, :LANES] / pvl[:, LANES:])
        o_ref[0, pl.ds(qoff, nq), :] = jnp.where(lo_half, outs[0], outs[1]).astype(o_ref.dtype)

    def body(ii, carry):
        for j in range(NA_BLK_UNROLL):
            one_block(ii * NA_BLK_UNROLL + j)
        return carry

    lax.fori_loop(0, nblk // NA_BLK_UNROLL, body, 0)


def _natten_layer(hs, g, sh, sc, gate, w_qkv, rpb, w_out, *, ncb, lc):
    bsz, lt, d = hs.shape
    rows = (lt - lc) // GRID_W
    assert rows >= NA_WIN_ROWS and rows % (NA_BLK_ROWS * NA_BLK_UNROLL) == 0
    npair = NA_HEADS // 2
    col_scale = jnp.concatenate([jnp.full((1, d), (NA_HEAD_DIM ** -0.5) * LOG2E, F32),
                                 jnp.ones((1, 2 * d), F32)], axis=-1)
    qkv = _norm_mod_matmul(hs, g, sh, sc, w_qkv.astype(BF16), col_scale, ncb=ncb, out_dtype=BF16, tn=1024)
    bias = _na_bias_table(rpb)
    o = pl.pallas_call(
        functools.partial(_na_kernel, lc=lc, rows=rows),
        grid=(npair, bsz),
        in_specs=[
            pl.BlockSpec((1, lt, LANES), lambda p, b: (b, 0, p)),
            pl.BlockSpec((1, lt, LANES), lambda p, b: (b, 0, npair + p)),
            pl.BlockSpec((1, lt, LANES), lambda p, b: (b, 0, 2 * npair + p)),
            pl.BlockSpec((2, NA_TABLE_PAIRS, GRID_W, 2 * GRID_W), lambda p, b: (p, 0, 0, 0)),
        ],
        out_specs=pl.BlockSpec((1, lt, LANES), lambda p, b: (b, 0, p)),
        out_shape=jax.ShapeDtypeStruct((bsz, lt, d), BF16),
        scratch_shapes=[pltpu.VMEM((lt, 2 * LANES), BF16)],
        compiler_params=_cparams("parallel", "parallel"),
        name="natten",
    )(qkv, qkv, qkv, bias)
    return _matmul_gate_residual(o, w_out.astype(BF16), hs, gate, ncb=ncb)


def _layer_mods(mod_rows, bsz, d):
    mx = mod_rows[:bsz].reshape(bsz, 6, d)
    mc = jnp.broadcast_to(mod_rows[bsz].reshape(1, 6, d), (bsz, 6, d))
    m = jnp.stack([mc, mx], axis=1)
    return [m[:, :, k, None, :] for k in range(6)]


def kernel(x, c, ctx, c_ctx, ada_w, ada_b, norm1_g, norm2_g, mla_w_in, mla_q_norm_g, mla_w_q_up, mla_kv_norm_g, mla_w_kv_up, mla_w_out, s5_w_in, s5_lambda_re, s5_lambda_im, s5_log_dt, s5_b_re, s5_b_im, s5_c_re, s5_c_im, s5_d, s5_w_glu, hg_w_in, hg_lower_bound, hg_norm_g, hg_w_out, na_w_qkv, na_rpb, na_w_out, ffn_w_up, ffn_conv_w, ffn_conv_b, ffn_w_down, final_g):
    bsz, l, d = x.shape
    lc = ctx.shape[1]
    depth = ada_w.shape[0]
    assert lc % TM == 0 and l % TM == 0 and l % GRID_W == 0 and bsz + 1 <= 16
    ncb = lc // TM
    hs = jnp.concatenate([ctx, x], axis=1)
    cond_rows = jnp.concatenate([c, c_ctx[None], jnp.zeros((16 - bsz - 1, d), F32)], axis=0)
    mod_all = _ada_mod(cond_rows, ada_w, ada_b)
    lb_cum = jnp.cumsum(jax.nn.softmax(hg_lower_bound.astype(F32), axis=0), axis=0)
    lower_bounds = lb_cum - lb_cum[0]
    for i in range(depth):
        kind, j = i % 4, i // 4
        sh1, sc1, g1, sh2, sc2, g2 = _layer_mods(mod_all[i], bsz, d)
        if kind == 0:
            hs = _mla_layer(hs, norm1_g[i], sh1, sc1, g1, mla_w_in[j], mla_q_norm_g[j], mla_w_q_up[j],
                            mla_kv_norm_g[j], mla_w_kv_up[j], mla_w_out[j], ncb=ncb, lc=lc)
        elif kind == 1:
            hs = _s5_layer(hs, norm1_g[i], sh1, sc1, g1, s5_w_in[j], s5_lambda_re[j], s5_lambda_im[j],
                           s5_log_dt[j], s5_b_re[j], s5_b_im[j], s5_c_re[j], s5_c_im[j], s5_d[j], s5_w_glu[j],
                           ncb=ncb, lc=lc)
        elif kind == 2:
            hs = _hgrn2_layer(hs, norm1_g[i], sh1, sc1, g1, hg_w_in[j], lower_bounds[i], hg_norm_g[j],
                              hg_w_out[j], ncb=ncb, lc=lc)
        else:
            hs = _natten_layer(hs, norm1_g[i], sh1, sc1, g1, na_w_qkv[j], na_rpb[j], na_w_out[j],
                               ncb=ncb, lc=lc)
        hs = _conv_ffn(hs, norm2_g[i], sh2, sc2, g2, ffn_w_up[i], ffn_conv_w[i], ffn_conv_b[i],
                       ffn_w_down[i], final_g if i == depth - 1 else None, ncb=ncb)
    return hs
```

```python
import functools
import math

import jax
import jax.numpy as jnp
import numpy as np
from jax import lax
from jax.experimental import pallas as pl
from jax.experimental.pallas import tpu as pltpu

F32 = jnp.float32
BF16 = jnp.bfloat16
HI = lax.Precision.HIGHEST

EPS = 1e-6
GRID_W = 64
ROPE_THETA = 10000.0
LOG2E = math.log2(math.e)

LANES = 128
VMEM_LIMIT_BYTES = 56 * 1024 * 1024
TM = 256

MLA_HEADS = 16
MLA_Q_RANK = 384
MLA_KV_RANK = 256
MLA_NOPE = 64
MLA_ROPE = 32
MLA_V = 64
MLA_QK = MLA_NOPE + MLA_ROPE
MLA_HEADS_PER_STEP = 8

S5_GROUP_CH = 16
S5_STATE = 64
S5_T = 16
S5_SCAN_UNROLL = 4
S5_QGRP = LANES // S5_GROUP_CH

HG_HEAD_DIM = 128
HG_CHUNK = 64
HG_HEADS_PER_STEP = 8

NA_HEADS = 16
NA_HEAD_DIM = 64
NA_KH = 8
NA_KW = 16
NEG_BIG = -1e30
NA_BLK_ROWS = 8
NA_WIN_ROWS = 16
NA_TABLE_PAIRS = 4 * NA_KH - 2
NA_BLK_UNROLL = 4


def _cparams(*sem):
    return pltpu.CompilerParams(dimension_semantics=sem, vmem_limit_bytes=VMEM_LIMIT_BYTES)


def _seg(ncb):
    return lambda i: jnp.where(i >= ncb, 1, 0)


def _norm_mod(x, g, sh, sc):
    ms = jnp.mean(x * x, axis=-1, keepdims=True)
    return (x * lax.rsqrt(ms + EPS) * g) * (1.0 + sc) + sh


def _sigmoid(x):
    return 1.0 / (1.0 + jnp.exp(-x))


def _dot(a, b):
    return jnp.dot(a, b, preferred_element_type=F32)


def _dot_nt(a, b):
    return lax.dot_general(a, b, (((1,), (1,)), ((), ())), preferred_element_type=F32)


def _dot_tn(a, b):
    return lax.dot_general(a, b, (((0,), (0,)), ((), ())), preferred_element_type=F32)


def _ada_kernel(cond_ref, w_ref, b_ref, o_ref):
    cond = cond_ref[...]
    a = (cond * _sigmoid(cond)).astype(BF16)
    o_ref[0] = _dot(a, w_ref[0].astype(BF16)) + b_ref[0]


def _ada_mod(cond_rows, ada_w, ada_b):
    depth, d, n = ada_w.shape
    rows = cond_rows.shape[0]
    tn = 1536
    return pl.pallas_call(
        _ada_kernel,
        grid=(depth, n // tn),
        in_specs=[
            pl.BlockSpec((rows, d), lambda i, j: (0, 0)),
            pl.BlockSpec((1, d, tn), lambda i, j: (i, 0, j)),
            pl.BlockSpec((1, 1, tn), lambda i, j: (i, 0, j)),
        ],
        out_specs=pl.BlockSpec((1, rows, tn), lambda i, j: (i, 0, j)),
        out_shape=jax.ShapeDtypeStruct((depth, rows, n), F32),
        compiler_params=_cparams("parallel", "parallel"),
        name="ada_mod",
    )(cond_rows, ada_w, ada_b.reshape(depth, 1, n))


def _nmm_kernel(x_ref, g_ref, sh_ref, sc_ref, w_ref, cs_ref, *o_refs, tn):
    u = _norm_mod(x_ref[0], g_ref[...], sh_ref[0, 0], sc_ref[0, 0]).astype(BF16)
    start = 0
    for o_ref in o_refs:
        for j in range(o_ref.shape[2] // tn):
            sl = slice(start + j * tn, start + (j + 1) * tn)
            o_ref[0, :, j * tn:(j + 1) * tn] = (_dot(u, w_ref[:, sl]) * cs_ref[:, sl]).astype(o_ref.dtype)
        start += o_ref.shape[2]


def _norm_mod_matmul(hs, g, sh, sc, w, col_scale, *, ncb, out_dtype, tn):
    bsz, lt, d = hs.shape
    n = w.shape[1]
    seg = _seg(ncb)
    multi = isinstance(out_dtype, tuple)
    parts = out_dtype if multi else ((out_dtype, n),)
    assert sum(c for _, c in parts) == n
    outs = pl.pallas_call(
        functools.partial(_nmm_kernel, tn=tn),
        grid=(bsz, lt // TM),
        in_specs=[
            pl.BlockSpec((1, TM, d), lambda b, i: (b, i, 0)),
            pl.BlockSpec((1, d), lambda b, i: (0, 0)),
            pl.BlockSpec((1, 1, 1, d), lambda b, i: (b, seg(i), 0, 0)),
            pl.BlockSpec((1, 1, 1, d), lambda b, i: (b, seg(i), 0, 0)),
            pl.BlockSpec((d, n), lambda b, i: (0, 0)),
            pl.BlockSpec((1, n), lambda b, i: (0, 0)),
        ],
        out_specs=[pl.BlockSpec((1, TM, c), lambda b, i: (b, i, 0)) for _, c in parts],
        out_shape=[jax.ShapeDtypeStruct((bsz, lt, c), dt) for dt, c in parts],
        compiler_params=_cparams("parallel", "parallel"),
        name="norm_mod_matmul",
    )(hs, g.reshape(1, d), sh, sc, w, col_scale)
    return tuple(outs) if multi else outs[0]


def _mgr_kernel(a_ref, w_ref, x_ref, gate_ref, o_ref):
    nbatch, tm, k = a_ref.shape
    y = _dot(a_ref[...].reshape(nbatch * tm, k), w_ref[...])
    for bb in range(nbatch):
        o_ref[bb] = x_ref[bb] + gate_ref[bb, 0] * y[bb * tm:(bb + 1) * tm]


def _matmul_gate_residual(a, w, hs, gate, *, ncb):
    bsz, lt, d = hs.shape
    k = a.shape[2]
    seg = _seg(ncb)
    bpair = 2 if bsz % 2 == 0 else 1
    return pl.pallas_call(
        _mgr_kernel,
        grid=(bsz // bpair, lt // TM),
        in_specs=[
            pl.BlockSpec((bpair, TM, k), lambda b, i: (b, i, 0)),
            pl.BlockSpec((k, d), lambda b, i: (0, 0)),
            pl.BlockSpec((bpair, TM, d), lambda b, i: (b, i, 0)),
            pl.BlockSpec((bpair, 1, 1, d), lambda b, i: (b, seg(i), 0, 0)),
        ],
        out_specs=pl.BlockSpec((bpair, TM, d), lambda b, i: (b, i, 0)),
        out_shape=jax.ShapeDtypeStruct((bsz, lt, d), F32),
        compiler_params=_cparams("parallel", "parallel"),
        name="matmul_gate_residual",
    )(a, w, hs, gate)


FFN_HALO = 16
FFN_PAD = 8


def _ffn_kernel(x_ref, xp_ref, xn_ref, g_ref, sh_ref, sc_ref, gate_ref, wup_ref, cw_ref, cb_ref,
                wdn_ref, fg_ref, o_ref, u_scr, h_scr, *, ncb, nb, off, final):
    i = pl.program_id(1) + off
    g = g_ref[...]
    sh = sh_ref[0, 0]
    sc = sc_ref[0, 0]
    f = wdn_ref.shape[0]
    keep_prev = jnp.where((i == 0) | (i == ncb), 0.0, 1.0)
    keep_next = jnp.where((i == ncb - 1) | (i == nb - 1), 0.0, 1.0)
    u_prev = _norm_mod(xp_ref[0], g, sh, sc)[FFN_HALO - 1:FFN_HALO] * keep_prev
    u_next = _norm_mod(xn_ref[0], g, sh, sc)[0:1] * keep_next
    row = lax.broadcasted_iota(jnp.int32, (FFN_HALO, u_prev.shape[1]), 0)
    extra = jnp.where(row == 0, u_prev, jnp.where(row == 1, u_next, 0.0))
    u_scr[0:TM, :] = _norm_mod(x_ref[0], g, sh, sc).astype(BF16)
    u_scr[TM:, :] = extra.astype(BF16)
    h = _dot(u_scr[...], wup_ref[...])
    h_scr[FFN_PAD:FFN_PAD + TM, :] = h[0:TM]
    h_scr[FFN_PAD - 1:FFN_PAD, :] = h[TM:TM + 1]
    h_scr[FFN_PAD + TM:FFN_PAD + TM + 1, :] = h[TM + 1:TM + 2]
    w = cw_ref[...]
    hc = (h_scr[FFN_PAD - 1:FFN_PAD - 1 + TM, :] * w[0:1] + h_scr[FFN_PAD:FFN_PAD + TM, :] * w[1:2]
          + h_scr[FFN_PAD + 1:FFN_PAD + 1 + TM, :] * w[2:3] + cb_ref[...])
    gg = hc[:, f:]
    act = (hc[:, :f] * (gg * _sigmoid(gg))).astype(BF16)
    y = x_ref[0] + gate_ref[0, 0] * _dot(act, wdn_ref[...])
    if final:
        y = y * lax.rsqrt(jnp.mean(y * y, axis=-1, keepdims=True) + EPS) * fg_ref[...]
    o_ref[0] = y


def _conv_ffn(hs, g, sh, sc, gate, w_up, conv_w, conv_b, w_down, final_g=None, *, ncb):
    bsz, lt, d = hs.shape
    f = w_down.shape[0]
    nb = lt // TM
    final = final_g is not None
    off = ncb if final else 0
    hb = TM // FFN_HALO
    nhalo = lt // FFN_HALO
    seg = _seg(ncb)
    fg = (final_g if final else jnp.ones((d,), F32)).reshape(1, d)
    return pl.pallas_call(
        functools.partial(_ffn_kernel, ncb=ncb, nb=nb, off=off, final=final),
        grid=(bsz, nb - off),
        in_specs=[
            pl.BlockSpec((1, TM, d), lambda b, i: (b, i + off, 0)),
            pl.BlockSpec((1, FFN_HALO, d), lambda b, i: (b, jnp.maximum((i + off) * hb - 1, 0), 0)),
            pl.BlockSpec((1, FFN_HALO, d), lambda b, i: (b, jnp.minimum((i + off + 1) * hb, nhalo - 1), 0)),
            pl.BlockSpec((1, d), lambda b, i: (0, 0)),
            pl.BlockSpec((1, 1, 1, d), lambda b, i: (b, seg(i + off), 0, 0)),
            pl.BlockSpec((1, 1, 1, d), lambda b, i: (b, seg(i + off), 0, 0)),
            pl.BlockSpec((1, 1, 1, d), lambda b, i: (b, seg(i + off), 0, 0)),
            pl.BlockSpec((d, 2 * f), lambda b, i: (0, 0)),
            pl.BlockSpec((3, 2 * f), lambda b, i: (0, 0)),
            pl.BlockSpec((1, 2 * f), lambda b, i: (0, 0)),
            pl.BlockSpec((f, d), lambda b, i: (0, 0)),
            pl.BlockSpec((1, d), lambda b, i: (0, 0)),
        ],
        out_specs=pl.BlockSpec((1, TM, d), lambda b, i: (b, i, 0)),
        out_shape=jax.ShapeDtypeStruct((bsz, lt - off * TM, d), F32),
        scratch_shapes=[
            pltpu.VMEM((TM + FFN_HALO, d), BF16),
            pltpu.VMEM((TM + 2 * FFN_PAD, 2 * f), F32),
        ],
        compiler_params=_cparams("parallel", "parallel"),
        name="conv_ffn",
    )(hs, hs, hs, g.reshape(1, d), sh, sc, gate, w_up.astype(BF16), conv_w, conv_b.reshape(1, 2 * f),
      w_down.astype(BF16), fg)


def _rope_tables(lc, l):
    half = MLA_ROPE // 4
    inv = 1.0 / (ROPE_THETA ** (jnp.arange(half, dtype=F32) / half))
    t = jnp.arange(l)
    ang_r = (t // GRID_W).astype(F32)[:, None] * inv[None, :]
    ang_c = (t % GRID_W).astype(F32)[:, None] * inv[None, :]
    cos = jnp.concatenate([jnp.cos(ang_r), jnp.cos(ang_r), jnp.cos(ang_c), jnp.cos(ang_c)], axis=-1)
    sin = jnp.concatenate([-jnp.sin(ang_r), jnp.sin(ang_r), -jnp.sin(ang_c), jnp.sin(ang_c)], axis=-1)
    pad = ((lc, 0), (MLA_NOPE, LANES - MLA_QK))
    return jnp.pad(cos, pad, constant_values=1.0), jnp.pad(sin, pad)


def _mla_proj_kernel(x_ref, g_ref, sh_ref, sc_ref, win_ref, qg_ref, wq_ref, kvg_ref, wkn_ref, wv_ref,
                     cos_ref, sin_ref, q_ref, k_ref, v_ref):
    u = _norm_mod(x_ref[0], g_ref[...], sh_ref[0, 0], sc_ref[0, 0]).astype(BF16)
    lat = _dot(u, win_ref[...])
    q_lat = lat[:, :MLA_Q_RANK]
    kv_lat = lat[:, MLA_Q_RANK:MLA_Q_RANK + MLA_KV_RANK]
    kr = lat[:, MLA_Q_RANK + MLA_KV_RANK:]

    def rms(t, gg):
        return (t * lax.rsqrt(jnp.mean(t * t, axis=-1, keepdims=True) + EPS) * gg).astype(BF16)

    qn = rms(q_lat, qg_ref[...])
    kvn = rms(kv_lat, kvg_ref[...])
    cos = cos_ref[...]
    sin = sin_ref[...]
    lane = lax.broadcasted_iota(jnp.int32, (TM, LANES), 1)
    first_half = (lane % (MLA_ROPE // 2)) < (MLA_ROPE // 4)

    def rope(t):
        partner = jnp.where(first_half, pltpu.roll(t, LANES - MLA_ROPE // 4, 1), pltpu.roll(t, MLA_ROPE // 4, 1))
        return t * cos + partner * sin

    kr_rot = rope(kr)
    v_all = _dot(kvn, wv_ref[...]).astype(BF16)
    ones = jnp.ones((v_all.shape[0], LANES), BF16)
    for pr in range(MLA_HEADS // 2):
        v_ref[0, :, 2 * pr * LANES:(2 * pr + 1) * LANES] = v_all[:, pr * LANES:(pr + 1) * LANES]
        v_ref[0, :, (2 * pr + 1) * LANES:(2 * pr + 2) * LANES] = ones
    q_all = _dot(qn, wq_ref[...]) * ((MLA_QK ** -0.5) * LOG2E)
    kn_all = _dot(kvn, wkn_ref[...])
    for h in range(MLA_HEADS):
        sl = slice(h * LANES, (h + 1) * LANES)
        q_ref[0, :, sl] = rope(q_all[:, sl]).astype(BF16)
        k_ref[0, :, sl] = (kn_all[:, sl] + kr_rot).astype(BF16)


def _mla_attn_kernel(q_ref, k_ref, v_ref, o_ref, *, ncb, lc):
    i = pl.program_id(2)

    def attend(lk):
        q = q_ref[0]
        lane = lax.broadcasted_iota(jnp.int32, (q.shape[0], LANES), 1)
        for pair in range(q.shape[1] // (2 * LANES)):
            v = v_ref[0, 0:lk, 2 * pair * LANES:(2 * pair + 2) * LANES]
            outs = []
            for e in range(2):
                sl = slice((2 * pair + e) * LANES, (2 * pair + e + 1) * LANES)
                s = _dot_nt(q[:, sl], k_ref[0, 0:lk, sl])
                m = jnp.max(s, axis=-1, keepdims=True)
                p = jnp.exp2(s - m)
                pvl = _dot(p.astype(BF16), v)
                outs.append(pvl[:, :LANES] / pvl[:, LANES:])
            o_ref[0, :, pair * LANES:(pair + 1) * LANES] = jnp.where(lane < MLA_V, outs[0], outs[1]).astype(o_ref.dtype)

    @pl.when(i < ncb)
    def _():
        attend(lc)

    @pl.when(i >= ncb)
    def _():
        attend(k_ref.shape[1])


def _mla_layer(hs, g, sh, sc, gate, w_in, q_norm_g, w_q_up, kv_norm_g, w_kv_up, w_out, *, ncb, lc):
    bsz, lt, d = hs.shape
    l = lt - lc
    nh = MLA_HEADS
    seg = _seg(ncb)
    n_lat = MLA_Q_RANK + MLA_KV_RANK
    w_in_p = jnp.concatenate([w_in[:, :n_lat], jnp.zeros((d, MLA_NOPE), w_in.dtype), w_in[:, n_lat:],
                              jnp.zeros((d, LANES - MLA_QK), w_in.dtype)], axis=1)
    wq = w_q_up.reshape(MLA_Q_RANK, nh, MLA_QK)
    wq = jnp.pad(wq, ((0, 0), (0, 0), (0, LANES - MLA_QK))).reshape(MLA_Q_RANK, nh * LANES)
    wkv = w_kv_up.reshape(MLA_KV_RANK, nh, MLA_NOPE + MLA_V)
    wkn = jnp.pad(wkv[:, :, :MLA_NOPE], ((0, 0), (0, 0), (0, LANES - MLA_NOPE))).reshape(MLA_KV_RANK, nh * LANES)
    wv = wkv[:, :, MLA_NOPE:].reshape(MLA_KV_RANK, nh * MLA_V)
    cos_t, sin_t = _rope_tables(lc, l)

    q, k, v = pl.pallas_call(
        _mla_proj_kernel,
        grid=(bsz, lt // TM),
        in_specs=[
            pl.BlockSpec((1, TM, d), lambda b, i: (b, i, 0)),
            pl.BlockSpec((1, d), lambda b, i: (0, 0)),
            pl.BlockSpec((1, 1, 1, d), lambda b, i: (b, seg(i), 0, 0)),
            pl.BlockSpec((1, 1, 1, d), lambda b, i: (b, seg(i), 0, 0)),
            pl.BlockSpec((d, 768), lambda b, i: (0, 0)),
            pl.BlockSpec((1, MLA_Q_RANK), lambda b, i: (0, 0)),
            pl.BlockSpec((MLA_Q_RANK, nh * LANES), lambda b, i: (0, 0)),
            pl.BlockSpec((1, MLA_KV_RANK), lambda b, i: (0, 0)),
            pl.BlockSpec((MLA_KV_RANK, nh * LANES), lambda b, i: (0, 0)),
            pl.BlockSpec((MLA_KV_RANK, nh * MLA_V), lambda b, i: (0, 0)),
            pl.BlockSpec((TM, LANES), lambda b, i: (i, 0)),
            pl.BlockSpec((TM, LANES), lambda b, i: (i, 0)),
        ],
        out_specs=[
            pl.BlockSpec((1, TM, nh * LANES), lambda b, i: (b, i, 0)),
            pl.BlockSpec((1, TM, nh * LANES), lambda b, i: (b, i, 0)),
            pl.BlockSpec((1, TM, nh * LANES), lambda b, i: (b, i, 0)),
        ],
        out_shape=[
            jax.ShapeDtypeStruct((bsz, lt, nh * LANES), BF16),
            jax.ShapeDtypeStruct((bsz, lt, nh * LANES), BF16),
            jax.ShapeDtypeStruct((bsz, lt, nh * LANES), BF16),
        ],
        compiler_params=_cparams("parallel", "parallel"),
        name="mla_proj",
    )(hs, g.reshape(1, d), sh, sc, w_in_p.astype(BF16), q_norm_g.reshape(1, -1), wq.astype(BF16),
      kv_norm_g.reshape(1, -1), wkn.astype(BF16), wv.astype(BF16), cos_t, sin_t)

    o = pl.pallas_call(
        functools.partial(_mla_attn_kernel, ncb=ncb, lc=lc),
        grid=(bsz, nh // MLA_HEADS_PER_STEP, lt // TM),
        in_specs=[
            pl.BlockSpec((1, TM, MLA_HEADS_PER_STEP * LANES), lambda b, p, i: (b, i, p)),
            pl.BlockSpec((1, lt, MLA_HEADS_PER_STEP * LANES), lambda b, p, i: (b, 0, p)),
            pl.BlockSpec((1, lt, MLA_HEADS_PER_STEP * LANES), lambda b, p, i: (b, 0, p)),
        ],
        out_specs=pl.BlockSpec((1, TM, MLA_HEADS_PER_STEP * MLA_V), lambda b, p, i: (b, i, p)),
        out_shape=jax.ShapeDtypeStruct((bsz, lt, nh * MLA_V), BF16),
        compiler_params=_cparams("parallel", "parallel", "parallel"),
        name="mla_attn",
    )(q, k, v)
    return _matmul_gate_residual(o, w_out.astype(BF16), hs, gate, ncb=ncb)


def _s5_matrices(lam_re, lam_im, log_dt, b_re, b_im, c_re, c_im):
    t_len = S5_T
    n_grp = lam_re.shape[1]
    ch = S5_GROUP_CH
    n_st = S5_STATE
    dt = jnp.exp(log_dt.astype(F32))[..., None]
    ld_re = lam_re.astype(F32) * dt
    ld_im = lam_im.astype(F32) * dt
    tau = jnp.arange(t_len + 1, dtype=F32)[:, None, None, None]
    mag = jnp.exp(tau * ld_re[None])
    pw_re = mag * jnp.cos(tau * ld_im[None])
    pw_im = mag * jnp.sin(tau * ld_im[None])
    tau_r = jnp.arange(t_len, -1, -1, dtype=F32)[:, None, None, None]
    mag_r = jnp.exp(tau_r * ld_re[None])
    pr_re = mag_r * jnp.cos(tau_r * ld_im[None])
    pr_im = mag_r * jnp.sin(tau_r * ld_im[None])
    lb_re, lb_im = pw_re[1] - 1.0, pw_im[1]
    den = lam_re * lam_re + lam_im * lam_im
    f_re = (lb_re * lam_re + lb_im * lam_im) / den
    f_im = (lb_im * lam_re - lb_re * lam_im) / den
    bb_re = f_re[..., None] * b_re - f_im[..., None] * b_im
    bb_im = f_re[..., None] * b_im + f_im[..., None] * b_re
    cp_re = c_re[None] * pw_re[:t_len, :, :, None, :] - c_im[None] * pw_im[:t_len, :, :, None, :]
    cp_im = c_re[None] * pw_im[:t_len, :, :, None, :] + c_im[None] * pw_re[:t_len, :, :, None, :]
    taps = (jnp.einsum("tdgon,dgni->tdgoi", cp_re, bb_re, precision=HI)
            - jnp.einsum("tdgon,dgni->tdgoi", cp_im, bb_im, precision=HI))
    s_idx = np.arange(t_len)[:, None, None]
    t_idx = np.arange(t_len)[None, :, None]
    u_idx = np.arange(t_len)[None, None, :]
    sel_f = jnp.asarray((t_idx - s_idx == u_idx).astype(np.float32))
    sel_b = jnp.asarray((s_idx - t_idx == u_idx).astype(np.float32))
    kf = jnp.einsum("stu,ugoi->stgoi", sel_f, taps[:, 0], precision=HI)
    kb = jnp.einsum("stu,ugoi->stgoi", sel_b, taps[:, 1], precision=HI)
    k_tot = (kf + kb).transpose(2, 0, 4, 1, 3).reshape(n_grp, t_len * ch, t_len * ch)

    def state_in(pw_r, pw_i, brr, bii):
        re = pw_r[..., None] * brr[None] - pw_i[..., None] * bii[None]
        im = pw_r[..., None] * bii[None] + pw_i[..., None] * brr[None]
        return (re.transpose(1, 0, 3, 2).reshape(n_grp, t_len * ch, n_st),
                im.transpose(1, 0, 3, 2).reshape(n_grp, t_len * ch, n_st))

    inf_re, inf_im = state_in(pr_re[1:, 0], pr_im[1:, 0], bb_re[0], bb_im[0])
    inb_re, inb_im = state_in(pw_re[:t_len, 1], pw_im[:t_len, 1], bb_re[1], bb_im[1])

    def state_out(pw_r, pw_i, crr, cii):
        re = crr[None] * pw_r[:, :, None, :] - cii[None] * pw_i[:, :, None, :]
        im = crr[None] * pw_i[:, :, None, :] + cii[None] * pw_r[:, :, None, :]
        return (re.transpose(1, 3, 0, 2).reshape(n_grp, n_st, t_len * ch),
                (-im).transpose(1, 3, 0, 2).reshape(n_grp, n_st, t_len * ch))

    outf_re, outf_im = state_out(pw_re[1:, 0], pw_im[1:, 0], c_re[0], c_im[0])
    outb_re, outb_im = state_out(pr_re[:t_len, 1], pr_im[:t_len, 1], c_re[1], c_im[1])
    nq = n_grp // S5_QGRP
    npair = n_grp // 2
    cols = t_len * ch
    x = jnp.stack([inf_re, inf_im, inb_re, inb_im], 0).reshape(4, npair, 2, cols, n_st)
    x = x.transpose(1, 2, 3, 0, 4)
    zx = jnp.zeros_like(x[:, 0])
    w_pair = jnp.concatenate([jnp.concatenate([x[:, 0], zx], -1), jnp.concatenate([zx, x[:, 1]], -1)], 1)
    w_pair = w_pair.reshape(nq, S5_QGRP // 2, 2 * cols, 4 * 2 * n_st)
    y = jnp.stack([outf_re, outf_im, outb_re, outb_im], 0).reshape(4, npair, 2, n_st, cols)
    y = y.transpose(1, 0, 2, 3, 4)
    zy = jnp.zeros_like(y[:, :, 0])
    m_pair = jnp.stack([jnp.concatenate([y[:, :, 0], zy], -1), jnp.concatenate([zy, y[:, :, 1]], -1)], 2)
    m_pair = m_pair.reshape(nq, S5_QGRP // 2, 4 * 2 * n_st, 2 * cols)
    decay = jnp.stack([pw_re[t_len, 0], pw_im[t_len, 0], pw_re[t_len, 1], pw_im[t_len, 1]], 0)
    decay = decay.reshape(4, nq, S5_QGRP * n_st).transpose(1, 0, 2).reshape(nq, 1, 4 * S5_QGRP * n_st)
    k_grp = k_tot.reshape(nq, S5_QGRP, cols, cols)
    return w_pair.astype(BF16), k_grp.astype(BF16), m_pair.astype(BF16), decay


def _s5_permutation():
    n = S5_T * LANES
    src = jnp.arange(n)
    t, g, c = src // LANES, (src % LANES) // S5_GROUP_CH, src % S5_GROUP_CH
    dst = g * (S5_T * S5_GROUP_CH) + t * S5_GROUP_CH + c
    return (dst[:, None] == jnp.arange(n)[None, :]).astype(BF16)


def _s5_chunk_rows(z_ref):
    jt = z_ref.shape[1] // S5_T
    return jnp.concatenate([z_ref[0, pl.ds(t, jt, stride=S5_T), :].astype(BF16) for t in range(S5_T)], axis=1)


def _s5_core_kernel(z_ref, perm_ref, win_ref, kg_ref, mout_ref, a_ref, y_ref, u_scr, p_scr, *, jc, jt):
    sc = a_ref.shape[2] // 4
    npair = win_ref.shape[1]
    pc = win_ref.shape[2]
    gc = pc // 2
    zg = _dot(_s5_chunk_rows(z_ref), perm_ref[...]).astype(BF16)
    for pp in range(npair):
        res = _dot(zg[:, pp * pc:(pp + 1) * pc], win_ref[0, pp])
        for k in range(4):
            u_scr[:, k * sc + pp * LANES:k * sc + (pp + 1) * LANES] = res[:, k * LANES:(k + 1) * LANES]
    a = a_ref[0]
    afr, afi, abr, abi = (a[:, k * sc:(k + 1) * sc] for k in range(4))
    zero = jnp.zeros((1, sc), F32)

    def body(jj, carry):
        fr, fi, br, bi = carry
        jb = jnp.where(jj < jc, jc - 1 - jj, jt - 1 - jj + jc)
        p_scr[pl.ds(jj, 1), 0:sc] = fr
        p_scr[pl.ds(jj, 1), sc:2 * sc] = fi
        p_scr[pl.ds(jb, 1), 2 * sc:3 * sc] = br
        p_scr[pl.ds(jb, 1), 3 * sc:4 * sc] = bi
        nfr = afr * fr - afi * fi + u_scr[pl.ds(jj, 1), 0:sc]
        nfi = afr * fi + afi * fr + u_scr[pl.ds(jj, 1), sc:2 * sc]
        nbr = abr * br - abi * bi + u_scr[pl.ds(jb, 1), 2 * sc:3 * sc]
        nbi = abr * bi + abi * br + u_scr[pl.ds(jb, 1), 3 * sc:4 * sc]
        return nfr, nfi, nbr, nbi

    lax.fori_loop(0, jt, body, (zero, zero, zero, zero), unroll=S5_SCAN_UNROLL)
    pieces = []
    for pp in range(npair):
        p_in = jnp.concatenate([p_scr[:, k * sc + pp * LANES:k * sc + (pp + 1) * LANES] for k in range(4)], axis=1)
        ys = _dot(p_in.astype(BF16), mout_ref[0, pp])
        for e in range(2):
            g = 2 * pp + e
            yg = ys[:, e * gc:(e + 1) * gc] + _dot(zg[:, g * gc:(g + 1) * gc], kg_ref[0, g])
            pieces.append(yg.astype(BF16))
    acc = _dot_nt(jnp.concatenate(pieces, axis=1), perm_ref[...])
    for t in range(S5_T):
        y_ref[0, pl.ds(t, jt, stride=S5_T), :] = acc[:, t * LANES:(t + 1) * LANES]


def _s5_glu_kernel(y_ref, z_ref, d_ref, w_ref, x_ref, gate_ref, o_ref):
    d = x_ref.shape[2]
    y = y_ref[0] + z_ref[0] * d_ref[...]
    ge = jax.nn.gelu(y).astype(BF16)
    ag = _dot(ge, w_ref[...])
    o_ref[0] = x_ref[0] + gate_ref[0, 0] * (ag[:, :d] * _sigmoid(ag[:, d:]))


def _s5_layer(hs, g, sh, sc, gate, w_in, lam_re, lam_im, log_dt, b_re, b_im, c_re, c_im, d_skip, w_glu,
              *, ncb, lc):
    bsz, lt, d = hs.shape
    width = w_in.shape[1]
    nq = width // LANES
    jt = lt // S5_T
    jc = lc // S5_T
    ccols = S5_T * LANES
    scols = 4 * S5_QGRP * S5_STATE
    seg = _seg(ncb)
    z = _norm_mod_matmul(hs, g, sh, sc, w_in.astype(BF16), jnp.ones((1, width), F32),
                         ncb=ncb, out_dtype=F32, tn=width)
    w_pair, k_grp, m_pair, decay = _s5_matrices(lam_re, lam_im, log_dt, b_re, b_im, c_re, c_im)

    def per_quarter(arr):
        return pl.BlockSpec((1,) + arr.shape[1:], lambda q, b: (q, 0, 0, 0))

    y = pl.pallas_call(
        functools.partial(_s5_core_kernel, jc=jc, jt=jt),
        grid=(nq, bsz),
        in_specs=[
            pl.BlockSpec((1, lt, LANES), lambda q, b: (b, 0, q)),
            pl.BlockSpec((ccols, ccols), lambda q, b: (0, 0), pipeline_mode=pl.Buffered(1)),
            per_quarter(w_pair),
            per_quarter(k_grp),
            per_quarter(m_pair),
            pl.BlockSpec((1, 1, scols), lambda q, b: (q, 0, 0)),
        ],
        out_specs=pl.BlockSpec((1, lt, LANES), lambda q, b: (b, 0, q)),
        out_shape=jax.ShapeDtypeStruct((bsz, lt, width), F32),
        scratch_shapes=[pltpu.VMEM((jt, scols), F32), pltpu.VMEM((jt, scols), F32)],
        compiler_params=_cparams("parallel", "parallel"),
        name="s5_core",
    )(z, _s5_permutation(), w_pair, k_grp, m_pair, decay)
    return pl.pallas_call(
        _s5_glu_kernel,
        grid=(bsz, lt // TM),
        in_specs=[
            pl.BlockSpec((1, TM, width), lambda b, i: (b, i, 0)),
            pl.BlockSpec((1, TM, width), lambda b, i: (b, i, 0)),
            pl.BlockSpec((1, width), lambda b, i: (0, 0)),
            pl.BlockSpec((width, 2 * d), lambda b, i: (0, 0)),
            pl.BlockSpec((1, TM, d), lambda b, i: (b, i, 0)),
            pl.BlockSpec((1, 1, 1, d), lambda b, i: (b, seg(i), 0, 0)),
        ],
        out_specs=pl.BlockSpec((1, TM, d), lambda b, i: (b, i, 0)),
        out_shape=jax.ShapeDtypeStruct((bsz, lt, d), F32),
        compiler_params=_cparams("parallel", "parallel"),
        name="s5_glu",
    )(y, z, d_skip.reshape(1, width), w_glu.astype(BF16), hs, gate)


def _gla_dir(zq, zf, zv, lb, states, incl, tri, rev):
    dk = HG_HEAD_DIM
    width = zq.shape[1]
    nchunks = zq.shape[0] // HG_CHUNK
    forget = lb + (1.0 - lb) * _sigmoid(zf)
    lf = jnp.log(forget)
    kk = 1.0 - forget
    v_all = zv.astype(BF16)
    hi = lf.astype(BF16)
    r1 = lf - hi.astype(F32)
    mid = r1.astype(BF16)
    lo = (r1 - mid.astype(F32)).astype(BF16)
    parts = _dot(tri, jnp.concatenate([hi, mid, lo], axis=1))
    bcum = parts[:, :width] + parts[:, width:2 * width] + parts[:, 2 * width:]
    btot = jnp.concatenate(
        [jnp.broadcast_to(bcum[c * HG_CHUNK:c * HG_CHUNK + 1] if rev else bcum[(c + 1) * HG_CHUNK - 1:(c + 1) * HG_CHUNK],
                          (HG_CHUNK, width)) for c in range(nchunks)], axis=0)
    q_in_all = (zq * _sigmoid(zq)) * (dk ** -0.5) * jnp.exp(bcum)
    k_in_all = (kk * jnp.exp(-bcum)).astype(BF16)
    k_out_all = kk * jnp.exp(btot - bcum)
    chunk_of_row = lax.broadcasted_iota(jnp.int32, (zq.shape[0], dk), 0) // HG_CHUNK
    zero = jnp.zeros((zq.shape[0], dk), F32)
    outs, new_states = [], []
    for h, st in enumerate(states):
        sl = slice(h * dk, (h + 1) * dk)
        q_in, k_out, v = q_in_all[:, sl], k_out_all[:, sl], v_all[:, sl]
        att = jnp.where(incl, _dot_nt(q_in.astype(BF16), k_in_all[:, sl]), 0.0)
        o_intra = _dot(att.astype(BF16), v)
        k_cat = jnp.concatenate([jnp.where(chunk_of_row == c, k_out, zero) for c in range(nchunks)], axis=1)
        q_cat = jnp.concatenate([jnp.where(chunk_of_row == c, q_in, zero) for c in range(nchunks)], axis=1)
        ds = _dot_tn(v, k_cat.astype(BF16))
        entering = [None] * nchunks
        for cc in range(nchunks):
            c = nchunks - 1 - cc if rev else cc
            entering[c] = st
            st = st * jnp.exp(btot[c * HG_CHUNK:c * HG_CHUNK + 1, sl]) + ds[:, c * dk:(c + 1) * dk]
        scat = jnp.concatenate(entering, axis=1).astype(BF16)
        outs.append(o_intra + _dot_nt(q_cat.astype(BF16), scat))
        new_states.append(st)
    return outs, new_states


def _gla_kernel(zqf_ref, zff_ref, zvf_ref, zqb_ref, zfb_ref, zvb_ref, lb_ref, of_ref, ob_ref, st_scr):
    @pl.when(pl.program_id(2) == 0)
    def _():
        st_scr[...] = jnp.zeros_like(st_scr)

    n = zqf_ref.shape[1]
    row = lax.broadcasted_iota(jnp.int32, (n, n), 0)
    col = lax.broadcasted_iota(jnp.int32, (n, n), 1)
    same = (row // HG_CHUNK) == (col // HG_CHUNK)
    dk = HG_HEAD_DIM
    for direction, (zq_ref, zf_ref, zv_ref, o_ref) in enumerate(
            ((zqf_ref, zff_ref, zvf_ref, of_ref), (zqb_ref, zfb_ref, zvb_ref, ob_ref))):
        rev = direction == 1
        incl = same & ((col >= row) if rev else (col <= row))
        tri = incl.astype(BF16)
        nheads = zq_ref.shape[2] // dk
        outs, states = _gla_dir(zq_ref[0].astype(F32), zf_ref[0], zv_ref[0], lb_ref[0],
                                [st_scr[direction, h] for h in range(nheads)], incl, tri, rev)
        for h in range(nheads):
            o_ref[0, :, h * dk:(h + 1) * dk] = outs[h].astype(o_ref.dtype)
            st_scr[direction, h] = states[h]


def _hg_out_kernel(of_ref, ob_ref, zg_ref, ng_ref, w_ref, x_ref, gate_ref, o_ref):
    o = of_ref[0].astype(F32) + ob_ref[0].astype(F32)
    gsig = zg_ref[0].astype(F32)
    gsig = gsig * _sigmoid(gsig)
    ng = ng_ref[...]
    parts = []
    for h in range(o.shape[1] // HG_HEAD_DIM):
        sl = slice(h * HG_HEAD_DIM, (h + 1) * HG_HEAD_DIM)
        oh = o[:, sl]
        on = oh * lax.rsqrt(jnp.mean(oh * oh, axis=-1, keepdims=True) + EPS)
        parts.append((on * ng[:, sl] * gsig[:, sl]).astype(BF16))
    a = jnp.concatenate(parts, axis=-1)
    o_ref[0] = x_ref[0] + gate_ref[0, 0] * _dot(a, w_ref[...])


def _hgrn2_layer(hs, g, sh, sc, gate, w_in, lower_bound, norm_g, w_out, *, ncb, lc):
    bsz, lt, d = hs.shape
    nh = d // HG_HEAD_DIM
    nb = lt // TM
    seg = _seg(ncb)
    w_r = jnp.concatenate([w_in[:, d:3 * d], w_in[:, :d], w_in[:, 3 * d:]], axis=1).astype(BF16)
    zf, zr = _norm_mod_matmul(hs, g, sh, sc, w_r, jnp.ones((1, 5 * d), F32),
                              ncb=ncb, out_dtype=((F32, 2 * d), (BF16, 3 * d)), tn=1024)
    lb = lower_bound.astype(F32).reshape(1, 1, d)
    hps = HG_HEADS_PER_STEP
    ng = nh // hps
    wb = hps * HG_HEAD_DIM

    def rblk(s):
        return jnp.where(s < ncb, ncb - 1 - s, nb - 1 - s + ncb)

    outs = pl.pallas_call(
        _gla_kernel,
        grid=(bsz, ng, nb),
        in_specs=[
            pl.BlockSpec((1, TM, wb), lambda b, h, s: (b, s, h)),
            pl.BlockSpec((1, TM, wb), lambda b, h, s: (b, s, h)),
            pl.BlockSpec((1, TM, wb), lambda b, h, s: (b, s, ng + h)),
            pl.BlockSpec((1, TM, wb), lambda b, h, s: (b, rblk(s), h)),
            pl.BlockSpec((1, TM, wb), lambda b, h, s: (b, rblk(s), ng + h)),
            pl.BlockSpec((1, TM, wb), lambda b, h, s: (b, rblk(s), ng + h)),
            pl.BlockSpec((1, 1, wb), lambda b, h, s: (0, 0, h)),
        ],
        out_specs=[
            pl.BlockSpec((1, TM, wb), lambda b, h, s: (b, s, h)),
            pl.BlockSpec((1, TM, wb), lambda b, h, s: (b, rblk(s), h)),
        ],
        out_shape=[jax.ShapeDtypeStruct((bsz, lt, d), BF16), jax.ShapeDtypeStruct((bsz, lt, d), BF16)],
        scratch_shapes=[pltpu.VMEM((2, hps, HG_HEAD_DIM, HG_HEAD_DIM), F32)],
        compiler_params=_cparams("parallel", "parallel", "arbitrary"),
        name="hgrn2_gla",
    )(zr, zf, zr, zr, zf, zr, lb)
    return pl.pallas_call(
        _hg_out_kernel,
        grid=(bsz, nb),
        in_specs=[
            pl.BlockSpec((1, TM, d), lambda b, i: (b, i, 0)),
            pl.BlockSpec((1, TM, d), lambda b, i: (b, i, 0)),
            pl.BlockSpec((1, TM, d), lambda b, i: (b, i, 2)),
            pl.BlockSpec((1, d), lambda b, i: (0, 0)),
            pl.BlockSpec((d, d), lambda b, i: (0, 0)),
            pl.BlockSpec((1, TM, d), lambda b, i: (b, i, 0)),
            pl.BlockSpec((1, 1, 1, d), lambda b, i: (b, seg(i), 0, 0)),
        ],
        out_specs=pl.BlockSpec((1, TM, d), lambda b, i: (b, i, 0)),
        out_shape=jax.ShapeDtypeStruct((bsz, lt, d), F32),
        compiler_params=_cparams("parallel", "parallel"),
        name="hgrn2_out",
    )(outs[0], outs[1], zr, norm_g.reshape(1, d), w_out.astype(BF16), hs, gate)


def _na_bias_table(rpb):
    w = np.arange(GRID_W)[:, None, None]
    kc = np.arange(GRID_W)[None, :, None]
    co = np.arange(2 * NA_KW - 1)[None, None, :]
    c0 = np.clip(w - NA_KW // 2, 0, GRID_W - NA_KW)
    valid = (kc >= c0) & (kc < c0 + NA_KW)
    onehot = jnp.asarray((valid & (kc - w + (NA_KW - 1) == co)).astype(np.float32))
    t = jnp.einsum("hrc,wkc->hrwk", rpb.astype(F32), onehot, precision=HI) * LOG2E
    t = jnp.where(jnp.asarray(valid[None, None, :, :, 0]), t, NEG_BIG)
    ext = jnp.pad(t, ((0, 0), (NA_KH, NA_KH), (0, 0), (0, 0)), constant_values=NEG_BIG)
    return jnp.concatenate([ext[:, :-1], ext[:, 1:]], axis=-1)


def _na_kernel(q_ref, k_ref, v_ref, bias_ref, o_ref, vx_scr, *, lc, rows):
    nblk = rows // NA_BLK_ROWS
    nq = NA_BLK_ROWS * GRID_W
    nwin = NA_WIN_ROWS * GRID_W
    o_ref[0, 0:lc, :] = jnp.zeros((lc, LANES), o_ref.dtype)
    vx_scr[:, 0:LANES] = v_ref[0]
    vx_scr[:, LANES:] = jnp.ones((vx_scr.shape[0], LANES), vx_scr.dtype)
    kc = k_ref[0, 0:lc, :]
    vc = vx_scr[0:lc, :]
    lane = lax.broadcasted_iota(jnp.int32, (nq, LANES), 1)
    lo_half = lane < NA_HEAD_DIM
    key_lane = lax.broadcasted_iota(jnp.int32, (GRID_W, nwin), 1)
    half = NA_KH // 2

    def one_block(i):
        wr0 = jnp.clip(i * NA_BLK_ROWS - half, 0, rows - NA_WIN_ROWS)
        first, last = i == 0, i == nblk - 1
        delta0 = jnp.where(first, NA_KH - 1, jnp.where(last, -1, NA_KH - 1 - half))
        qoff = pl.multiple_of(lc + i * nq, GRID_W * NA_KH // 2)
        woff = pl.multiple_of(lc + wr0 * GRID_W, GRID_W * NA_KH // 2)
        q = q_ref[0, pl.ds(qoff, nq), :]
        kw = k_ref[0, pl.ds(woff, nwin), :]
        vw = vx_scr[pl.ds(woff, nwin), :]
        zq = jnp.zeros_like(q)
        outs = []
        for e in range(2):
            qe = jnp.where(lo_half, q, zq) if e == 0 else jnp.where(lo_half, zq, q)
            s_all = _dot_nt(qe, kw)
            slabs = []
            for j in range(NA_BLK_ROWS):
                a = jnp.where(first, max(j - half, 0), jnp.where(last, min(j + half, NA_KH), j))
                valid = (key_lane >= a * GRID_W) & (key_lane < (a + NA_KH) * GRID_W)
                bias = jnp.concatenate([bias_ref[e, delta0 - j + NA_KH + 2 * m] for m in range(NA_WIN_ROWS // 2)],
                                       axis=1)
                slabs.append(jnp.where(valid, s_all[j * GRID_W:(j + 1) * GRID_W] + bias, NEG_BIG))
            s_loc = jnp.concatenate(slabs, axis=0)
            s_ctx = _dot_nt(qe, kc)
            m = jnp.maximum(jnp.max(s_loc, axis=-1, keepdims=True), jnp.max(s_ctx, axis=-1, keepdims=True))
            p_loc = jnp.exp2(s_loc - m)
            p_ctx = jnp.exp2(s_ctx - m)
            pvl = _dot(p_loc.astype(BF16), vw) + _dot(p_ctx.astype(BF16), vc)
            outs.append(pvl[:, :LANES] / pvl[:, LANES:])
        o_ref[0, pl.ds(qoff, nq), :] = jnp.where(lo_half, outs[0], outs[1]).astype(o_ref.dtype)

    def body(ii, carry):
        for j in range(NA_BLK_UNROLL):
            one_block(ii * NA_BLK_UNROLL + j)
        return carry

    lax.fori_loop(0, nblk // NA_BLK_UNROLL, body, 0)


def _natten_layer(hs, g, sh, sc, gate, w_qkv, rpb, w_out, *, ncb, lc):
    bsz, lt, d = hs.shape
    rows = (lt - lc) // GRID_W
    assert rows >= NA_WIN_ROWS and rows % (NA_BLK_ROWS * NA_BLK_UNROLL) == 0
    npair = NA_HEADS // 2
    col_scale = jnp.concatenate([jnp.full((1, d), (NA_HEAD_DIM ** -0.5) * LOG2E, F32),
                                 jnp.ones((1, 2 * d), F32)], axis=-1)
    qkv = _norm_mod_matmul(hs, g, sh, sc, w_qkv.astype(BF16), col_scale, ncb=ncb, out_dtype=BF16, tn=1024)
    bias = _na_bias_table(rpb)
    o = pl.pallas_call(
        functools.partial(_na_kernel, lc=lc, rows=rows),
        grid=(npair, bsz),
        in_specs=[
            pl.BlockSpec((1, lt, LANES), lambda p, b: (b, 0, p)),
            pl.BlockSpec((1, lt, LANES), lambda p, b: (b, 0, npair + p)),
            pl.BlockSpec((1, lt, LANES), lambda p, b: (b, 0, 2 * npair + p)),
            pl.BlockSpec((2, NA_TABLE_PAIRS, GRID_W, 2 * GRID_W), lambda p, b: (p, 0, 0, 0)),
        ],
        out_specs=pl.BlockSpec((1, lt, LANES), lambda p, b: (b, 0, p)),
        out_shape=jax.ShapeDtypeStruct((bsz, lt, d), BF16),
        scratch_shapes=[pltpu.VMEM((lt, 2 * LANES), BF16)],
        compiler_params=_cparams("parallel", "parallel"),
        name="natten",
    )(qkv, qkv, qkv, bias)
    return _matmul_gate_residual(o, w_out.astype(BF16), hs, gate, ncb=ncb)


def _layer_mods(mod_rows, bsz, d):
    mx = mod_rows[:bsz].reshape(bsz, 6, d)
    mc = jnp.broadcast_to(mod_rows[bsz].reshape(1, 6, d), (bsz, 6, d))
    m = jnp.stack([mc, mx], axis=1)
    return [m[:, :, k, None, :] for k in range(6)]


def kernel(x, c, ctx, c_ctx, ada_w, ada_b, norm1_g, norm2_g, mla_w_in, mla_q_norm_g, mla_w_q_up, mla_kv_norm_g, mla_w_kv_up, mla_w_out, s5_w_in, s5_lambda_re, s5_lambda_im, s5_log_dt, s5_b_re, s5_b_im, s5_c_re, s5_c_im, s5_d, s5_w_glu, hg_w_in, hg_lower_bound, hg_norm_g, hg_w_out, na_w_qkv, na_rpb, na_w_out, ffn_w_up, ffn_conv_w, ffn_conv_b, ffn_w_down, final_g):
    bsz, l, d = x.shape
    lc = ctx.shape[1]
    depth = ada_w.shape[0]
    assert lc % TM == 0 and l % TM == 0 and l % GRID_W == 0 and bsz + 1 <= 16
    ncb = lc // TM
    hs = jnp.concatenate([ctx, x], axis=1)
    cond_rows = jnp.concatenate([c, c_ctx[None], jnp.zeros((16 - bsz - 1, d), F32)], axis=0)
    mod_all = _ada_mod(cond_rows, ada_w, ada_b)
    lb_cum = jnp.cumsum(jax.nn.softmax(hg_lower_bound.astype(F32), axis=0), axis=0)
    lower_bounds = lb_cum - lb_cum[0]
    for i in range(depth):
        kind, j = i % 4, i // 4
        sh1, sc1, g1, sh2, sc2, g2 = _layer_mods(mod_all[i], bsz, d)
        if kind == 0:
            hs = _mla_layer(hs, norm1_g[i], sh1, sc1, g1, mla_w_in[j], mla_q_norm_g[j], mla_w_q_up[j],
                            mla_kv_norm_g[j], mla_w_kv_up[j], mla_w_out[j], ncb=ncb, lc=lc)
        elif kind == 1:
            hs = _s5_layer(hs, norm1_g[i], sh1, sc1, g1, s5_w_in[j], s5_lambda_re[j], s5_lambda_im[j],
                           s5_log_dt[j], s5_b_re[j], s5_b_im[j], s5_c_re[j], s5_c_im[j], s5_d[j], s5_w_glu[j],
                           ncb=ncb, lc=lc)
        elif kind == 2:
            hs = _hgrn2_layer(hs, norm1_g[i], sh1, sc1, g1, hg_w_in[j], lower_bounds[i], hg_norm_g[j],
                              hg_w_out[j], ncb=ncb, lc=lc)
        else:
            hs = _natten_layer(hs, norm1_g[i], sh1, sc1, g1, na_w_qkv[j], na_rpb[j], na_w_out[j],
                               ncb=ncb, lc=lc)
        hs = _conv_ffn(hs, norm2_g[i], sh2, sc2, g2, ffn_w_up[i], ffn_conv_w[i], ffn_conv_b[i],
                       ffn_w_down[i], final_g if i == depth - 1 else None, ncb=ncb)
    return hs
```
